```python
import jax, jax.numpy as jnp
from jax import lax
import numpy as np

D_MODEL = 1024
BATCH = 8
SEQ = 4096
DEPTH = 1

HEAD_DIM = 64
ROPE_THETA = 10000.0
RMS_EPS = 1e-6
NEG_BIG = -1e30

SWA_HEADS = 8
SWA_KV_HEADS = 2
SWA_WINDOW = 128
SWA_BLOCK = 128

NSA_HEADS = 8
NSA_KV_HEADS = 2
CMP_LEN = 32
CMP_STRIDE = 16
CMP_HIDDEN = 256
SEL_BLOCK = 64
SEL_TOPN = 16
SEL_QCHUNK = 64
NSA_WINDOW = 512
NSA_BLOCK = 128
NSA_N_GATES = 3

KV_W = HEAD_DIM * SWA_KV_HEADS
NKV_W = HEAD_DIM * NSA_KV_HEADS
SPLIT_SIZES = (
    SWA_HEADS * HEAD_DIM, KV_W, KV_W,
    NSA_HEADS * HEAD_DIM,
    NKV_W, NKV_W, NKV_W, NKV_W, NKV_W, NKV_W,
    NSA_HEADS * NSA_N_GATES,
)
IN_WIDTH = sum(SPLIT_SIZES)
MIX_WIDTH = (SWA_HEADS + NSA_HEADS) * HEAD_DIM

PEER_HEADS = 8
N_KEYS = 128
N_EXPERTS = N_KEYS * N_KEYS
PEER_QDIM = 256
PEER_TOPK = 16
PEER_CHUNK = 128

kernel_name = "hymba_swa_sink_nsa_peer_block"


def rmsnorm(x, g):
    xf = x.astype(jnp.float32)
    y = xf * lax.rsqrt(jnp.mean(xf * xf, axis=-1, keepdims=True) + RMS_EPS)
    return (y * g.astype(jnp.float32)).astype(x.dtype)


def rope(x, positions):
    half = HEAD_DIM // 2
    inv_freq = ROPE_THETA ** (-jnp.arange(half, dtype=jnp.float32) / half)
    ang = positions.astype(jnp.float32)[..., None] * inv_freq
    cos = jnp.cos(ang)[:, :, None, :]
    sin = jnp.sin(ang)[:, :, None, :]
    xf = x.astype(jnp.float32)
    x1, x2 = xf[..., :half], xf[..., half:]
    return jnp.concatenate([x1 * cos - x2 * sin, x2 * cos + x1 * sin], axis=-1).astype(x.dtype)


def banded_gqa_attention(q, k, v, window, block, sinks=None):
    B, T, H, dh = q.shape
    Hkv = k.shape[2]
    G = H // Hkv
    nb = T // block
    nprev = -(-(window - 1) // block)
    span = (nprev + 1) * block
    pad = ((0, 0), (nprev * block, 0), (0, 0), (0, 0))
    kp = jnp.pad(k, pad)
    vp = jnp.pad(v, pad)
    kidx = (jnp.arange(nb) * block)[:, None] + jnp.arange(span)[None, :]
    kb = kp[:, kidx]
    vb = vp[:, kidx]
    qb = q.reshape(B, nb, block, Hkv, G, dh)
    s = jnp.einsum('bnqhgd,bnkhd->bnhgqk', qb, kb).astype(jnp.float32) * (dh ** -0.5)
    qpos = (jnp.arange(nb) * block)[:, None] + jnp.arange(block)[None, :]
    kpos = kidx - nprev * block
    diff = qpos[:, :, None] - kpos[:, None, :]
    mask = (diff >= 0) & (diff < window) & (kpos[:, None, :] >= 0)
    s = jnp.where(mask[None, :, None, None], s, -jnp.inf)
    if sinks is None:
        p = jax.nn.softmax(s, axis=-1)
    else:
        sink = sinks.astype(jnp.float32).reshape(Hkv, G)[None, None, :, :, None, None]
        sink = jnp.broadcast_to(sink, s.shape[:-1] + (1,))
        p = jax.nn.softmax(jnp.concatenate([s, sink], axis=-1), axis=-1)[..., :-1]
    o = jnp.einsum('bnhgqk,bnkhd->bnqhgd', p.astype(v.dtype), vb)
    return o.reshape(B, T, H, dh)


def compress_blocks(kv, pos_emb, w1, w2):
    B, T, Hkv, dh = kv.shape
    nc = (T - CMP_LEN) // CMP_STRIDE + 1
    idx = (jnp.arange(nc) * CMP_STRIDE)[:, None] + jnp.arange(CMP_LEN)[None, :]
    blocks = kv[:, idx] + pos_emb[None, None, :, None, :]
    flat = blocks.transpose(0, 1, 3, 2, 4).reshape(B, nc, Hkv, CMP_LEN * dh)
    return jax.nn.gelu(flat @ w1, approximate=False) @ w2


def nsa_compressed_attention(q5, k_cmp, v_cmp):
    T = q5.shape[1]
    nc = k_cmp.shape[1]
    s = jnp.einsum('bthgd,bchd->bhgtc', q5, k_cmp).astype(jnp.float32) * (HEAD_DIM ** -0.5)
    vis = (jnp.arange(nc) * CMP_STRIDE + CMP_LEN - 1)[None, :] <= jnp.arange(T)[:, None]
    p = jnp.where(vis, jax.nn.softmax(jnp.where(vis, s, NEG_BIG), axis=-1), 0.0)
    o = jnp.einsum('bhgtc,bchd->bthgd', p.astype(v_cmp.dtype), v_cmp)
    return o, p


def nsa_select_blocks(p_cmp, T):
    nc = p_cmp.shape[-1]
    ns = T // SEL_BLOCK
    n_sel = min(SEL_TOPN, ns)
    cstart = jnp.arange(nc) * CMP_STRIDE
    bstart = jnp.arange(ns) * SEL_BLOCK
    lo = jnp.maximum(cstart[:, None], bstart[None, :])
    hi = jnp.minimum(cstart[:, None] + CMP_LEN, bstart[None, :] + SEL_BLOCK)
    overlap = jnp.clip(hi - lo, 0).astype(jnp.float32) / CMP_LEN
    imp = jnp.einsum('bhgtc,cj->bhtj', p_cmp, overlap)
    t = jnp.arange(T)[:, None]
    j = jnp.arange(ns)[None, :]
    cur = t // SEL_BLOCK
    valid = j * SEL_BLOCK <= t
    forced = (j == 0) | (j == cur) | (j == cur - 1)
    score = jnp.where(forced, jnp.inf, jnp.where(valid, imp, -jnp.inf))
    top, idx = lax.top_k(score, n_sel)
    return idx, top > -jnp.inf


def nsa_selected_attention(q5, k_sel, v_sel, sel_idx, sel_valid):
    B, T, Hkv, G, dh = q5.shape
    ns = T // SEL_BLOCK
    n_sel = sel_idx.shape[-1]
    C = SEL_QCHUNK
    nq = T // C
    kb = k_sel.reshape(B, ns, SEL_BLOCK, Hkv, dh).transpose(0, 3, 1, 2, 4)
    vb = v_sel.reshape(B, ns, SEL_BLOCK, Hkv, dh).transpose(0, 3, 1, 2, 4)
    qc = q5.reshape(B, nq, C, Hkv, G, dh).transpose(1, 0, 3, 4, 2, 5)
    ic = sel_idx.reshape(B, Hkv, nq, C, n_sel).transpose(2, 0, 1, 3, 4)
    vc = sel_valid.reshape(B, Hkv, nq, C, n_sel).transpose(2, 0, 1, 3, 4)
    t0 = jnp.arange(nq) * C
    bi = jnp.arange(B)[:, None, None, None]
    hi = jnp.arange(Hkv)[None, :, None, None]

    def step(args):
        q_c, i_c, v_c, start = args
        kg = kb[bi, hi, i_c]
        vg = vb[bi, hi, i_c]
        s = jnp.einsum('bhgqd,bhqnkd->bhgqnk', q_c, kg).astype(jnp.float32) * (dh ** -0.5)
        kpos = i_c[..., None] * SEL_BLOCK + jnp.arange(SEL_BLOCK)
        qpos = start + jnp.arange(C)
        mask = (kpos <= qpos[None, None, :, None, None]) & v_c[..., None]
        s = jnp.where(mask[:, :, None], s, -jnp.inf)
        p = jax.nn.softmax(s.reshape(B, Hkv, G, C, n_sel * SEL_BLOCK), axis=-1)
        p = p.reshape(B, Hkv, G, C, n_sel, SEL_BLOCK)
        return jnp.einsum('bhgqnk,bhqnkd->bqhgd', p.astype(vg.dtype), vg)

    out = lax.map(step, (qc, ic, vc, t0))
    return out.transpose(1, 0, 2, 3, 4, 5).reshape(B, T, Hkv, G, dh)


def peer_ffn(h, w_query, sub_keys_1, sub_keys_2, expert_down, expert_up):
    B, T, D = h.shape
    N = B * T
    ht = h.reshape(N, D)
    q = (ht @ w_query).reshape(N, PEER_HEADS, 2, PEER_QDIM // 2)
    s1 = jnp.einsum('nhd,kd->nhk', q[:, :, 0], sub_keys_1)
    s2 = jnp.einsum('nhd,kd->nhk', q[:, :, 1], sub_keys_2)
    t1, i1 = lax.top_k(s1, PEER_TOPK)
    t2, i2 = lax.top_k(s2, PEER_TOPK)
    cand = (t1[..., :, None] + t2[..., None, :]).reshape(N, PEER_HEADS, PEER_TOPK * PEER_TOPK)
    cidx = (i1[..., :, None] * N_KEYS + i2[..., None, :]).reshape(N, PEER_HEADS, PEER_TOPK * PEER_TOPK)
    sc, pos = lax.top_k(cand, PEER_TOPK)
    eidx = jnp.take_along_axis(cidx, pos, axis=-1)
    gate = jax.nn.softmax(sc.astype(jnp.float32), axis=-1).astype(h.dtype)
    nchunk = N // PEER_CHUNK

    def step(args):
        x_c, e_c, g_c = args
        u = expert_down[e_c]
        a = jax.nn.gelu(jnp.einsum('cd,chkd->chk', x_c, u), approximate=False)
        vv = expert_up[e_c]
        return jnp.einsum('chk,chkd->cd', g_c * a, vv)

    out = lax.map(step, (ht.reshape(nchunk, PEER_CHUNK, D),
                         eidx.reshape(nchunk, PEER_CHUNK, PEER_HEADS, PEER_TOPK),
                         gate.reshape(nchunk, PEER_CHUNK, PEER_HEADS, PEER_TOPK)))
    return out.reshape(B, T, D)


def hybrid_layer(x, positions, norm_attn, w_in, attn_sinks,
                 cmp_k_pos, cmp_k_w1, cmp_k_w2, cmp_v_pos, cmp_v_w1, cmp_v_w2,
                 w_out, norm_ffn, peer_w_query, peer_sub_keys_1, peer_sub_keys_2,
                 peer_expert_down, peer_expert_up):
    B, T, _ = x.shape
    h = rmsnorm(x, norm_attn)
    proj = h @ w_in
    offsets = np.cumsum(SPLIT_SIZES)[:-1].tolist()
    qa, ka, va, qb, kc, vc, ks, vs, kw, vw, gl = jnp.split(proj, offsets, axis=-1)
    hd = lambda t, n: t.reshape(B, T, n, HEAD_DIM)

    qa = rope(hd(qa, SWA_HEADS), positions)
    ka = rope(hd(ka, SWA_KV_HEADS), positions)
    o_a = banded_gqa_attention(qa, ka, hd(va, SWA_KV_HEADS), SWA_WINDOW, SWA_BLOCK, attn_sinks)

    G = NSA_HEADS // NSA_KV_HEADS
    qb = rope(hd(qb, NSA_HEADS), positions)
    q5 = qb.reshape(B, T, NSA_KV_HEADS, G, HEAD_DIM)
    k_cmp = compress_blocks(rope(hd(kc, NSA_KV_HEADS), positions), cmp_k_pos, cmp_k_w1, cmp_k_w2)
    v_cmp = compress_blocks(hd(vc, NSA_KV_HEADS), cmp_v_pos, cmp_v_w1, cmp_v_w2)
    o_cmp, p_cmp = nsa_compressed_attention(q5, k_cmp, v_cmp)
    sel_idx, sel_valid = nsa_select_blocks(p_cmp, T)
    o_sel = nsa_selected_attention(q5, rope(hd(ks, NSA_KV_HEADS), positions),
                                   hd(vs, NSA_KV_HEADS), sel_idx, sel_valid)
    o_win = banded_gqa_attention(qb, rope(hd(kw, NSA_KV_HEADS), positions),
                                 hd(vw, NSA_KV_HEADS), NSA_WINDOW, NSA_BLOCK)
    gates = jax.nn.sigmoid(gl.astype(jnp.float32)).astype(x.dtype).reshape(B, T, NSA_HEADS, NSA_N_GATES)
    o_b = (gates[..., 0:1] * o_cmp.reshape(B, T, NSA_HEADS, HEAD_DIM)
           + gates[..., 1:2] * o_sel.reshape(B, T, NSA_HEADS, HEAD_DIM)
           + gates[..., 2:3] * o_win)

    mixed = jnp.concatenate([o_a.reshape(B, T, -1), o_b.reshape(B, T, -1)], axis=-1)
    x = x + mixed @ w_out
    x = x + peer_ffn(rmsnorm(x, norm_ffn), peer_w_query, peer_sub_keys_1, peer_sub_keys_2,
                     peer_expert_down, peer_expert_up)
    return x


def setup_inputs(seed: int = 0) -> dict:
    key = jax.random.key(seed)
    ks = jax.random.split(key, 20)
    f32 = jnp.float32
    nrm = lambda k, shape, scale: jax.random.normal(k, shape, f32) * scale
    L = DEPTH
    return {
        "x": nrm(ks[0], (BATCH, SEQ, D_MODEL), 1.0),
        "positions": jnp.broadcast_to(jnp.arange(SEQ, dtype=jnp.int32)[None, :], (BATCH, SEQ)),
        "norm_attn": 1.0 + nrm(ks[1], (L, D_MODEL), 0.02),
        "w_in": nrm(ks[2], (L, D_MODEL, IN_WIDTH), D_MODEL ** -0.5),
        "attn_sinks": nrm(ks[3], (L, SWA_HEADS), 0.5),
        "cmp_k_pos": nrm(ks[4], (L, CMP_LEN, HEAD_DIM), 0.1),
        "cmp_k_w1": nrm(ks[5], (L, CMP_LEN * HEAD_DIM, CMP_HIDDEN), (CMP_LEN * HEAD_DIM) ** -0.5),
        "cmp_k_w2": nrm(ks[6], (L, CMP_HIDDEN, HEAD_DIM), CMP_HIDDEN ** -0.5),
        "cmp_v_pos": nrm(ks[7], (L, CMP_LEN, HEAD_DIM), 0.1),
        "cmp_v_w1": nrm(ks[8], (L, CMP_LEN * HEAD_DIM, CMP_HIDDEN), (CMP_LEN * HEAD_DIM) ** -0.5),
        "cmp_v_w2": nrm(ks[9], (L, CMP_HIDDEN, HEAD_DIM), CMP_HIDDEN ** -0.5),
        "w_out": nrm(ks[10], (L, MIX_WIDTH, D_MODEL), MIX_WIDTH ** -0.5),
        "norm_ffn": 1.0 + nrm(ks[11], (L, D_MODEL), 0.02),
        "peer_w_query": nrm(ks[12], (L, D_MODEL, PEER_HEADS * PEER_QDIM), D_MODEL ** -0.5),
        "peer_sub_keys_1": nrm(ks[13], (L, N_KEYS, PEER_QDIM // 2), (PEER_QDIM // 2) ** -0.5),
        "peer_sub_keys_2": nrm(ks[14], (L, N_KEYS, PEER_QDIM // 2), (PEER_QDIM // 2) ** -0.5),
        "peer_expert_down": nrm(ks[15], (L, N_EXPERTS, D_MODEL), D_MODEL ** -0.5),
        "peer_expert_up": nrm(ks[16], (L, N_EXPERTS, D_MODEL), 0.5),
        "norm_f": 1.0 + nrm(ks[17], (D_MODEL,), 0.02),
    }


def reference(x, positions, norm_attn, w_in, attn_sinks, cmp_k_pos, cmp_k_w1, cmp_k_w2,
              cmp_v_pos, cmp_v_w1, cmp_v_w2, w_out, norm_ffn, peer_w_query,
              peer_sub_keys_1, peer_sub_keys_2, peer_expert_down, peer_expert_up, norm_f):
    h = x
    for layer in range(DEPTH):
        h = hybrid_layer(h, positions, norm_attn[layer], w_in[layer], attn_sinks[layer],
                         cmp_k_pos[layer], cmp_k_w1[layer], cmp_k_w2[layer],
                         cmp_v_pos[layer], cmp_v_w1[layer], cmp_v_w2[layer],
                         w_out[layer], norm_ffn[layer], peer_w_query[layer],
                         peer_sub_keys_1[layer], peer_sub_keys_2[layer],
                         peer_expert_down[layer], peer_expert_up[layer])
    return rmsnorm(h, norm_f)
```

```python
import functools

import numpy as np
import jax
import jax.numpy as jnp
from jax import lax
from jax.experimental import pallas as pl
from jax.experimental.pallas import tpu as pltpu

F32 = jnp.float32
BF16 = jnp.bfloat16
I32 = jnp.int32

HEAD_DIM = 64
ROPE_THETA = 10000.0
RMS_EPS = 1e-6
NEG_BIG = -1e30

SWA_HEADS = 8
SWA_KV_HEADS = 2
SWA_WINDOW = 128

NSA_HEADS = 8
NSA_KV_HEADS = 2
CMP_LEN = 32
CMP_STRIDE = 16
CMP_HIDDEN = 256
SEL_BLOCK = 64
SEL_TOPN = 16
NSA_WINDOW = 512
NSA_N_GATES = 3

PEER_HEADS = 8
N_KEYS = 128
PEER_QDIM = 256
PEER_TOPK = 16

LANES = 128
SUBLANES = 8
VMEM_LIMIT = 56 * 1024 * 1024

_IN_SPLITS = (("qa", 512), ("ka", 128), ("va", 128), ("qb", 512), ("kc", 128), ("vc", 128),
              ("ks", 128), ("vs", 128), ("kw", 128), ("vw", 128), ("gl", 24))
IN_WIDTH = sum(w for _, w in _IN_SPLITS)
IN_PAD = 2176
_IN_OFF = {}
_o = 0
for _n, _w in _IN_SPLITS:
    _IN_OFF[_n] = _o
    _o += _w
_ROPED = ("qa", "ka", "qb", "kc", "ks", "kw")


def _gelu(x):
    return 0.5 * x * (1.0 + lax.erf(x * 0.7071067811865476))


def _params(*sem):
    return pltpu.CompilerParams(dimension_semantics=sem, vmem_limit_bytes=VMEM_LIMIT)


def _inproj_kernel(x_ref, pos_ref, g_ref, w_ref, invf_ref,
                   qa_ref, ka_ref, va_ref, qb_ref, kc_ref, vc_ref, ks_ref, vs_ref, kw_ref, vw_ref, gt_ref):
    x = x_ref[...]
    h = x * lax.rsqrt(jnp.mean(x * x, axis=-1, keepdims=True) + RMS_EPS) * g_ref[...]
    proj = jnp.dot(h.astype(BF16), w_ref[...], preferred_element_type=F32)
    ang = pos_ref[...].astype(F32) * invf_ref[...]
    lane = lax.broadcasted_iota(I32, ang.shape, 1)
    lo = (lane & (HEAD_DIM - 1)) < (HEAD_DIM // 2)
    cos = jnp.cos(ang)
    sin = jnp.sin(ang)
    sin_s = jnp.where(lo, -sin, sin)
    outs = {"qa": qa_ref, "ka": ka_ref, "va": va_ref, "qb": qb_ref, "kc": kc_ref, "vc": vc_ref,
            "ks": ks_ref, "vs": vs_ref, "kw": kw_ref, "vw": vw_ref}
    for name, width in _IN_SPLITS[:-1]:
        off = _IN_OFF[name]
        ref = outs[name]
        for i in range(width // LANES):
            v = proj[:, off + LANES * i: off + LANES * (i + 1)]
            if name in _ROPED:
                rot = jnp.where(lo, pltpu.roll(v, LANES - HEAD_DIM // 2, 1), pltpu.roll(v, HEAD_DIM // 2, 1))
                v = v * cos + rot * sin_s
            ref[:, LANES * i: LANES * (i + 1)] = v.astype(ref.dtype)
    gl = proj[:, _IN_OFF["gl"]: _IN_OFF["gl"] + LANES]
    gt_ref[...] = jax.nn.sigmoid(gl)


def _inproj(x2, pos2, g, w_pad, invf, tm):
    n, d = x2.shape
    row = lambda w: pl.BlockSpec((tm, w), lambda i: (i, 0))
    full = lambda a: pl.BlockSpec(a.shape, lambda i: (0,) * a.ndim)
    out_dt = {"qa": BF16, "ka": BF16, "va": BF16, "qb": BF16, "kc": F32, "vc": F32,
              "ks": BF16, "vs": BF16, "kw": BF16, "vw": BF16}
    names = [nm for nm, _ in _IN_SPLITS[:-1]]
    widths = dict(_IN_SPLITS)
    out_shape = [jax.ShapeDtypeStruct((n, widths[nm]), out_dt[nm]) for nm in names]
    out_shape.append(jax.ShapeDtypeStruct((n, LANES), F32))
    out_specs = [row(widths[nm]) for nm in names] + [row(LANES)]
    res = pl.pallas_call(
        _inproj_kernel,
        grid=(n // tm,),
        in_specs=[row(d), row(1), full(g), full(w_pad), full(invf)],
        out_specs=out_specs,
        out_shape=out_shape,
        compiler_params=_params("parallel"),
        name="inproj",
    )(x2, pos2, g, w_pad, invf)
    out = dict(zip(names, res[:-1]))
    out["gates"] = res[-1]
    return out


def _compress_kernel(kc_ref, vc_ref, kpos_ref, kw1_ref, kw2_ref, vpos_ref, vw1_ref, vw2_ref, ko_ref, vo_ref):
    ncp = ko_ref.shape[0]
    for src, pos_ref, w1_ref, w2_ref, out_ref in ((kc_ref, kpos_ref, kw1_ref, kw2_ref, ko_ref),
                                                  (vc_ref, vpos_ref, vw1_ref, vw2_ref, vo_ref)):
        bias = jnp.dot(pos_ref[...].astype(BF16), w1_ref[...], preferred_element_type=F32)
        acc_a = [jnp.zeros((ncp, CMP_HIDDEN), F32) for _ in range(2)]
        acc_b = [jnp.zeros((ncp, CMP_HIDDEN), F32) for _ in range(2)]
        for j in range(CMP_STRIDE):
            pj = src[pl.ds(j, ncp, stride=CMP_STRIDE), :].astype(BF16)
            wa = w1_ref[j * HEAD_DIM:(j + 1) * HEAD_DIM, :]
            wb = w1_ref[(CMP_STRIDE + j) * HEAD_DIM:(CMP_STRIDE + j + 1) * HEAD_DIM, :]
            for hh in range(2):
                ph = pj[:, hh * HEAD_DIM:(hh + 1) * HEAD_DIM]
                acc_a[hh] = acc_a[hh] + jnp.dot(ph, wa, preferred_element_type=F32)
                acc_b[hh] = acc_b[hh] + jnp.dot(ph, wb, preferred_element_type=F32)
        for hh in range(2):
            h1 = acc_a[hh] + pltpu.roll(acc_b[hh], ncp - 1, 0) + bias
            out = jnp.dot(_gelu(h1).astype(BF16), w2_ref[...], preferred_element_type=F32)
            out_ref[:, hh * HEAD_DIM:(hh + 1) * HEAD_DIM] = out


def _compress(kc3, vc3, kpos, kw1, kw2, vpos, vw1, vw2):
    b, t, w = kc3.shape
    ncp = t // CMP_STRIDE
    per_b = pl.BlockSpec((None, t, w), lambda i: (i, 0, 0))
    full = lambda a: pl.BlockSpec(a.shape, lambda i: (0,) * a.ndim)
    out_spec = pl.BlockSpec((None, ncp, w), lambda i: (i, 0, 0))
    return pl.pallas_call(
        _compress_kernel,
        grid=(b,),
        in_specs=[per_b, per_b, full(kpos), full(kw1), full(kw2), full(vpos), full(vw1), full(vw2)],
        out_specs=[out_spec, out_spec],
        out_shape=[jax.ShapeDtypeStruct((b, ncp, w), F32)] * 2,
        compiler_params=_params("parallel"),
        name="compress",
    )(kc3, vc3, kpos, kw1, kw2, vpos, vw1, vw2)


def _cmpsel_kernel(q_ref, kc_ref, vc_ref, ovt_ref, o_ref, sel_ref, *, tq, n_cmp):
    i = pl.program_id(1)
    t0 = i * tq
    g = NSA_HEADS // NSA_KV_HEADS
    ncp = kc_ref.shape[0]
    ns = ovt_ref.shape[0]
    rows = g * tq
    tpos = t0 + (lax.broadcasted_iota(I32, (rows, ncp), 0) & (tq - 1))
    cidx = lax.broadcasted_iota(I32, (rows, ncp), 1)
    vis = (cidx * CMP_STRIDE + (CMP_LEN - 1) <= tpos) & (cidx < n_cmp)
    jidx = lax.broadcasted_iota(I32, (ns, tq), 0)
    tcol = t0 + lax.broadcasted_iota(I32, (ns, tq), 1)
    cur = tcol // SEL_BLOCK
    forced = (jidx == 0) | (jidx == cur) | (jidx == cur - 1)
    valid = jidx * SEL_BLOCK <= tcol
    sel_w = sel_ref.shape[1] // NSA_KV_HEADS
    if ns < sel_w:
        sel_ref[...] = jnp.zeros(sel_ref.shape, sel_ref.dtype)
    for hk in range(NSA_KV_HEADS):
        kk = kc_ref[:, hk * HEAD_DIM:(hk + 1) * HEAD_DIM].astype(BF16)
        vv = vc_ref[:, hk * HEAD_DIM:(hk + 1) * HEAD_DIM].astype(BF16)
        q4 = jnp.concatenate(
            [q_ref[:, (hk * g + gi) * HEAD_DIM:(hk * g + gi + 1) * HEAD_DIM] for gi in range(g)], axis=0)
        s = lax.dot_general(q4, kk, (((1,), (1,)), ((), ())), preferred_element_type=F32) * (HEAD_DIM ** -0.5)
        s = jnp.where(vis, s, NEG_BIG)
        m = jnp.max(s, axis=-1, keepdims=True)
        e = jnp.exp(s - m)
        p = jnp.where(vis, e / jnp.sum(e, axis=-1, keepdims=True), 0.0)
        o = jnp.dot(p.astype(BF16), vv, preferred_element_type=F32)
        for gi in range(g):
            h = hk * g + gi
            o_ref[:, h * HEAD_DIM:(h + 1) * HEAD_DIM] = o[gi * tq:(gi + 1) * tq]
        psum = p[0:tq]
        for gi in range(1, g):
            psum = psum + p[gi * tq:(gi + 1) * tq]
        imp_t = lax.dot_general(ovt_ref[...], psum, (((1,), (1,)), ((), ())),
                                precision=lax.Precision.HIGHEST, preferred_element_type=F32)
        score = jnp.where(forced, jnp.inf, jnp.where(valid, imp_t, -jnp.inf))
        cnt = jnp.zeros((ns, tq), F32)
        for r in range(ns):
            row = score[r:r + 1, :]
            beats = (row > score) | ((row >= score) & (jidx > r))
            cnt = cnt + jnp.where(beats, 1.0, 0.0)
        sel_t = jnp.where((cnt < SEL_TOPN) & (score > -jnp.inf), 1.0, 0.0)
        sel_ref[:, hk * sel_w:hk * sel_w + ns] = jnp.transpose(sel_t).astype(sel_ref.dtype)


def _cmpsel(qb3, kcmp, vcmp, ovt, tq, n_cmp):
    b, t, w = qb3.shape
    ncp = kcmp.shape[1]
    kern = functools.partial(_cmpsel_kernel, tq=tq, n_cmp=n_cmp)
    return pl.pallas_call(
        kern,
        grid=(b, t // tq),
        in_specs=[pl.BlockSpec((None, tq, w), lambda bi, i: (bi, i, 0)),
                  pl.BlockSpec((None, ncp, LANES), lambda bi, i: (bi, 0, 0)),
                  pl.BlockSpec((None, ncp, LANES), lambda bi, i: (bi, 0, 0)),
                  pl.BlockSpec(ovt.shape, lambda bi, i: (0, 0))],
        out_specs=[pl.BlockSpec((None, tq, w), lambda bi, i: (bi, i, 0)),
                   pl.BlockSpec((None, tq, LANES), lambda bi, i: (bi, i, 0))],
        out_shape=[jax.ShapeDtypeStruct((b, t, w), F32), jax.ShapeDtypeStruct((b, t, LANES), BF16)],
        compiler_params=_params("parallel", "parallel"),
        name="cmpsel",
    )(qb3, kcmp, vcmp, ovt)


def _selattn_kernel(q_ref, k_ref, v_ref, sel_ref, o_ref, m_sc, l_sc, acc_sc, *, tq, tk):
    i = pl.program_id(1)
    t0 = i * tq
    g = NSA_HEADS // NSA_KV_HEADS
    rows = g * tq
    n_kt = (t0 + tq + tk - 1) // tk
    qpos = t0 + (lax.broadcasted_iota(I32, (rows, tk), 0) & (tq - 1))
    kcol = lax.broadcasted_iota(I32, (rows, tk), 1)
    ns = sel_ref.shape[1] // NSA_KV_HEADS
    erow = lax.broadcasted_iota(I32, (ns, tk), 0)
    ecol = lax.broadcasted_iota(I32, (ns, tk), 1)
    for hk in range(NSA_KV_HEADS):
        q4 = jnp.concatenate(
            [q_ref[:, (hk * g + gi) * HEAD_DIM:(hk * g + gi + 1) * HEAD_DIM] for gi in range(g)], axis=0)
        selh = sel_ref[:, hk * ns:(hk + 1) * ns]
        sel4 = jnp.concatenate([selh] * g, axis=0)
        m_sc[...] = jnp.full(m_sc.shape, NEG_BIG, F32)
        l_sc[...] = jnp.zeros(l_sc.shape, F32)
        acc_sc[...] = jnp.zeros(acc_sc.shape, F32)

        def body(kt, carry):
            ks = pl.multiple_of(kt * tk, tk)
            kk = k_ref[pl.ds(ks, tk), hk * HEAD_DIM:(hk + 1) * HEAD_DIM]
            vv = v_ref[pl.ds(ks, tk), hk * HEAD_DIM:(hk + 1) * HEAD_DIM]
            s = lax.dot_general(q4, kk, (((1,), (1,)), ((), ())), preferred_element_type=F32) * (HEAD_DIM ** -0.5)
            expand = jnp.where(((ks + ecol) // SEL_BLOCK) == erow, 1.0, 0.0).astype(BF16)
            selm = jnp.dot(sel4, expand, preferred_element_type=F32)
            mask = (selm > 0.5) & (ks + kcol <= qpos)
            s = jnp.where(mask, s, NEG_BIG)
            m_old = m_sc[...]
            m_new = jnp.maximum(m_old, jnp.max(s, axis=-1, keepdims=True))
            alpha = jnp.exp(m_old - m_new)
            p = jnp.where(mask, jnp.exp(s - m_new), 0.0)
            l_sc[...] = alpha * l_sc[...] + jnp.sum(p, axis=-1, keepdims=True)
            acc_sc[...] = alpha * acc_sc[...] + jnp.dot(p.astype(BF16), vv, preferred_element_type=F32)
            m_sc[...] = m_new
            return carry

        lax.fori_loop(0, n_kt, body, 0)
        o = acc_sc[...] / l_sc[...]
        for gi in range(g):
            h = hk * g + gi
            o_ref[:, h * HEAD_DIM:(h + 1) * HEAD_DIM] = o[gi * tq:(gi + 1) * tq]


def _selattn(qb3, ks3, vs3, sel3, tq, tk):
    b, t, w = qb3.shape
    g = NSA_HEADS // NSA_KV_HEADS
    kern = functools.partial(_selattn_kernel, tq=tq, tk=tk)
    return pl.pallas_call(
        kern,
        grid=(b, t // tq),
        in_specs=[pl.BlockSpec((None, tq, w), lambda bi, i: (bi, i, 0)),
                  pl.BlockSpec((None, t, LANES), lambda bi, i: (bi, 0, 0)),
                  pl.BlockSpec((None, t, LANES), lambda bi, i: (bi, 0, 0)),
                  pl.BlockSpec((None, tq, LANES), lambda bi, i: (bi, i, 0))],
        out_specs=pl.BlockSpec((None, tq, w), lambda bi, i: (bi, i, 0)),
        out_shape=jax.ShapeDtypeStruct((b, t, w), F32),
        scratch_shapes=[pltpu.VMEM((g * tq, 1), F32), pltpu.VMEM((g * tq, 1), F32),
                        pltpu.VMEM((g * tq, HEAD_DIM), F32)],
        compiler_params=_params("parallel", "arbitrary"),
        name="selattn",
    )(qb3, ks3, vs3, sel3)


def _band_kernel(*refs, window, tq, span, n_heads, n_kv, has_sink):
    if has_sink:
        sink_ref, q_ref, k_ref, v_ref, o_ref = refs
    else:
        q_ref, k_ref, v_ref, o_ref = refs
    i = pl.program_id(1)
    t0 = i * tq
    start = pl.multiple_of(jnp.maximum(t0 + tq - span, 0), tq)
    g = n_heads // n_kv
    rows = g * tq
    qpos = t0 + (lax.broadcasted_iota(I32, (rows, span), 0) & (tq - 1))
    kpos = start + lax.broadcasted_iota(I32, (rows, span), 1)
    diff = qpos - kpos
    mask = (diff >= 0) & (diff < window)
    for hk in range(n_kv):
        kk = k_ref[pl.ds(start, span), hk * HEAD_DIM:(hk + 1) * HEAD_DIM]
        vv = v_ref[pl.ds(start, span), hk * HEAD_DIM:(hk + 1) * HEAD_DIM]
        q4 = jnp.concatenate(
            [q_ref[:, (hk * g + gi) * HEAD_DIM:(hk * g + gi + 1) * HEAD_DIM] for gi in range(g)], axis=0)
        s = lax.dot_general(q4, kk, (((1,), (1,)), ((), ())), preferred_element_type=F32) * (HEAD_DIM ** -0.5)
        s = jnp.where(mask, s, NEG_BIG)
        m = jnp.max(s, axis=-1, keepdims=True)
        if has_sink:
            sink = jnp.concatenate([jnp.full((tq, 1), sink_ref[hk * g + gi], F32) for gi in range(g)], axis=0)
            m = jnp.maximum(m, sink)
        p = jnp.where(mask, jnp.exp(s - m), 0.0)
        l = jnp.sum(p, axis=-1, keepdims=True)
        if has_sink:
            l = l + jnp.exp(sink - m)
        o = jnp.dot(p.astype(BF16), vv, preferred_element_type=F32) / l
        for gi in range(g):
            h = hk * g + gi
            o_ref[:, h * HEAD_DIM:(h + 1) * HEAD_DIM] = o[gi * tq:(gi + 1) * tq]


def _band(q3, k3, v3, sinks, window, tq, n_heads, n_kv):
    b, t, w = q3.shape
    nprev = -(-(window - 1) // tq)
    span = min((nprev + 1) * tq, t)
    has_sink = sinks is not None
    kern = functools.partial(_band_kernel, window=window, tq=tq, span=span, n_heads=n_heads, n_kv=n_kv,
                             has_sink=has_sink)
    in_specs = [pl.BlockSpec((None, tq, w), lambda bi, i: (bi, i, 0)),
                pl.BlockSpec((None, t, LANES), lambda bi, i: (bi, 0, 0)),
                pl.BlockSpec((None, t, LANES), lambda bi, i: (bi, 0, 0))]
    args = [q3, k3, v3]
    if has_sink:
        in_specs = [pl.BlockSpec(memory_space=pltpu.SMEM)] + in_specs
        args = [sinks] + args
    return pl.pallas_call(
        kern,
        grid=(b, t // tq),
        in_specs=in_specs,
        out_specs=pl.BlockSpec((None, tq, w), lambda bi, i: (bi, i, 0)),
        out_shape=jax.ShapeDtypeStruct((b, t, w), F32),
        compiler_params=_params("parallel", "parallel"),
        name="band_sink" if has_sink else "band_win",
    )(*args)


def _topk_rows(s, k, ridx):
    big = float(s.shape[0])
    vals, idxs = [], []
    for _ in range(k):
        m = jnp.max(s, axis=0, keepdims=True)
        first = jnp.min(jnp.where(s == m, ridx, big), axis=0, keepdims=True)
        vals.append(m)
        idxs.append(first)
        s = jnp.where(ridx == first, -jnp.inf, s)
    return jnp.concatenate(vals, axis=0), jnp.concatenate(idxs, axis=0)


def _peer_candidates(t1, i1, t2, i2):
    vals = [t1[0:1] + t2]
    eids = [i1[0:1] * float(N_KEYS) + i2]
    for a in range(1, 8):
        vals.append(t1[a:a + 1] + t2[0:8])
        eids.append(i1[a:a + 1] * float(N_KEYS) + i2[0:8])
    vals.append(t1[8:16] + t2[0:1])
    eids.append(i1[8:16] * float(N_KEYS) + i2[0:1])
    return jnp.concatenate(vals, axis=0), jnp.concatenate(eids, axis=0)


def _route_kernel(x_ref, oa_ref, oc_ref, os_ref, ow_ref, gt_ref, gexp_ref, wo_ref, gf_ref, wq_ref,
                  k1_ref, k2_ref, x1_ref, eidx_ref, gate_ref):
    tm = x_ref.shape[0]
    gt = gt_ref[...]
    g_hi = gt.astype(BF16)
    g_lo = (gt - g_hi.astype(F32)).astype(BF16)
    ob = jnp.zeros(oc_ref.shape, F32)
    for j, br in enumerate((oc_ref, os_ref, ow_ref)):
        ex = gexp_ref[j]
        gj = (jnp.dot(g_hi, ex, preferred_element_type=F32) + jnp.dot(g_lo, ex, preferred_element_type=F32))
        ob = ob + gj * br[...]
    half = oa_ref.shape[1]
    mixed = (jnp.dot(oa_ref[...].astype(BF16), wo_ref[0:half, :], preferred_element_type=F32)
             + jnp.dot(ob.astype(BF16), wo_ref[half:, :], preferred_element_type=F32))
    x1 = x_ref[...] + mixed
    x1_ref[...] = x1
    h2 = x1 * lax.rsqrt(jnp.mean(x1 * x1, axis=-1, keepdims=True) + RMS_EPS) * gf_ref[...]
    q = jnp.dot(h2.astype(BF16), wq_ref[...], preferred_element_type=F32)
    hq = PEER_QDIM // 2
    ridx = lax.broadcasted_iota(I32, (N_KEYS, tm), 0).astype(F32)
    e_rows, g_rows = [], []
    for h in range(PEER_HEADS):
        q1 = q[:, h * PEER_QDIM: h * PEER_QDIM + hq].astype(BF16)
        q2 = q[:, h * PEER_QDIM + hq:(h + 1) * PEER_QDIM].astype(BF16)
        s1 = lax.dot_general(k1_ref[...], q1, (((1,), (1,)), ((), ())), preferred_element_type=F32)
        s2 = lax.dot_general(k2_ref[...], q2, (((1,), (1,)), ((), ())), preferred_element_type=F32)
        t1, i1 = _topk_rows(s1, PEER_TOPK, ridx)
        t2, i2 = _topk_rows(s2, PEER_TOPK, ridx)
        cand, cand_e = _peer_candidates(t1, i1, t2, i2)
        cidx = lax.broadcasted_iota(I32, cand.shape, 0).astype(F32)
        big = float(cand.shape[0])
        sc, ee = [], []
        for _ in range(PEER_TOPK):
            m = jnp.max(cand, axis=0, keepdims=True)
            first = jnp.min(jnp.where(cand == m, cidx, big), axis=0, keepdims=True)
            hit = cidx == first
            sc.append(m)
            ee.append(jnp.max(jnp.where(hit, cand_e, -1.0), axis=0, keepdims=True))
            cand = jnp.where(hit, -jnp.inf, cand)
        sc = jnp.concatenate(sc, axis=0)
        ex = jnp.exp(sc - sc[0:1])
        g_rows.append(ex / jnp.sum(ex, axis=0, keepdims=True))
        e_rows.append(jnp.concatenate(ee, axis=0))
    e_t = jnp.concatenate(e_rows, axis=0)
    g_t = jnp.concatenate(g_rows, axis=0)
    eidx_ref[...] = jnp.transpose(e_t).astype(I32)
    gate_ref[...] = jnp.transpose(g_t)


def _route(x2, oa, oc, osel, ow, gates, gexp, wo, gf, wq, k1, k2, tm):
    n, d = x2.shape
    row = lambda w: pl.BlockSpec((tm, w), lambda i: (i, 0))
    full = lambda a: pl.BlockSpec(a.shape, lambda i: (0,) * a.ndim)
    nk = PEER_HEADS * PEER_TOPK
    return pl.pallas_call(
        _route_kernel,
        grid=(n // tm,),
        in_specs=[row(d), row(oa.shape[1]), row(oc.shape[1]), row(osel.shape[1]), row(ow.shape[1]), row(LANES),
                  full(gexp), full(wo), full(gf), full(wq), full(k1), full(k2)],
        out_specs=[row(d), row(nk), row(nk)],
        out_shape=[jax.ShapeDtypeStruct((n, d), F32), jax.ShapeDtypeStruct((n, nk), I32),
                   jax.ShapeDtypeStruct((n, nk), F32)],
        compiler_params=_params("parallel"),
        name="route",
    )(x2, oa, oc, osel, ow, gates, gexp, wo, gf, wq, k1, k2)


def _fold8(p):
    sub = lax.broadcasted_iota(I32, (SUBLANES, LANES), 0)
    lo4 = sub < 4
    q = [jnp.where(lo4, p[j], p[j + 4]) + pltpu.roll(jnp.where(lo4, p[j + 4], p[j]), 4, 0) for j in range(4)]
    m2 = (sub & 3) < 2
    r = [jnp.where(m2, q[j] + pltpu.roll(q[j], 6, 0), q[j + 2] + pltpu.roll(q[j + 2], 2, 0)) for j in range(2)]
    m1 = (sub & 1) == 0
    return jnp.where(m1, r[0] + pltpu.roll(r[0], 7, 0), r[1] + pltpu.roll(r[1], 1, 0))


def _peer_kernel(ecur_ref, enxt_ref, x_ref, gate_ref, gffn_ref, gfin_ref, tab_ref, o_ref,
                 buf, sem, gt_sc, cb_sc, *, tb):
    step = pl.program_id(0)
    nsteps = pl.num_programs(0)
    slot = step % 2
    nk = PEER_HEADS * PEER_TOPK
    npair = tb * nk

    def issue(eref, dst_slot):
        def body(j, carry):
            e = eref[j // nk, j % nk]
            pltpu.make_async_copy(tab_ref.at[e], buf.at[dst_slot, j], sem.at[dst_slot]).start()
            return carry
        lax.fori_loop(0, npair, body, 0, unroll=8)

    @pl.when(step == 0)
    def _():
        issue(ecur_ref, 0)

    @pl.when(step + 1 < nsteps)
    def _():
        issue(enxt_ref, 1 - slot)

    eye = jnp.where(lax.broadcasted_iota(I32, (nk, nk), 0) == lax.broadcasted_iota(I32, (nk, nk), 1),
                    1.0, 0.0).astype(BF16)
    gr = gate_ref[...]
    g0 = gr.astype(BF16)
    r1 = gr - g0.astype(F32)
    g1 = r1.astype(BF16)
    g2 = (r1 - g1.astype(F32)).astype(BF16)
    dn = (((1,), (1,)), ((), ()))
    gt_sc[...] = (lax.dot_general(eye, g0, dn, preferred_element_type=F32)
                  + lax.dot_general(eye, g1, dn, preferred_element_type=F32)
                  + lax.dot_general(eye, g2, dn, preferred_element_type=F32))

    pltpu.make_async_copy(tab_ref.at[pl.ds(0, npair)], buf.at[slot], sem.at[slot]).wait()

    lane_tb = lax.broadcasted_iota(I32, (nk, tb), 1)
    inv_d = 1.0 / (SUBLANES * LANES)

    def token(t, carry):
        x8 = x_ref[t]
        ms = jnp.sum(jnp.sum(x8 * x8, axis=1, keepdims=True), axis=0, keepdims=True) * inv_d
        h8 = x8 * lax.rsqrt(ms + RMS_EPS) * gffn_ref[...]
        base = t * nk
        folded = []
        for gi in range(nk // SUBLANES):
            blk = buf[slot, pl.ds(base + gi * SUBLANES, SUBLANES), 0:SUBLANES, :]
            folded.append(_fold8([blk[j] * h8 for j in range(SUBLANES)]))
        a = jnp.sum(jnp.concatenate(folded, axis=0), axis=1, keepdims=True)
        gcol = jnp.sum(jnp.where(lane_tb == t, gt_sc[...], 0.0), axis=1, keepdims=True)
        c = gcol * _gelu(a)
        cb_sc[...] = jnp.broadcast_to(c, (nk, LANES))
        acc = jnp.zeros((SUBLANES, LANES), F32)
        for k in range(nk):
            ck = jnp.broadcast_to(cb_sc[k:k + 1, :], (SUBLANES, LANES))
            acc = acc + ck * buf[slot, base + k, SUBLANES:2 * SUBLANES, :]
        y8 = x8 + acc
        ms2 = jnp.sum(jnp.sum(y8 * y8, axis=1, keepdims=True), axis=0, keepdims=True) * inv_d
        o_ref[t] = y8 * lax.rsqrt(ms2 + RMS_EPS) * gfin_ref[...]
        return carry

    lax.fori_loop(0, tb, token, 0)


def _peer(eidx, x1s, gate, gffn8, gfin8, table, tb):
    n = x1s.shape[0]
    nk = PEER_HEADS * PEER_TOPK
    nsteps = n // tb
    kern = functools.partial(_peer_kernel, tb=tb)
    return pl.pallas_call(
        kern,
        grid=(nsteps,),
        in_specs=[pl.BlockSpec((tb, nk), lambda i: (i, 0), memory_space=pltpu.SMEM),
                  pl.BlockSpec((tb, nk), lambda i: (jnp.minimum(i + 1, nsteps - 1), 0), memory_space=pltpu.SMEM),
                  pl.BlockSpec((tb, SUBLANES, LANES), lambda i: (i, 0, 0)),
                  pl.BlockSpec((tb, nk), lambda i: (i, 0)),
                  pl.BlockSpec((SUBLANES, LANES), lambda i: (0, 0)),
                  pl.BlockSpec((SUBLANES, LANES), lambda i: (0, 0)),
                  pl.BlockSpec(memory_space=pl.ANY)],
        out_specs=pl.BlockSpec((tb, SUBLANES, LANES), lambda i: (i, 0, 0)),
        out_shape=jax.ShapeDtypeStruct((n, SUBLANES, LANES), F32),
        scratch_shapes=[pltpu.VMEM((2, tb * nk, 2 * SUBLANES, LANES), F32),
                        pltpu.SemaphoreType.DMA((2,)),
                        pltpu.VMEM((nk, tb), F32),
                        pltpu.VMEM((nk, LANES), F32)],
        compiler_params=_params("arbitrary"),
        name="peer",
    )(eidx, eidx, x1s, gate, gffn8, gfin8, table)


def _overlap_t(t):
    nc = (t - CMP_LEN) // CMP_STRIDE + 1
    ns = t // SEL_BLOCK
    ncp = t // CMP_STRIDE
    cstart = np.arange(nc) * CMP_STRIDE
    bstart = np.arange(ns) * SEL_BLOCK
    lo = np.maximum(cstart[:, None], bstart[None, :])
    hi = np.minimum(cstart[:, None] + CMP_LEN, bstart[None, :] + SEL_BLOCK)
    ov = np.clip(hi - lo, 0, None).astype(np.float32) / CMP_LEN
    out = np.zeros((ns, ncp), np.float32)
    out[:, :nc] = ov.T
    return out, nc


def _gate_expand():
    ex = np.zeros((NSA_N_GATES, LANES, NSA_HEADS * HEAD_DIM), np.float32)
    for j in range(NSA_N_GATES):
        for h in range(NSA_HEADS):
            ex[j, h * NSA_N_GATES + j, h * HEAD_DIM:(h + 1) * HEAD_DIM] = 1.0
    return ex


def _layer(x, positions, norm_attn, w_in, attn_sinks, cmp_k_pos, cmp_k_w1, cmp_k_w2, cmp_v_pos, cmp_v_w1,
           cmp_v_w2, w_out, norm_ffn, w_query, sub_keys_1, sub_keys_2, expert_down, expert_up, norm_out,
           tm_in=512, tq_cmp=256, tq_sel=256, tq_band=128, tm_route=256, tb_peer=8):
    b, t, d = x.shape
    n = b * t
    x2 = x.reshape(n, d)
    half = HEAD_DIM // 2
    inv_freq = ROPE_THETA ** (-jnp.arange(half, dtype=F32) / half)
    invf = jnp.tile(inv_freq, LANES // half).reshape(1, LANES)
    w_pad = jnp.pad(w_in, ((0, 0), (0, IN_PAD - IN_WIDTH))).astype(BF16)
    pr = _inproj(x2, positions.reshape(n, 1), norm_attn.reshape(1, d), w_pad, invf, min(tm_in, n))
    r3 = lambda a: a.reshape(b, t, a.shape[-1])

    o_a = _band(r3(pr["qa"]), r3(pr["ka"]), r3(pr["va"]), attn_sinks.astype(F32), SWA_WINDOW, tq_band,
                SWA_HEADS, SWA_KV_HEADS)
    kcmp, vcmp = _compress(r3(pr["kc"]), r3(pr["vc"]),
                           cmp_k_pos.reshape(1, -1), cmp_k_w1.astype(BF16), cmp_k_w2.astype(BF16),
                           cmp_v_pos.reshape(1, -1), cmp_v_w1.astype(BF16), cmp_v_w2.astype(BF16))
    ovt, n_cmp = _overlap_t(t)
    qb3 = r3(pr["qb"])
    o_cmp, sel = _cmpsel(qb3, kcmp, vcmp, jnp.asarray(ovt), tq_cmp, n_cmp)
    o_sel = _selattn(qb3, r3(pr["ks"]), r3(pr["vs"]), sel, tq_sel, tq_sel)
    o_win = _band(qb3, r3(pr["kw"]), r3(pr["vw"]), None, NSA_WINDOW, tq_band, NSA_HEADS, NSA_KV_HEADS)

    x1, eidx, gate = _route(x2, o_a.reshape(n, -1), o_cmp.reshape(n, -1), o_sel.reshape(n, -1),
                            o_win.reshape(n, -1), pr["gates"], jnp.asarray(_gate_expand(), BF16),
                            w_out.astype(BF16), norm_ffn.reshape(1, d), w_query.astype(BF16),
                            sub_keys_1.astype(BF16), sub_keys_2.astype(BF16), min(tm_route, n))
    ne = expert_down.shape[0]
    table = jnp.concatenate([expert_down.reshape(ne, SUBLANES, LANES), expert_up.reshape(ne, SUBLANES, LANES)],
                            axis=1)
    y = _peer(eidx, x1.reshape(n, SUBLANES, LANES), gate, norm_ffn.reshape(SUBLANES, LANES),
              norm_out.reshape(SUBLANES, LANES), table, tb_peer)
    return y.reshape(b, t, d)


def kernel(x, positions, norm_attn, w_in, attn_sinks, cmp_k_pos, cmp_k_w1, cmp_k_w2, cmp_v_pos, cmp_v_w1,
           cmp_v_w2, w_out, norm_ffn, peer_w_query, peer_sub_keys_1, peer_sub_keys_2, peer_expert_down,
           peer_expert_up, norm_f):
    assert norm_attn.shape[0] == 1, "single-layer block"
    return _layer(x, positions, norm_attn[0], w_in[0], attn_sinks[0], cmp_k_pos[0], cmp_k_w1[0], cmp_k_w2[0],
                  cmp_v_pos[0], cmp_v_w1[0], cmp_v_w2[0], w_out[0], norm_ffn[0], peer_w_query[0],
                  peer_sub_keys_1[0], peer_sub_keys_2[0], peer_expert_down[0], peer_expert_up[0], norm_f)
```

```python
import functools

import numpy as np
import jax
import jax.numpy as jnp
from jax import lax
from jax.experimental import pallas as pl
from jax.experimental.pallas import tpu as pltpu

F32 = jnp.float32
BF16 = jnp.bfloat16
I32 = jnp.int32

HEAD_DIM = 64
ROPE_THETA = 10000.0
RMS_EPS = 1e-6
NEG_BIG = -1e30

SWA_HEADS = 8
SWA_KV_HEADS = 2
SWA_WINDOW = 128

NSA_HEADS = 8
NSA_KV_HEADS = 2
CMP_LEN = 32
CMP_STRIDE = 16
CMP_HIDDEN = 256
SEL_BLOCK = 64
SEL_TOPN = 16
NSA_WINDOW = 512
NSA_N_GATES = 3

PEER_HEADS = 8
N_KEYS = 128
PEER_QDIM = 256
PEER_TOPK = 16

LANES = 128
SUBLANES = 8
VMEM_LIMIT = 56 * 1024 * 1024

_IN_SPLITS = (("qa", 512), ("ka", 128), ("va", 128), ("qb", 512), ("kc", 128), ("vc", 128),
              ("ks", 128), ("vs", 128), ("kw", 128), ("vw", 128), ("gl", 24))
IN_WIDTH = sum(w for _, w in _IN_SPLITS)
IN_PAD = 2176
_IN_OFF = {}
_o = 0
for _n, _w in _IN_SPLITS:
    _IN_OFF[_n] = _o
    _o += _w
_ROPED = ("qa", "ka", "qb", "kc", "ks", "kw")


def _gelu(x):
    return 0.5 * x * (1.0 + lax.erf(x * 0.7071067811865476))


def _params(*sem):
    return pltpu.CompilerParams(dimension_semantics=sem, vmem_limit_bytes=VMEM_LIMIT)


def _inproj_kernel(x_ref, pos_ref, g_ref, w_ref, invf_ref,
                   qa_ref, ka_ref, va_ref, qb_ref, kc_ref, vc_ref, ks_ref, vs_ref, kw_ref, vw_ref, gt_ref):
    x = x_ref[...]
    h = x * lax.rsqrt(jnp.mean(x * x, axis=-1, keepdims=True) + RMS_EPS) * g_ref[...]
    proj = jnp.dot(h.astype(BF16), w_ref[...], preferred_element_type=F32)
    ang = pos_ref[...].astype(F32) * invf_ref[...]
    lane = lax.broadcasted_iota(I32, ang.shape, 1)
    lo = (lane & (HEAD_DIM - 1)) < (HEAD_DIM // 2)
    cos = jnp.cos(ang)
    sin = jnp.sin(ang)
    sin_s = jnp.where(lo, -sin, sin)
    outs = {"qa": qa_ref, "ka": ka_ref, "va": va_ref, "qb": qb_ref, "kc": kc_ref, "vc": vc_ref,
            "ks": ks_ref, "vs": vs_ref, "kw": kw_ref, "vw": vw_ref}
    for name, width in _IN_SPLITS[:-1]:
        off = _IN_OFF[name]
        ref = outs[name]
        for i in range(width // LANES):
            v = proj[:, off + LANES * i: off + LANES * (i + 1)]
            if name in _ROPED:
                rot = jnp.where(lo, pltpu.roll(v, LANES - HEAD_DIM // 2, 1), pltpu.roll(v, HEAD_DIM // 2, 1))
                v = v * cos + rot * sin_s
            ref[:, LANES * i: LANES * (i + 1)] = v.astype(ref.dtype)
    gl = proj[:, _IN_OFF["gl"]: _IN_OFF["gl"] + LANES]
    gt_ref[...] = jax.nn.sigmoid(gl)


def _inproj(x2, pos2, g, w_pad, invf, tm):
    n, d = x2.shape
    row = lambda w: pl.BlockSpec((tm, w), lambda i: (i, 0))
    full = lambda a: pl.BlockSpec(a.shape, lambda i: (0,) * a.ndim)
    out_dt = {"qa": BF16, "ka": BF16, "va": BF16, "qb": BF16, "kc": F32, "vc": F32,
              "ks": BF16, "vs": BF16, "kw": BF16, "vw": BF16}
    names = [nm for nm, _ in _IN_SPLITS[:-1]]
    widths = dict(_IN_SPLITS)
    out_shape = [jax.ShapeDtypeStruct((n, widths[nm]), out_dt[nm]) for nm in names]
    out_shape.append(jax.ShapeDtypeStruct((n, LANES), F32))
    out_specs = [row(widths[nm]) for nm in names] + [row(LANES)]
    res = pl.pallas_call(
        _inproj_kernel,
        grid=(n // tm,),
        in_specs=[row(d), row(1), full(g), full(w_pad), full(invf)],
        out_specs=out_specs,
        out_shape=out_shape,
        compiler_params=_params("parallel"),
        name="inproj",
    )(x2, pos2, g, w_pad, invf)
    out = dict(zip(names, res[:-1]))
    out["gates"] = res[-1]
    return out


def _compress_kernel(kc_ref, vc_ref, kpos_ref, kw1_ref, kw2_ref, vpos_ref, vw1_ref, vw2_ref, ko_ref, vo_ref):
    ncp = ko_ref.shape[0]
    for src, pos_ref, w1_ref, w2_ref, out_ref in ((kc_ref, kpos_ref, kw1_ref, kw2_ref, ko_ref),
                                                  (vc_ref, vpos_ref, vw1_ref, vw2_ref, vo_ref)):
        bias = jnp.dot(pos_ref[...].astype(BF16), w1_ref[...], preferred_element_type=F32)
        acc_a = [jnp.zeros((ncp, CMP_HIDDEN), F32) for _ in range(2)]
        acc_b = [jnp.zeros((ncp, CMP_HIDDEN), F32) for _ in range(2)]
        for j in range(CMP_STRIDE):
            pj = src[pl.ds(j, ncp, stride=CMP_STRIDE), :].astype(BF16)
            wa = w1_ref[j * HEAD_DIM:(j + 1) * HEAD_DIM, :]
            wb = w1_ref[(CMP_STRIDE + j) * HEAD_DIM:(CMP_STRIDE + j + 1) * HEAD_DIM, :]
            for hh in range(2):
                ph = pj[:, hh * HEAD_DIM:(hh + 1) * HEAD_DIM]
                acc_a[hh] = acc_a[hh] + jnp.dot(ph, wa, preferred_element_type=F32)
                acc_b[hh] = acc_b[hh] + jnp.dot(ph, wb, preferred_element_type=F32)
        for hh in range(2):
            h1 = acc_a[hh] + pltpu.roll(acc_b[hh], ncp - 1, 0) + bias
            out = jnp.dot(_gelu(h1).astype(BF16), w2_ref[...], preferred_element_type=F32)
            out_ref[:, hh * HEAD_DIM:(hh + 1) * HEAD_DIM] = out


def _compress(kc3, vc3, kpos, kw1, kw2, vpos, vw1, vw2):
    b, t, w = kc3.shape
    ncp = t // CMP_STRIDE
    per_b = pl.BlockSpec((None, t, w), lambda i: (i, 0, 0))
    full = lambda a: pl.BlockSpec(a.shape, lambda i: (0,) * a.ndim)
    out_spec = pl.BlockSpec((None, ncp, w), lambda i: (i, 0, 0))
    return pl.pallas_call(
        _compress_kernel,
        grid=(b,),
        in_specs=[per_b, per_b, full(kpos), full(kw1), full(kw2), full(vpos), full(vw1), full(vw2)],
        out_specs=[out_spec, out_spec],
        out_shape=[jax.ShapeDtypeStruct((b, ncp, w), F32)] * 2,
        compiler_params=_params("parallel"),
        name="compress",
    )(kc3, vc3, kpos, kw1, kw2, vpos, vw1, vw2)


def _cmpsel_kernel(q_ref, kc_ref, vc_ref, ovt_ref, o_ref, sel_ref, *, tq, n_cmp):
    i = pl.program_id(1)
    t0 = i * tq
    g = NSA_HEADS // NSA_KV_HEADS
    ncp = kc_ref.shape[0]
    ns = ovt_ref.shape[0]
    rows = g * tq
    tpos = t0 + (lax.broadcasted_iota(I32, (rows, ncp), 0) & (tq - 1))
    cidx = lax.broadcasted_iota(I32, (rows, ncp), 1)
    vis = (cidx * CMP_STRIDE + (CMP_LEN - 1) <= tpos) & (cidx < n_cmp)
    jidx = lax.broadcasted_iota(I32, (ns, tq), 0)
    tcol = t0 + lax.broadcasted_iota(I32, (ns, tq), 1)
    cur = tcol // SEL_BLOCK
    forced = (jidx == 0) | (jidx == cur) | (jidx == cur - 1)
    valid = jidx * SEL_BLOCK <= tcol
    for hk in range(NSA_KV_HEADS):
        kk = kc_ref[:, hk * HEAD_DIM:(hk + 1) * HEAD_DIM].astype(BF16)
        vv = vc_ref[:, hk * HEAD_DIM:(hk + 1) * HEAD_DIM].astype(BF16)
        q4 = jnp.concatenate(
            [q_ref[:, (hk * g + gi) * HEAD_DIM:(hk * g + gi + 1) * HEAD_DIM] for gi in range(g)], axis=0)
        s = lax.dot_general(q4, kk, (((1,), (1,)), ((), ())), preferred_element_type=F32) * (HEAD_DIM ** -0.5)
        s = jnp.where(vis, s, NEG_BIG)
        m = jnp.max(s, axis=-1, keepdims=True)
        e = jnp.exp(s - m)
        p = jnp.where(vis, e / jnp.sum(e, axis=-1, keepdims=True), 0.0)
        o = jnp.dot(p.astype(BF16), vv, preferred_element_type=F32)
        for gi in range(g):
            h = hk * g + gi
            o_ref[:, h * HEAD_DIM:(h + 1) * HEAD_DIM] = o[gi * tq:(gi + 1) * tq]
        psum = p[0:tq]
        for gi in range(1, g):
            psum = psum + p[gi * tq:(gi + 1) * tq]
        imp_t = lax.dot_general(ovt_ref[...], psum, (((1,), (1,)), ((), ())),
                                precision=lax.Precision.HIGHEST, preferred_element_type=F32)
        score = jnp.where(forced, jnp.inf, jnp.where(valid, imp_t, -jnp.inf))
        cnt = jnp.zeros((ns, tq), F32)
        for r in range(ns):
            row = score[r:r + 1, :]
            beats = (row > score) | ((row >= score) & (jidx > r))
            cnt = cnt + jnp.where(beats, 1.0, 0.0)
        sel_ref[hk] = jnp.where((cnt < SEL_TOPN) & (score > -jnp.inf), 1.0, 0.0)


def _cmpsel(qb3, kcmp, vcmp, ovt, tq, n_cmp):
    b, t, w = qb3.shape
    ncp = kcmp.shape[1]
    ns = ovt.shape[0]
    kern = functools.partial(_cmpsel_kernel, tq=tq, n_cmp=n_cmp)
    return pl.pallas_call(
        kern,
        grid=(b, t // tq),
        in_specs=[pl.BlockSpec((None, tq, w), lambda bi, i: (bi, i, 0)),
                  pl.BlockSpec((None, ncp, LANES), lambda bi, i: (bi, 0, 0)),
                  pl.BlockSpec((None, ncp, LANES), lambda bi, i: (bi, 0, 0)),
                  pl.BlockSpec(ovt.shape, lambda bi, i: (0, 0))],
        out_specs=[pl.BlockSpec((None, tq, w), lambda bi, i: (bi, i, 0)),
                   pl.BlockSpec((None, NSA_KV_HEADS, ns, tq), lambda bi, i: (bi, 0, 0, i))],
        out_shape=[jax.ShapeDtypeStruct((b, t, w), F32), jax.ShapeDtypeStruct((b, NSA_KV_HEADS, ns, t), F32)],
        compiler_params=_params("parallel", "parallel"),
        name="cmpsel",
    )(qb3, kcmp, vcmp, ovt)


def _selattn_kernel(q_ref, k_ref, vt_ref, sel_ref, o_ref, m_sc, l_sc, acc_sc, *, tq, tk):
    i = pl.program_id(1)
    t0 = i * tq
    g = NSA_HEADS // NSA_KV_HEADS
    rows = g * tq
    n_kt = (t0 + tq + tk - 1) // tk
    nb = tk // SEL_BLOCK
    qpos = t0 + (lax.broadcasted_iota(I32, (tk, rows), 1) & (tq - 1))
    krow = lax.broadcasted_iota(I32, (tk, rows), 0)
    for hk in range(NSA_KV_HEADS):
        qt = jnp.concatenate(
            [jnp.transpose(q_ref[:, (hk * g + gi) * HEAD_DIM:(hk * g + gi + 1) * HEAD_DIM].astype(F32))
             for gi in range(g)], axis=1)
        qt = (qt * (HEAD_DIM ** -0.5)).astype(BF16)
        m_sc[...] = jnp.full(m_sc.shape, NEG_BIG, F32)
        l_sc[...] = jnp.zeros(l_sc.shape, F32)
        acc_sc[...] = jnp.zeros(acc_sc.shape, F32)

        def body(kt, carry):
            ks = pl.multiple_of(kt * tk, tk)
            kk = k_ref[pl.ds(ks, tk), hk * HEAD_DIM:(hk + 1) * HEAD_DIM]
            s = jnp.dot(kk, qt, preferred_element_type=F32)
            selm = jnp.concatenate(
                [jnp.broadcast_to(jnp.concatenate([sel_ref[hk, pl.ds(kt * nb + jj, 1), :]] * g, axis=1),
                                  (SEL_BLOCK, rows)) for jj in range(nb)], axis=0)
            mask = (selm > 0.5) & (ks + krow <= qpos)
            s = jnp.where(mask, s, NEG_BIG)
            m_old = m_sc[...]
            m_new = jnp.maximum(m_old, jnp.max(s, axis=0, keepdims=True))
            alpha = jnp.exp(m_old - m_new)
            p = jnp.where(mask, jnp.exp(s - m_new), 0.0)
            l_sc[...] = alpha * l_sc[...] + jnp.sum(p, axis=0, keepdims=True)
            vt = vt_ref[kt, hk * HEAD_DIM:(hk + 1) * HEAD_DIM, :]
            acc_sc[...] = alpha * acc_sc[...] + jnp.dot(vt, p.astype(BF16), preferred_element_type=F32)
            m_sc[...] = m_new
            return carry

        lax.fori_loop(0, n_kt, body, 0)
        o_t = acc_sc[...] / l_sc[...]
        for gi in range(g):
            h = hk * g + gi
            o_ref[:, h * HEAD_DIM:(h + 1) * HEAD_DIM] = jnp.transpose(o_t[:, gi * tq:(gi + 1) * tq])


def _selattn(qb3, ks3, vst4, sel4, tq, tk):
    b, t, w = qb3.shape
    g = NSA_HEADS // NSA_KV_HEADS
    ns = sel4.shape[2]
    kern = functools.partial(_selattn_kernel, tq=tq, tk=tk)
    return pl.pallas_call(
        kern,
        grid=(b, t // tq),
        in_specs=[pl.BlockSpec((None, tq, w), lambda bi, i: (bi, i, 0)),
                  pl.BlockSpec((None, t, LANES), lambda bi, i: (bi, 0, 0)),
                  pl.BlockSpec((None, t // tk, LANES, tk), lambda bi, i: (bi, 0, 0, 0)),
                  pl.BlockSpec((None, NSA_KV_HEADS, ns, tq), lambda bi, i: (bi, 0, 0, i))],
        out_specs=pl.BlockSpec((None, tq, w), lambda bi, i: (bi, i, 0)),
        out_shape=jax.ShapeDtypeStruct((b, t, w), F32),
        scratch_shapes=[pltpu.VMEM((1, g * tq), F32), pltpu.VMEM((1, g * tq), F32),
                        pltpu.VMEM((HEAD_DIM, g * tq), F32)],
        compiler_params=_params("parallel", "arbitrary"),
        name="selattn",
    )(qb3, ks3, vst4, sel4)


def _band_kernel(*refs, window, tq, span, n_heads, n_kv, has_sink):
    if has_sink:
        sink_ref, q_ref, k_ref, v_ref, o_ref = refs
    else:
        q_ref, k_ref, v_ref, o_ref = refs
    i = pl.program_id(1)
    t0 = i * tq
    start = pl.multiple_of(jnp.maximum(t0 + tq - span, 0), tq)
    g = n_heads // n_kv
    rows = g * tq
    qpos = t0 + (lax.broadcasted_iota(I32, (rows, span), 0) & (tq - 1))
    kpos = start + lax.broadcasted_iota(I32, (rows, span), 1)
    diff = qpos - kpos
    mask = (diff >= 0) & (diff < window)
    for hk in range(n_kv):
        kk = k_ref[pl.ds(start, span), hk * HEAD_DIM:(hk + 1) * HEAD_DIM]
        vv = v_ref[pl.ds(start, span), hk * HEAD_DIM:(hk + 1) * HEAD_DIM]
        q4 = jnp.concatenate(
            [q_ref[:, (hk * g + gi) * HEAD_DIM:(hk * g + gi + 1) * HEAD_DIM] for gi in range(g)], axis=0)
        s = lax.dot_general(q4, kk, (((1,), (1,)), ((), ())), preferred_element_type=F32) * (HEAD_DIM ** -0.5)
        s = jnp.where(mask, s, NEG_BIG)
        m = jnp.max(s, axis=-1, keepdims=True)
        if has_sink:
            sink = jnp.concatenate([jnp.full((tq, 1), sink_ref[hk * g + gi], F32) for gi in range(g)], axis=0)
            m = jnp.maximum(m, sink)
        p = jnp.where(mask, jnp.exp(s - m), 0.0)
        l = jnp.sum(p, axis=-1, keepdims=True)
        if has_sink:
            l = l + jnp.exp(sink - m)
        o = jnp.dot(p.astype(BF16), vv, preferred_element_type=F32) / l
        for gi in range(g):
            h = hk * g + gi
            o_ref[:, h * HEAD_DIM:(h + 1) * HEAD_DIM] = o[gi * tq:(gi + 1) * tq]


def _band(q3, k3, v3, sinks, window, tq, n_heads, n_kv):
    b, t, w = q3.shape
    nprev = -(-(window - 1) // tq)
    span = min((nprev + 1) * tq, t)
    has_sink = sinks is not None
    kern = functools.partial(_band_kernel, window=window, tq=tq, span=span, n_heads=n_heads, n_kv=n_kv,
                             has_sink=has_sink)
    in_specs = [pl.BlockSpec((None, tq, w), lambda bi, i: (bi, i, 0)),
                pl.BlockSpec((None, t, LANES), lambda bi, i: (bi, 0, 0)),
                pl.BlockSpec((None, t, LANES), lambda bi, i: (bi, 0, 0))]
    args = [q3, k3, v3]
    if has_sink:
        in_specs = [pl.BlockSpec(memory_space=pltpu.SMEM)] + in_specs
        args = [sinks] + args
    return pl.pallas_call(
        kern,
        grid=(b, t // tq),
        in_specs=in_specs,
        out_specs=pl.BlockSpec((None, tq, w), lambda bi, i: (bi, i, 0)),
        out_shape=jax.ShapeDtypeStruct((b, t, w), F32),
        compiler_params=_params("parallel", "parallel"),
        name="band_sink" if has_sink else "band_win",
    )(*args)


def _topk_rows(s, k, ridx):
    big = float(s.shape[0])
    vals, idxs = [], []
    for _ in range(k):
        m = jnp.max(s, axis=0, keepdims=True)
        first = jnp.min(jnp.where(s == m, ridx, big), axis=0, keepdims=True)
        vals.append(m)
        idxs.append(first)
        s = jnp.where(ridx == first, -jnp.inf, s)
    return jnp.concatenate(vals, axis=0), jnp.concatenate(idxs, axis=0)


def _peer_candidates(t1, i1, t2, i2):
    vals = [t1[0:1] + t2]
    eids = [i1[0:1] * float(N_KEYS) + i2]
    for a in range(1, 8):
        vals.append(t1[a:a + 1] + t2[0:8])
        eids.append(i1[a:a + 1] * float(N_KEYS) + i2[0:8])
    vals.append(t1[8:16] + t2[0:1])
    eids.append(i1[8:16] * float(N_KEYS) + i2[0:1])
    return jnp.concatenate(vals, axis=0), jnp.concatenate(eids, axis=0)


def _route_kernel(x_ref, oa_ref, oc_ref, os_ref, ow_ref, gt_ref, gexp_ref, wo_ref, gf_ref, wq_ref,
                  k1_ref, k2_ref, x1_ref, eidx_ref, gate_ref):
    tm = x_ref.shape[0]
    gt = gt_ref[...]
    g_hi = gt.astype(BF16)
    g_lo = (gt - g_hi.astype(F32)).astype(BF16)
    ob = jnp.zeros(oc_ref.shape, F32)
    for j, br in enumerate((oc_ref, os_ref, ow_ref)):
        ex = gexp_ref[j]
        gj = (jnp.dot(g_hi, ex, preferred_element_type=F32) + jnp.dot(g_lo, ex, preferred_element_type=F32))
        ob = ob + gj * br[...]
    half = oa_ref.shape[1]
    mixed = (jnp.dot(oa_ref[...].astype(BF16), wo_ref[0:half, :], preferred_element_type=F32)
             + jnp.dot(ob.astype(BF16), wo_ref[half:, :], preferred_element_type=F32))
    x1 = x_ref[...] + mixed
    x1_ref[...] = x1
    h2 = x1 * lax.rsqrt(jnp.mean(x1 * x1, axis=-1, keepdims=True) + RMS_EPS) * gf_ref[...]
    q = jnp.dot(h2.astype(BF16), wq_ref[...], preferred_element_type=F32)
    hq = PEER_QDIM // 2
    ridx = lax.broadcasted_iota(I32, (N_KEYS, tm), 0).astype(F32)
    e_rows, g_rows = [], []
    for h in range(PEER_HEADS):
        q1 = q[:, h * PEER_QDIM: h * PEER_QDIM + hq].astype(BF16)
        q2 = q[:, h * PEER_QDIM + hq:(h + 1) * PEER_QDIM].astype(BF16)
        s1 = lax.dot_general(k1_ref[...], q1, (((1,), (1,)), ((), ())), preferred_element_type=F32)
        s2 = lax.dot_general(k2_ref[...], q2, (((1,), (1,)), ((), ())), preferred_element_type=F32)
        t1, i1 = _topk_rows(s1, PEER_TOPK, ridx)
        t2, i2 = _topk_rows(s2, PEER_TOPK, ridx)
        cand, cand_e = _peer_candidates(t1, i1, t2, i2)
        cidx = lax.broadcasted_iota(I32, cand.shape, 0).astype(F32)
        big = float(cand.shape[0])
        sc, ee = [], []
        for _ in range(PEER_TOPK):
            m = jnp.max(cand, axis=0, keepdims=True)
            first = jnp.min(jnp.where(cand == m, cidx, big), axis=0, keepdims=True)
            hit = cidx == first
            sc.append(m)
            ee.append(jnp.max(jnp.where(hit, cand_e, -1.0), axis=0, keepdims=True))
            cand = jnp.where(hit, -jnp.inf, cand)
        sc = jnp.concatenate(sc, axis=0)
        ex = jnp.exp(sc - sc[0:1])
        g_rows.append(ex / jnp.sum(ex, axis=0, keepdims=True))
        e_rows.append(jnp.concatenate(ee, axis=0))
    e_t = jnp.concatenate(e_rows, axis=0)
    g_t = jnp.concatenate(g_rows, axis=0)
    eidx_ref[...] = jnp.transpose(e_t).astype(I32)
    gate_ref[...] = jnp.transpose(g_t)


def _route(x2, oa, oc, osel, ow, gates, gexp, wo, gf, wq, k1, k2, tm):
    n, d = x2.shape
    row = lambda w: pl.BlockSpec((tm, w), lambda i: (i, 0))
    full = lambda a: pl.BlockSpec(a.shape, lambda i: (0,) * a.ndim)
    nk = PEER_HEADS * PEER_TOPK
    return pl.pallas_call(
        _route_kernel,
        grid=(n // tm,),
        in_specs=[row(d), row(oa.shape[1]), row(oc.shape[1]), row(osel.shape[1]), row(ow.shape[1]), row(LANES),
                  full(gexp), full(wo), full(gf), full(wq), full(k1), full(k2)],
        out_specs=[row(d), row(nk), row(nk)],
        out_shape=[jax.ShapeDtypeStruct((n, d), F32), jax.ShapeDtypeStruct((n, nk), I32),
                   jax.ShapeDtypeStruct((n, nk), F32)],
        compiler_params=_params("parallel"),
        name="route",
    )(x2, oa, oc, osel, ow, gates, gexp, wo, gf, wq, k1, k2)


def _fold8(p):
    sub = lax.broadcasted_iota(I32, (SUBLANES, LANES), 0)
    lo4 = sub < 4
    q = [jnp.where(lo4, p[j], p[j + 4]) + pltpu.roll(jnp.where(lo4, p[j + 4], p[j]), 4, 0) for j in range(4)]
    m2 = (sub & 3) < 2
    r = [jnp.where(m2, q[j] + pltpu.roll(q[j], 6, 0), q[j + 2] + pltpu.roll(q[j + 2], 2, 0)) for j in range(2)]
    m1 = (sub & 1) == 0
    return jnp.where(m1, r[0] + pltpu.roll(r[0], 7, 0), r[1] + pltpu.roll(r[1], 1, 0))


def _peer_kernel(ecur_ref, enxt_ref, x_ref, gate_ref, gffn_ref, gfin_ref, tab_ref, o_ref,
                 buf, sem, a_sc, cb_sc, *, tb):
    step = pl.program_id(0)
    nsteps = pl.num_programs(0)
    slot = step % 2
    nk = PEER_HEADS * PEER_TOPK
    npair = tb * nk

    def issue_rows(eref, dst_slot, t, k0, k1):
        for k in range(k0, k1):
            pltpu.make_async_copy(tab_ref.at[eref[t, k]], buf.at[dst_slot, t * nk + k], sem.at[dst_slot]).start()

    def wait_slot(s):
        pltpu.make_async_copy(tab_ref.at[pl.ds(0, npair)], buf.at[s], sem.at[s]).wait()

    @pl.when(step == 0)
    def _():
        def first(t, carry):
            issue_rows(ecur_ref, 0, t, 0, nk)
            return carry
        lax.fori_loop(0, tb, first, 0)

    eye = jnp.where(lax.broadcasted_iota(I32, (nk, nk), 0) == lax.broadcasted_iota(I32, (nk, nk), 1),
                    1.0, 0.0).astype(BF16)
    gr = gate_ref[...]
    g0 = gr.astype(BF16)
    r1 = gr - g0.astype(F32)
    g1 = r1.astype(BF16)
    g2 = (r1 - g1.astype(F32)).astype(BF16)
    dn = (((1,), (1,)), ((), ()))
    gate_t = (lax.dot_general(eye, g0, dn, preferred_element_type=F32)
              + lax.dot_general(eye, g1, dn, preferred_element_type=F32)
              + lax.dot_general(eye, g2, dn, preferred_element_type=F32))

    wait_slot(slot)

    lane_tb = lax.broadcasted_iota(I32, (nk, tb), 1)
    inv_d = 1.0 / (SUBLANES * LANES)
    a_sc[...] = jnp.zeros(a_sc.shape, F32)

    ngroup = nk // SUBLANES
    per_group = nk // (2 * ngroup)

    def dots(t, carry):
        x8 = x_ref[t]
        ms = jnp.sum(jnp.sum(x8 * x8, axis=1, keepdims=True), axis=0, keepdims=True) * inv_d
        h8 = x8 * lax.rsqrt(ms + RMS_EPS) * gffn_ref[...]
        base = t * nk
        folded = []
        for gi in range(nk // SUBLANES):
            blk = buf[slot, pl.ds(base + gi * SUBLANES, SUBLANES), 0:SUBLANES, :]
            folded.append(_fold8([blk[j] * h8 for j in range(SUBLANES)]))
            issue_rows(enxt_ref, 1 - slot, t, gi * per_group, (gi + 1) * per_group)
        a = jnp.sum(jnp.concatenate(folded, axis=0), axis=1, keepdims=True)
        a_sc[...] = jnp.where(lane_tb == t, a, a_sc[...])
        return carry

    lax.fori_loop(0, tb, dots, 0)

    c_all = gate_t * _gelu(a_sc[...])
    for t in range(tb):
        cb_sc[t] = jnp.broadcast_to(c_all[:, t:t + 1], (nk, LANES))

    def combine(t, carry):
        base = t * nk
        accs = [jnp.zeros((SUBLANES, LANES), F32) for _ in range(4)]
        for gi in range(ngroup):
            for k in range(gi * SUBLANES, (gi + 1) * SUBLANES):
                ck = jnp.broadcast_to(cb_sc[t, k:k + 1, :], (SUBLANES, LANES))
                accs[k % 4] = accs[k % 4] + ck * buf[slot, base + k, SUBLANES:2 * SUBLANES, :]
            issue_rows(enxt_ref, 1 - slot, t, nk // 2 + gi * per_group, nk // 2 + (gi + 1) * per_group)
        y8 = x_ref[t] + ((accs[0] + accs[1]) + (accs[2] + accs[3]))
        ms2 = jnp.sum(jnp.sum(y8 * y8, axis=1, keepdims=True), axis=0, keepdims=True) * inv_d
        o_ref[t] = y8 * lax.rsqrt(ms2 + RMS_EPS) * gfin_ref[...]
        return carry

    lax.fori_loop(0, tb, combine, 0)

    @pl.when(step == nsteps - 1)
    def _():
        wait_slot(1 - slot)


def _peer(eidx, x1s, gate, gffn8, gfin8, table, tb):
    n = x1s.shape[0]
    nk = PEER_HEADS * PEER_TOPK
    nsteps = n // tb
    kern = functools.partial(_peer_kernel, tb=tb)
    return pl.pallas_call(
        kern,
        grid=(nsteps,),
        in_specs=[pl.BlockSpec((tb, nk), lambda i: (i, 0), memory_space=pltpu.SMEM),
                  pl.BlockSpec((tb, nk), lambda i: ((i + 1) % nsteps, 0), memory_space=pltpu.SMEM),
                  pl.BlockSpec((tb, SUBLANES, LANES), lambda i: (i, 0, 0)),
                  pl.BlockSpec((tb, nk), lambda i: (i, 0)),
                  pl.BlockSpec((SUBLANES, LANES), lambda i: (0, 0)),
                  pl.BlockSpec((SUBLANES, LANES), lambda i: (0, 0)),
                  pl.BlockSpec(memory_space=pl.ANY)],
        out_specs=pl.BlockSpec((tb, SUBLANES, LANES), lambda i: (i, 0, 0)),
        out_shape=jax.ShapeDtypeStruct((n, SUBLANES, LANES), F32),
        scratch_shapes=[pltpu.VMEM((2, tb * nk, 2 * SUBLANES, LANES), F32),
                        pltpu.SemaphoreType.DMA((2,)),
                        pltpu.VMEM((nk, tb), F32),
                        pltpu.VMEM((tb, nk, LANES), F32)],
        compiler_params=_params("arbitrary"),
        name="peer",
    )(eidx, eidx, x1s, gate, gffn8, gfin8, table)


def _overlap_t(t):
    nc = (t - CMP_LEN) // CMP_STRIDE + 1
    ns = t // SEL_BLOCK
    ncp = t // CMP_STRIDE
    cstart = np.arange(nc) * CMP_STRIDE
    bstart = np.arange(ns) * SEL_BLOCK
    lo = np.maximum(cstart[:, None], bstart[None, :])
    hi = np.minimum(cstart[:, None] + CMP_LEN, bstart[None, :] + SEL_BLOCK)
    ov = np.clip(hi - lo, 0, None).astype(np.float32) / CMP_LEN
    out = np.zeros((ns, ncp), np.float32)
    out[:, :nc] = ov.T
    return out, nc


def _gate_expand():
    ex = np.zeros((NSA_N_GATES, LANES, NSA_HEADS * HEAD_DIM), np.float32)
    for j in range(NSA_N_GATES):
        for h in range(NSA_HEADS):
            ex[j, h * NSA_N_GATES + j, h * HEAD_DIM:(h + 1) * HEAD_DIM] = 1.0
    return ex


def _layer(x, positions, norm_attn, w_in, attn_sinks, cmp_k_pos, cmp_k_w1, cmp_k_w2, cmp_v_pos, cmp_v_w1,
           cmp_v_w2, w_out, norm_ffn, w_query, sub_keys_1, sub_keys_2, expert_down, expert_up, norm_out,
           tm_in=512, tq_cmp=256, tq_sel=256, tq_band=128, tm_route=256, tb_peer=8):
    b, t, d = x.shape
    n = b * t
    x2 = x.reshape(n, d)
    half = HEAD_DIM // 2
    inv_freq = ROPE_THETA ** (-jnp.arange(half, dtype=F32) / half)
    invf = jnp.tile(inv_freq, LANES // half).reshape(1, LANES)
    w_pad = jnp.pad(w_in, ((0, 0), (0, IN_PAD - IN_WIDTH))).astype(BF16)
    pr = _inproj(x2, positions.reshape(n, 1), norm_attn.reshape(1, d), w_pad, invf, min(tm_in, n))
    r3 = lambda a: a.reshape(b, t, a.shape[-1])

    o_a = _band(r3(pr["qa"]), r3(pr["ka"]), r3(pr["va"]), attn_sinks.astype(F32), SWA_WINDOW, tq_band,
                SWA_HEADS, SWA_KV_HEADS)
    kcmp, vcmp = _compress(r3(pr["kc"]), r3(pr["vc"]),
                           cmp_k_pos.reshape(1, -1), cmp_k_w1.astype(BF16), cmp_k_w2.astype(BF16),
                           cmp_v_pos.reshape(1, -1), cmp_v_w1.astype(BF16), cmp_v_w2.astype(BF16))
    ovt, n_cmp = _overlap_t(t)
    qb3 = r3(pr["qb"])
    o_cmp, sel = _cmpsel(qb3, kcmp, vcmp, jnp.asarray(ovt), tq_cmp, n_cmp)
    vst4 = jnp.swapaxes(r3(pr["vs"]).reshape(b, t // tq_sel, tq_sel, LANES), 2, 3)
    o_sel = _selattn(qb3, r3(pr["ks"]), vst4, sel, tq_sel, tq_sel)
    o_win = _band(qb3, r3(pr["kw"]), r3(pr["vw"]), None, NSA_WINDOW, tq_band, NSA_HEADS, NSA_KV_HEADS)

    x1, eidx, gate = _route(x2, o_a.reshape(n, -1), o_cmp.reshape(n, -1), o_sel.reshape(n, -1),
                            o_win.reshape(n, -1), pr["gates"], jnp.asarray(_gate_expand(), BF16),
                            w_out.astype(BF16), norm_ffn.reshape(1, d), w_query.astype(BF16),
                            sub_keys_1.astype(BF16), sub_keys_2.astype(BF16), min(tm_route, n))
    ne = expert_down.shape[0]
    table = jnp.concatenate([expert_down.reshape(ne, SUBLANES, LANES), expert_up.reshape(ne, SUBLANES, LANES)],
                            axis=1)
    y = _peer(eidx, x1.reshape(n, SUBLANES, LANES), gate, norm_ffn.reshape(SUBLANES, LANES),
              norm_out.reshape(SUBLANES, LANES), table, tb_peer)
    return y.reshape(b, t, d)


def kernel(x, positions, norm_attn, w_in, attn_sinks, cmp_k_pos, cmp_k_w1, cmp_k_w2, cmp_v_pos, cmp_v_w1,
           cmp_v_w2, w_out, norm_ffn, peer_w_query, peer_sub_keys_1, peer_sub_keys_2, peer_expert_down,
           peer_expert_up, norm_f):
    assert norm_attn.shape[0] == 1, "single-layer block"
    return _layer(x, positions, norm_attn[0], w_in[0], attn_sinks[0], cmp_k_pos[0], cmp_k_w1[0], cmp_k_w2[0],
                  cmp_v_pos[0], cmp_v_w1[0], cmp_v_w2[0], w_out[0], norm_ffn[0], peer_w_query[0],
                  peer_sub_keys_1[0], peer_sub_keys_2[0], peer_expert_down[0], peer_expert_up[0], norm_f)
```

```python
import functools

import numpy as np
import jax
import jax.numpy as jnp
from jax import lax
from jax.experimental import pallas as pl
from jax.experimental.pallas import tpu as pltpu

F32 = jnp.float32
BF16 = jnp.bfloat16
I32 = jnp.int32

HEAD_DIM = 64
ROPE_THETA = 10000.0
RMS_EPS = 1e-6
NEG_BIG = -1e30

SWA_HEADS = 8
SWA_KV_HEADS = 2
SWA_WINDOW = 128

NSA_HEADS = 8
NSA_KV_HEADS = 2
CMP_LEN = 32
CMP_STRIDE = 16
CMP_HIDDEN = 256
SEL_BLOCK = 64
SEL_TOPN = 16
NSA_WINDOW = 512
NSA_N_GATES = 3

PEER_HEADS = 8
N_KEYS = 128
PEER_QDIM = 256
PEER_TOPK = 16

LANES = 128
SUBLANES = 8
VMEM_LIMIT = 56 * 1024 * 1024

_IN_SPLITS = (("qa", 512), ("ka", 128), ("va", 128), ("qb", 512), ("kc", 128), ("vc", 128),
              ("ks", 128), ("vs", 128), ("kw", 128), ("vw", 128), ("gl", 24))
IN_WIDTH = sum(w for _, w in _IN_SPLITS)
IN_PAD = 2176
_IN_OFF = {}
_o = 0
for _n, _w in _IN_SPLITS:
    _IN_OFF[_n] = _o
    _o += _w
_ROPED = ("qa", "ka", "qb", "kc", "ks", "kw")


def _gelu(x):
    return 0.5 * x * (1.0 + lax.erf(x * 0.7071067811865476))


def _params(*sem):
    return pltpu.CompilerParams(dimension_semantics=sem, vmem_limit_bytes=VMEM_LIMIT)


def _inproj_kernel(x_ref, pos_ref, g_ref, w_ref, invf_ref,
                   qa_ref, ka_ref, va_ref, qb_ref, kc_ref, vc_ref, ks_ref, vs_ref, kw_ref, vw_ref, gt_ref):
    x = x_ref[...]
    h = x * lax.rsqrt(jnp.mean(x * x, axis=-1, keepdims=True) + RMS_EPS) * g_ref[...]
    proj = jnp.dot(h.astype(BF16), w_ref[...], preferred_element_type=F32)
    ang = pos_ref[...].astype(F32) * invf_ref[...]
    lane = lax.broadcasted_iota(I32, ang.shape, 1)
    lo = (lane & (HEAD_DIM - 1)) < (HEAD_DIM // 2)
    cos = jnp.cos(ang)
    sin = jnp.sin(ang)
    sin_s = jnp.where(lo, -sin, sin)
    outs = {"qa": qa_ref, "ka": ka_ref, "va": va_ref, "qb": qb_ref, "kc": kc_ref, "vc": vc_ref,
            "ks": ks_ref, "vs": vs_ref, "kw": kw_ref, "vw": vw_ref}
    for name, width in _IN_SPLITS[:-1]:
        off = _IN_OFF[name]
        ref = outs[name]
        for i in range(width // LANES):
            v = proj[:, off + LANES * i: off + LANES * (i + 1)]
            if name in _ROPED:
                rot = jnp.where(lo, pltpu.roll(v, LANES - HEAD_DIM // 2, 1), pltpu.roll(v, HEAD_DIM // 2, 1))
                v = v * cos + rot * sin_s
            ref[:, LANES * i: LANES * (i + 1)] = v.astype(ref.dtype)
    gl = proj[:, _IN_OFF["gl"]: _IN_OFF["gl"] + LANES]
    gt_ref[...] = jax.nn.sigmoid(gl)


def _inproj(x2, pos2, g, w_pad, invf, tm):
    n, d = x2.shape
    row = lambda w: pl.BlockSpec((tm, w), lambda i: (i, 0))
    full = lambda a: pl.BlockSpec(a.shape, lambda i: (0,) * a.ndim)
    out_dt = {"qa": BF16, "ka": BF16, "va": BF16, "qb": BF16, "kc": F32, "vc": F32,
              "ks": BF16, "vs": BF16, "kw": BF16, "vw": BF16}
    names = [nm for nm, _ in _IN_SPLITS[:-1]]
    widths = dict(_IN_SPLITS)
    out_shape = [jax.ShapeDtypeStruct((n, widths[nm]), out_dt[nm]) for nm in names]
    out_shape.append(jax.ShapeDtypeStruct((n, LANES), F32))
    out_specs = [row(widths[nm]) for nm in names] + [row(LANES)]
    res = pl.pallas_call(
        _inproj_kernel,
        grid=(n // tm,),
        in_specs=[row(d), row(1), full(g), full(w_pad), full(invf)],
        out_specs=out_specs,
        out_shape=out_shape,
        compiler_params=_params("parallel"),
        name="inproj",
    )(x2, pos2, g, w_pad, invf)
    out = dict(zip(names, res[:-1]))
    out["gates"] = res[-1]
    return out


def _compress_kernel(kc_ref, vc_ref, kpos_ref, kw1_ref, kw2_ref, vpos_ref, vw1_ref, vw2_ref, ko_ref, vo_ref):
    ncp = ko_ref.shape[0]
    for src, pos_ref, w1_ref, w2_ref, out_ref in ((kc_ref, kpos_ref, kw1_ref, kw2_ref, ko_ref),
                                                  (vc_ref, vpos_ref, vw1_ref, vw2_ref, vo_ref)):
        bias = jnp.dot(pos_ref[...].astype(BF16), w1_ref[...], preferred_element_type=F32)
        acc_a = [jnp.zeros((ncp, CMP_HIDDEN), F32) for _ in range(2)]
        acc_b = [jnp.zeros((ncp, CMP_HIDDEN), F32) for _ in range(2)]
        for j in range(CMP_STRIDE):
            pj = src[pl.ds(j, ncp, stride=CMP_STRIDE), :].astype(BF16)
            wa = w1_ref[j * HEAD_DIM:(j + 1) * HEAD_DIM, :]
            wb = w1_ref[(CMP_STRIDE + j) * HEAD_DIM:(CMP_STRIDE + j + 1) * HEAD_DIM, :]
            for hh in range(2):
                ph = pj[:, hh * HEAD_DIM:(hh + 1) * HEAD_DIM]
                acc_a[hh] = acc_a[hh] + jnp.dot(ph, wa, preferred_element_type=F32)
                acc_b[hh] = acc_b[hh] + jnp.dot(ph, wb, preferred_element_type=F32)
        for hh in range(2):
            h1 = acc_a[hh] + pltpu.roll(acc_b[hh], ncp - 1, 0) + bias
            out = jnp.dot(_gelu(h1).astype(BF16), w2_ref[...], preferred_element_type=F32)
            out_ref[:, hh * HEAD_DIM:(hh + 1) * HEAD_DIM] = out


def _compress(kc3, vc3, kpos, kw1, kw2, vpos, vw1, vw2):
    b, t, w = kc3.shape
    ncp = t // CMP_STRIDE
    per_b = pl.BlockSpec((None, t, w), lambda i: (i, 0, 0))
    full = lambda a: pl.BlockSpec(a.shape, lambda i: (0,) * a.ndim)
    out_spec = pl.BlockSpec((None, ncp, w), lambda i: (i, 0, 0))
    return pl.pallas_call(
        _compress_kernel,
        grid=(b,),
        in_specs=[per_b, per_b, full(kpos), full(kw1), full(kw2), full(vpos), full(vw1), full(vw2)],
        out_specs=[out_spec, out_spec],
        out_shape=[jax.ShapeDtypeStruct((b, ncp, w), F32)] * 2,
        compiler_params=_params("parallel"),
        name="compress",
    )(kc3, vc3, kpos, kw1, kw2, vpos, vw1, vw2)


def _cmpsel_kernel(q_ref, kc_ref, vc_ref, ovt_ref, o_ref, sel_ref, *, tq, n_cmp):
    i = pl.program_id(1)
    t0 = i * tq
    g = NSA_HEADS // NSA_KV_HEADS
    ncp = kc_ref.shape[0]
    ns = ovt_ref.shape[0]
    rows = g * tq
    tpos = t0 + (lax.broadcasted_iota(I32, (rows, ncp), 0) & (tq - 1))
    cidx = lax.broadcasted_iota(I32, (rows, ncp), 1)
    vis = (cidx * CMP_STRIDE + (CMP_LEN - 1) <= tpos) & (cidx < n_cmp)
    jidx = lax.broadcasted_iota(I32, (ns, tq), 0)
    tcol = t0 + lax.broadcasted_iota(I32, (ns, tq), 1)
    cur = tcol // SEL_BLOCK
    forced = (jidx == 0) | (jidx == cur) | (jidx == cur - 1)
    valid = jidx * SEL_BLOCK <= tcol
    for hk in range(NSA_KV_HEADS):
        kk = kc_ref[:, hk * HEAD_DIM:(hk + 1) * HEAD_DIM].astype(BF16)
        vv = vc_ref[:, hk * HEAD_DIM:(hk + 1) * HEAD_DIM].astype(BF16)
        q4 = jnp.concatenate(
            [q_ref[:, (hk * g + gi) * HEAD_DIM:(hk * g + gi + 1) * HEAD_DIM] for gi in range(g)], axis=0)
        s = lax.dot_general(q4, kk, (((1,), (1,)), ((), ())), preferred_element_type=F32) * (HEAD_DIM ** -0.5)
        s = jnp.where(vis, s, NEG_BIG)
        m = jnp.max(s, axis=-1, keepdims=True)
        e = jnp.exp(s - m)
        p = jnp.where(vis, e / jnp.sum(e, axis=-1, keepdims=True), 0.0)
        o = jnp.dot(p.astype(BF16), vv, preferred_element_type=F32)
        for gi in range(g):
            h = hk * g + gi
            o_ref[:, h * HEAD_DIM:(h + 1) * HEAD_DIM] = o[gi * tq:(gi + 1) * tq]
        psum = p[0:tq]
        for gi in range(1, g):
            psum = psum + p[gi * tq:(gi + 1) * tq]
        imp_t = lax.dot_general(ovt_ref[...], psum, (((1,), (1,)), ((), ())),
                                precision=lax.Precision.HIGHEST, preferred_element_type=F32)
        score = jnp.where(forced, jnp.inf, jnp.where(valid, imp_t, -jnp.inf))
        cnt = jnp.zeros((ns, tq), F32)
        for r in range(ns):
            row = score[r:r + 1, :]
            beats = (row > score) | ((row >= score) & (jidx > r))
            cnt = cnt + jnp.where(beats, 1.0, 0.0)
        sel_ref[hk] = jnp.where((cnt < SEL_TOPN) & (score > -jnp.inf), 1.0, 0.0)


def _cmpsel(qb3, kcmp, vcmp, ovt, tq, n_cmp):
    b, t, w = qb3.shape
    ncp = kcmp.shape[1]
    ns = ovt.shape[0]
    kern = functools.partial(_cmpsel_kernel, tq=tq, n_cmp=n_cmp)
    return pl.pallas_call(
        kern,
        grid=(b, t // tq),
        in_specs=[pl.BlockSpec((None, tq, w), lambda bi, i: (bi, i, 0)),
                  pl.BlockSpec((None, ncp, LANES), lambda bi, i: (bi, 0, 0)),
                  pl.BlockSpec((None, ncp, LANES), lambda bi, i: (bi, 0, 0)),
                  pl.BlockSpec(ovt.shape, lambda bi, i: (0, 0))],
        out_specs=[pl.BlockSpec((None, tq, w), lambda bi, i: (bi, i, 0)),
                   pl.BlockSpec((None, NSA_KV_HEADS, ns, tq), lambda bi, i: (bi, 0, 0, i))],
        out_shape=[jax.ShapeDtypeStruct((b, t, w), F32), jax.ShapeDtypeStruct((b, NSA_KV_HEADS, ns, t), F32)],
        compiler_params=_params("parallel", "parallel"),
        name="cmpsel",
    )(qb3, kcmp, vcmp, ovt)


def _selattn_kernel(q_ref, k_ref, vt_ref, sel_ref, o_ref, m_sc, l_sc, acc_sc, *, tq, tk):
    i = pl.program_id(1)
    t0 = i * tq
    g = NSA_HEADS // NSA_KV_HEADS
    rows = g * tq
    n_kt = (t0 + tq + tk - 1) // tk
    nb = tk // SEL_BLOCK
    qpos = t0 + (lax.broadcasted_iota(I32, (tk, rows), 1) & (tq - 1))
    krow = lax.broadcasted_iota(I32, (tk, rows), 0)
    for hk in range(NSA_KV_HEADS):
        qt = jnp.concatenate(
            [jnp.transpose(q_ref[:, (hk * g + gi) * HEAD_DIM:(hk * g + gi + 1) * HEAD_DIM].astype(F32))
             for gi in range(g)], axis=1)
        qt = (qt * (HEAD_DIM ** -0.5)).astype(BF16)
        m_sc[...] = jnp.full(m_sc.shape, NEG_BIG, F32)
        l_sc[...] = jnp.zeros(l_sc.shape, F32)
        acc_sc[...] = jnp.zeros(acc_sc.shape, F32)

        def body(kt, carry):
            ks = pl.multiple_of(kt * tk, tk)
            kk = k_ref[pl.ds(ks, tk), hk * HEAD_DIM:(hk + 1) * HEAD_DIM]
            s = jnp.dot(kk, qt, preferred_element_type=F32)
            selm = jnp.concatenate(
                [jnp.broadcast_to(jnp.concatenate([sel_ref[hk, pl.ds(kt * nb + jj, 1), :]] * g, axis=1),
                                  (SEL_BLOCK, rows)) for jj in range(nb)], axis=0)
            mask = (selm > 0.5) & (ks + krow <= qpos)
            s = jnp.where(mask, s, NEG_BIG)
            m_old = m_sc[...]
            m_new = jnp.maximum(m_old, jnp.max(s, axis=0, keepdims=True))
            alpha = jnp.exp(m_old - m_new)
            p = jnp.where(mask, jnp.exp(s - m_new), 0.0)
            l_sc[...] = alpha * l_sc[...] + jnp.sum(p, axis=0, keepdims=True)
            vt = vt_ref[kt, hk * HEAD_DIM:(hk + 1) * HEAD_DIM, :]
            acc_sc[...] = alpha * acc_sc[...] + jnp.dot(vt, p.astype(BF16), preferred_element_type=F32)
            m_sc[...] = m_new
            return carry

        lax.fori_loop(0, n_kt, body, 0)
        o_t = acc_sc[...] / l_sc[...]
        for gi in range(g):
            h = hk * g + gi
            o_ref[:, h * HEAD_DIM:(h + 1) * HEAD_DIM] = jnp.transpose(o_t[:, gi * tq:(gi + 1) * tq])


def _selattn(qb3, ks3, vst4, sel4, tq, tk):
    b, t, w = qb3.shape
    g = NSA_HEADS // NSA_KV_HEADS
    ns = sel4.shape[2]
    kern = functools.partial(_selattn_kernel, tq=tq, tk=tk)
    return pl.pallas_call(
        kern,
        grid=(b, t // tq),
        in_specs=[pl.BlockSpec((None, tq, w), lambda bi, i: (bi, i, 0)),
                  pl.BlockSpec((None, t, LANES), lambda bi, i: (bi, 0, 0)),
                  pl.BlockSpec((None, t // tk, LANES, tk), lambda bi, i: (bi, 0, 0, 0)),
                  pl.BlockSpec((None, NSA_KV_HEADS, ns, tq), lambda bi, i: (bi, 0, 0, i))],
        out_specs=pl.BlockSpec((None, tq, w), lambda bi, i: (bi, i, 0)),
        out_shape=jax.ShapeDtypeStruct((b, t, w), F32),
        scratch_shapes=[pltpu.VMEM((1, g * tq), F32), pltpu.VMEM((1, g * tq), F32),
                        pltpu.VMEM((HEAD_DIM, g * tq), F32)],
        compiler_params=_params("parallel", "arbitrary"),
        name="selattn",
    )(qb3, ks3, vst4, sel4)


def _band_kernel(*refs, window, tq, span, n_heads, n_kv, has_sink):
    if has_sink:
        sink_ref, q_ref, k_ref, v_ref, o_ref = refs
    else:
        q_ref, k_ref, v_ref, o_ref = refs
    i = pl.program_id(1)
    t0 = i * tq
    start = pl.multiple_of(jnp.maximum(t0 + tq - span, 0), tq)
    g = n_heads // n_kv
    rows = g * tq
    qpos = t0 + (lax.broadcasted_iota(I32, (rows, span), 0) & (tq - 1))
    kpos = start + lax.broadcasted_iota(I32, (rows, span), 1)
    diff = qpos - kpos
    mask = (diff >= 0) & (diff < window)
    for hk in range(n_kv):
        kk = k_ref[pl.ds(start, span), hk * HEAD_DIM:(hk + 1) * HEAD_DIM]
        vv = v_ref[pl.ds(start, span), hk * HEAD_DIM:(hk + 1) * HEAD_DIM]
        q4 = jnp.concatenate(
            [q_ref[:, (hk * g + gi) * HEAD_DIM:(hk * g + gi + 1) * HEAD_DIM] for gi in range(g)], axis=0)
        s = lax.dot_general(q4, kk, (((1,), (1,)), ((), ())), preferred_element_type=F32) * (HEAD_DIM ** -0.5)
        s = jnp.where(mask, s, NEG_BIG)
        m = jnp.max(s, axis=-1, keepdims=True)
        if has_sink:
            sink = jnp.concatenate([jnp.full((tq, 1), sink_ref[hk * g + gi], F32) for gi in range(g)], axis=0)
            m = jnp.maximum(m, sink)
        p = jnp.where(mask, jnp.exp(s - m), 0.0)
        l = jnp.sum(p, axis=-1, keepdims=True)
        if has_sink:
            l = l + jnp.exp(sink - m)
        o = jnp.dot(p.astype(BF16), vv, preferred_element_type=F32) / l
        for gi in range(g):
            h = hk * g + gi
            o_ref[:, h * HEAD_DIM:(h + 1) * HEAD_DIM] = o[gi * tq:(gi + 1) * tq]


def _band(q3, k3, v3, sinks, window, tq, n_heads, n_kv):
    b, t, w = q3.shape
    nprev = -(-(window - 1) // tq)
    span = min((nprev + 1) * tq, t)
    has_sink = sinks is not None
    kern = functools.partial(_band_kernel, window=window, tq=tq, span=span, n_heads=n_heads, n_kv=n_kv,
                             has_sink=has_sink)
    in_specs = [pl.BlockSpec((None, tq, w), lambda bi, i: (bi, i, 0)),
                pl.BlockSpec((None, t, LANES), lambda bi, i: (bi, 0, 0)),
                pl.BlockSpec((None, t, LANES), lambda bi, i: (bi, 0, 0))]
    args = [q3, k3, v3]
    if has_sink:
        in_specs = [pl.BlockSpec(memory_space=pltpu.SMEM)] + in_specs
        args = [sinks] + args
    return pl.pallas_call(
        kern,
        grid=(b, t // tq),
        in_specs=in_specs,
        out_specs=pl.BlockSpec((None, tq, w), lambda bi, i: (bi, i, 0)),
        out_shape=jax.ShapeDtypeStruct((b, t, w), F32),
        compiler_params=_params("parallel", "parallel"),
        name="band_sink" if has_sink else "band_win",
    )(*args)


def _topk_rows(s, k, ridx):
    big = float(s.shape[0])
    vals, idxs = [], []
    for _ in range(k):
        m = jnp.max(s, axis=0, keepdims=True)
        first = jnp.min(jnp.where(s == m, ridx, big), axis=0, keepdims=True)
        vals.append(m)
        idxs.append(first)
        s = jnp.where(ridx == first, -jnp.inf, s)
    return jnp.concatenate(vals, axis=0), jnp.concatenate(idxs, axis=0)


def _peer_candidates(t1, i1, t2, i2):
    vals = [t1[0:1] + t2]
    eids = [i1[0:1] * float(N_KEYS) + i2]
    for a in range(1, 8):
        vals.append(t1[a:a + 1] + t2[0:8])
        eids.append(i1[a:a + 1] * float(N_KEYS) + i2[0:8])
    vals.append(t1[8:16] + t2[0:1])
    eids.append(i1[8:16] * float(N_KEYS) + i2[0:1])
    return jnp.concatenate(vals, axis=0), jnp.concatenate(eids, axis=0)


def _route_kernel(x_ref, oa_ref, oc_ref, os_ref, ow_ref, gt_ref, gexp_ref, wo_ref, gf_ref, wq_ref,
                  k1_ref, k2_ref, x1_ref, eidx_ref, gate_ref):
    tm = x_ref.shape[0]
    gt = gt_ref[...]
    g_hi = gt.astype(BF16)
    g_lo = (gt - g_hi.astype(F32)).astype(BF16)
    ob = jnp.zeros(oc_ref.shape, F32)
    for j, br in enumerate((oc_ref, os_ref, ow_ref)):
        ex = gexp_ref[j]
        gj = (jnp.dot(g_hi, ex, preferred_element_type=F32) + jnp.dot(g_lo, ex, preferred_element_type=F32))
        ob = ob + gj * br[...]
    half = oa_ref.shape[1]
    mixed = (jnp.dot(oa_ref[...].astype(BF16), wo_ref[0:half, :], preferred_element_type=F32)
             + jnp.dot(ob.astype(BF16), wo_ref[half:, :], preferred_element_type=F32))
    x1 = x_ref[...] + mixed
    x1_ref[...] = x1
    h2 = x1 * lax.rsqrt(jnp.mean(x1 * x1, axis=-1, keepdims=True) + RMS_EPS) * gf_ref[...]
    q = jnp.dot(h2.astype(BF16), wq_ref[...], preferred_element_type=F32)
    hq = PEER_QDIM // 2
    ridx = lax.broadcasted_iota(I32, (N_KEYS, tm), 0).astype(F32)
    e_rows, g_rows = [], []
    for h in range(PEER_HEADS):
        q1 = q[:, h * PEER_QDIM: h * PEER_QDIM + hq].astype(BF16)
        q2 = q[:, h * PEER_QDIM + hq:(h + 1) * PEER_QDIM].astype(BF16)
        s1 = lax.dot_general(k1_ref[...], q1, (((1,), (1,)), ((), ())), preferred_element_type=F32)
        s2 = lax.dot_general(k2_ref[...], q2, (((1,), (1,)), ((), ())), preferred_element_type=F32)
        t1, i1 = _topk_rows(s1, PEER_TOPK, ridx)
        t2, i2 = _topk_rows(s2, PEER_TOPK, ridx)
        cand, cand_e = _peer_candidates(t1, i1, t2, i2)
        cidx = lax.broadcasted_iota(I32, cand.shape, 0).astype(F32)
        big = float(cand.shape[0])
        sc, ee = [], []
        for _ in range(PEER_TOPK):
            m = jnp.max(cand, axis=0, keepdims=True)
            first = jnp.min(jnp.where(cand == m, cidx, big), axis=0, keepdims=True)
            hit = cidx == first
            sc.append(m)
            ee.append(jnp.max(jnp.where(hit, cand_e, -1.0), axis=0, keepdims=True))
            cand = jnp.where(hit, -jnp.inf, cand)
        sc = jnp.concatenate(sc, axis=0)
        ex = jnp.exp(sc - sc[0:1])
        g_rows.append(ex / jnp.sum(ex, axis=0, keepdims=True))
        e_rows.append(jnp.concatenate(ee, axis=0))
    e_t = jnp.concatenate(e_rows, axis=0)
    g_t = jnp.concatenate(g_rows, axis=0)
    eidx_ref[...] = jnp.transpose(e_t).astype(I32)
    gate_ref[...] = jnp.transpose(g_t)


def _route(x2, oa, oc, osel, ow, gates, gexp, wo, gf, wq, k1, k2, tm):
    n, d = x2.shape
    row = lambda w: pl.BlockSpec((tm, w), lambda i: (i, 0))
    full = lambda a: pl.BlockSpec(a.shape, lambda i: (0,) * a.ndim)
    nk = PEER_HEADS * PEER_TOPK
    return pl.pallas_call(
        _route_kernel,
        grid=(n // tm,),
        in_specs=[row(d), row(oa.shape[1]), row(oc.shape[1]), row(osel.shape[1]), row(ow.shape[1]), row(LANES),
                  full(gexp), full(wo), full(gf), full(wq), full(k1), full(k2)],
        out_specs=[row(d), row(nk), row(nk)],
        out_shape=[jax.ShapeDtypeStruct((n, d), F32), jax.ShapeDtypeStruct((n, nk), I32),
                   jax.ShapeDtypeStruct((n, nk), F32)],
        compiler_params=_params("parallel"),
        name="route",
    )(x2, oa, oc, osel, ow, gates, gexp, wo, gf, wq, k1, k2)


def _fold8(p):
    sub = lax.broadcasted_iota(I32, (SUBLANES, LANES), 0)
    lo4 = sub < 4
    q = [jnp.where(lo4, p[j], p[j + 4]) + pltpu.roll(jnp.where(lo4, p[j + 4], p[j]), 4, 0) for j in range(4)]
    m2 = (sub & 3) < 2
    r = [jnp.where(m2, q[j] + pltpu.roll(q[j], 6, 0), q[j + 2] + pltpu.roll(q[j + 2], 2, 0)) for j in range(2)]
    m1 = (sub & 1) == 0
    return jnp.where(m1, r[0] + pltpu.roll(r[0], 7, 0), r[1] + pltpu.roll(r[1], 1, 0))


def _peer_kernel(ecur_ref, enxt_ref, x_ref, gate_ref, gffn_ref, gfin_ref, tab_ref, o_ref,
                 buf, sem, a_sc, cb_sc, *, tb):
    step = pl.program_id(0)
    nsteps = pl.num_programs(0)
    slot = step % 2
    nk = PEER_HEADS * PEER_TOPK
    npair = tb * nk

    def issue_rows(eref, dst_slot, t, k0, k1):
        erow = eref.at[t]
        for k in range(k0, k1):
            copy = pltpu.make_async_copy(tab_ref.at[erow[k]], buf.at[dst_slot, t * nk + k], sem.at[dst_slot])
            copy.start(priority=k % 2)

    def wait_slot(s):
        pltpu.make_async_copy(tab_ref.at[pl.ds(0, npair)], buf.at[s], sem.at[s]).wait()

    @pl.when(step == 0)
    def _():
        def first(t, carry):
            issue_rows(ecur_ref, 0, t, 0, nk)
            return carry
        lax.fori_loop(0, tb, first, 0)

    eye = jnp.where(lax.broadcasted_iota(I32, (nk, nk), 0) == lax.broadcasted_iota(I32, (nk, nk), 1),
                    1.0, 0.0).astype(BF16)
    gr = gate_ref[...]
    g0 = gr.astype(BF16)
    r1 = gr - g0.astype(F32)
    g1 = r1.astype(BF16)
    g2 = (r1 - g1.astype(F32)).astype(BF16)
    dn = (((1,), (1,)), ((), ()))
    gate_t = (lax.dot_general(eye, g0, dn, preferred_element_type=F32)
              + lax.dot_general(eye, g1, dn, preferred_element_type=F32)
              + lax.dot_general(eye, g2, dn, preferred_element_type=F32))

    wait_slot(slot)

    lane_tb = lax.broadcasted_iota(I32, (nk, tb), 1)
    inv_d = 1.0 / (SUBLANES * LANES)
    a_sc[...] = jnp.zeros(a_sc.shape, F32)

    ngroup = nk // SUBLANES
    per_group = nk // (2 * ngroup)

    def dots(t, carry):
        x8 = x_ref[t]
        ms = jnp.sum(jnp.sum(x8 * x8, axis=1, keepdims=True), axis=0, keepdims=True) * inv_d
        h8 = x8 * lax.rsqrt(ms + RMS_EPS) * gffn_ref[...]
        base = t * nk
        folded = []
        for gi in range(nk // SUBLANES):
            blk = buf[slot, pl.ds(base + gi * SUBLANES, SUBLANES), 0:SUBLANES, :]
            folded.append(_fold8([blk[j] * h8 for j in range(SUBLANES)]))
            issue_rows(enxt_ref, 1 - slot, t, gi * per_group, (gi + 1) * per_group)
        a = jnp.sum(jnp.concatenate(folded, axis=0), axis=1, keepdims=True)
        a_sc[...] = jnp.where(lane_tb == t, a, a_sc[...])
        return carry

    lax.fori_loop(0, tb, dots, 0, unroll=2)

    c_all = gate_t * _gelu(a_sc[...])
    for t in range(tb):
        cb_sc[t] = jnp.broadcast_to(c_all[:, t:t + 1], (nk, LANES))

    def combine(t, carry):
        base = t * nk
        accs = [jnp.zeros((SUBLANES, LANES), F32) for _ in range(4)]
        for gi in range(ngroup):
            for k in range(gi * SUBLANES, (gi + 1) * SUBLANES):
                ck = jnp.broadcast_to(cb_sc[t, k:k + 1, :], (SUBLANES, LANES))
                accs[k % 4] = accs[k % 4] + ck * buf[slot, base + k, SUBLANES:2 * SUBLANES, :]
            issue_rows(enxt_ref, 1 - slot, t, nk // 2 + gi * per_group, nk // 2 + (gi + 1) * per_group)
        y8 = x_ref[t] + ((accs[0] + accs[1]) + (accs[2] + accs[3]))
        ms2 = jnp.sum(jnp.sum(y8 * y8, axis=1, keepdims=True), axis=0, keepdims=True) * inv_d
        o_ref[t] = y8 * lax.rsqrt(ms2 + RMS_EPS) * gfin_ref[...]
        return carry

    lax.fori_loop(0, tb, combine, 0, unroll=4)

    @pl.when(step == nsteps - 1)
    def _():
        wait_slot(1 - slot)


def _peer(eidx, x1s, gate, gffn8, gfin8, table, tb):
    n = x1s.shape[0]
    nk = PEER_HEADS * PEER_TOPK
    nsteps = n // tb
    kern = functools.partial(_peer_kernel, tb=tb)
    return pl.pallas_call(
        kern,
        grid=(nsteps,),
        in_specs=[pl.BlockSpec((tb, nk), lambda i: (i, 0), memory_space=pltpu.SMEM),
                  pl.BlockSpec((tb, nk), lambda i: ((i + 1) % nsteps, 0), memory_space=pltpu.SMEM),
                  pl.BlockSpec((tb, SUBLANES, LANES), lambda i: (i, 0, 0)),
                  pl.BlockSpec((tb, nk), lambda i: (i, 0)),
                  pl.BlockSpec((SUBLANES, LANES), lambda i: (0, 0)),
                  pl.BlockSpec((SUBLANES, LANES), lambda i: (0, 0)),
                  pl.BlockSpec(memory_space=pl.ANY)],
        out_specs=pl.BlockSpec((tb, SUBLANES, LANES), lambda i: (i, 0, 0)),
        out_shape=jax.ShapeDtypeStruct((n, SUBLANES, LANES), F32),
        scratch_shapes=[pltpu.VMEM((2, tb * nk, 2 * SUBLANES, LANES), F32),
                        pltpu.SemaphoreType.DMA((2,)),
                        pltpu.VMEM((nk, tb), F32),
                        pltpu.VMEM((tb, nk, LANES), F32)],
        compiler_params=_params("arbitrary"),
        name="peer",
    )(eidx, eidx, x1s, gate, gffn8, gfin8, table)


def _overlap_t(t):
    nc = (t - CMP_LEN) // CMP_STRIDE + 1
    ns = t // SEL_BLOCK
    ncp = t // CMP_STRIDE
    cstart = np.arange(nc) * CMP_STRIDE
    bstart = np.arange(ns) * SEL_BLOCK
    lo = np.maximum(cstart[:, None], bstart[None, :])
    hi = np.minimum(cstart[:, None] + CMP_LEN, bstart[None, :] + SEL_BLOCK)
    ov = np.clip(hi - lo, 0, None).astype(np.float32) / CMP_LEN
    out = np.zeros((ns, ncp), np.float32)
    out[:, :nc] = ov.T
    return out, nc


def _gate_expand():
    ex = np.zeros((NSA_N_GATES, LANES, NSA_HEADS * HEAD_DIM), np.float32)
    for j in range(NSA_N_GATES):
        for h in range(NSA_HEADS):
            ex[j, h * NSA_N_GATES + j, h * HEAD_DIM:(h + 1) * HEAD_DIM] = 1.0
    return ex


def _layer(x, positions, norm_attn, w_in, attn_sinks, cmp_k_pos, cmp_k_w1, cmp_k_w2, cmp_v_pos, cmp_v_w1,
           cmp_v_w2, w_out, norm_ffn, w_query, sub_keys_1, sub_keys_2, expert_down, expert_up, norm_out,
           tm_in=512, tq_cmp=256, tq_sel=256, tq_band=128, tm_route=256, tb_peer=8):
    b, t, d = x.shape
    n = b * t
    x2 = x.reshape(n, d)
    half = HEAD_DIM // 2
    inv_freq = ROPE_THETA ** (-jnp.arange(half, dtype=F32) / half)
    invf = jnp.tile(inv_freq, LANES // half).reshape(1, LANES)
    w_pad = jnp.pad(w_in, ((0, 0), (0, IN_PAD - IN_WIDTH))).astype(BF16)
    pr = _inproj(x2, positions.reshape(n, 1), norm_attn.reshape(1, d), w_pad, invf, min(tm_in, n))
    r3 = lambda a: a.reshape(b, t, a.shape[-1])

    o_a = _band(r3(pr["qa"]), r3(pr["ka"]), r3(pr["va"]), attn_sinks.astype(F32), SWA_WINDOW, tq_band,
                SWA_HEADS, SWA_KV_HEADS)
    kcmp, vcmp = _compress(r3(pr["kc"]), r3(pr["vc"]),
                           cmp_k_pos.reshape(1, -1), cmp_k_w1.astype(BF16), cmp_k_w2.astype(BF16),
                           cmp_v_pos.reshape(1, -1), cmp_v_w1.astype(BF16), cmp_v_w2.astype(BF16))
    ovt, n_cmp = _overlap_t(t)
    qb3 = r3(pr["qb"])
    o_cmp, sel = _cmpsel(qb3, kcmp, vcmp, jnp.asarray(ovt), tq_cmp, n_cmp)
    vst4 = jnp.swapaxes(r3(pr["vs"]).reshape(b, t // tq_sel, tq_sel, LANES), 2, 3)
    o_sel = _selattn(qb3, r3(pr["ks"]), vst4, sel, tq_sel, tq_sel)
    o_win = _band(qb3, r3(pr["kw"]), r3(pr["vw"]), None, NSA_WINDOW, tq_band, NSA_HEADS, NSA_KV_HEADS)

    x1, eidx, gate = _route(x2, o_a.reshape(n, -1), o_cmp.reshape(n, -1), o_sel.reshape(n, -1),
                            o_win.reshape(n, -1), pr["gates"], jnp.asarray(_gate_expand(), BF16),
                            w_out.astype(BF16), norm_ffn.reshape(1, d), w_query.astype(BF16),
                            sub_keys_1.astype(BF16), sub_keys_2.astype(BF16), min(tm_route, n))
    ne = expert_down.shape[0]
    table = jnp.concatenate([expert_down.reshape(ne, SUBLANES, LANES), expert_up.reshape(ne, SUBLANES, LANES)],
                            axis=1)
    y = _peer(eidx, x1.reshape(n, SUBLANES, LANES), gate, norm_ffn.reshape(SUBLANES, LANES),
              norm_out.reshape(SUBLANES, LANES), table, tb_peer)
    return y.reshape(b, t, d)


def kernel(x, positions, norm_attn, w_in, attn_sinks, cmp_k_pos, cmp_k_w1, cmp_k_w2, cmp_v_pos, cmp_v_w1,
           cmp_v_w2, w_out, norm_ffn, peer_w_query, peer_sub_keys_1, peer_sub_keys_2, peer_expert_down,
           peer_expert_up, norm_f):
    assert norm_attn.shape[0] == 1, "single-layer block"
    return _layer(x, positions, norm_attn[0], w_in[0], attn_sinks[0], cmp_k_pos[0], cmp_k_w1[0], cmp_k_w2[0],
                  cmp_v_pos[0], cmp_v_w1[0], cmp_v_w2[0], w_out[0], norm_ffn[0], peer_w_query[0],
                  peer_sub_keys_1[0], peer_sub_keys_2[0], peer_expert_down[0], peer_expert_up[0], norm_f)
```

```python
import functools

import numpy as np
import jax
import jax.numpy as jnp
from jax import lax
from jax.experimental import pallas as pl
from jax.experimental.pallas import tpu as pltpu

F32 = jnp.float32
BF16 = jnp.bfloat16
I32 = jnp.int32

HEAD_DIM = 64
ROPE_THETA = 10000.0
RMS_EPS = 1e-6
NEG_BIG = -1e30

SWA_HEADS = 8
SWA_KV_HEADS = 2
SWA_WINDOW = 128

NSA_HEADS = 8
NSA_KV_HEADS = 2
CMP_LEN = 32
CMP_STRIDE = 16
CMP_HIDDEN = 256
SEL_BLOCK = 64
SEL_TOPN = 16
NSA_WINDOW = 512
NSA_N_GATES = 3

PEER_HEADS = 8
N_KEYS = 128
PEER_QDIM = 256
PEER_TOPK = 16

LANES = 128
SUBLANES = 8
VMEM_LIMIT = 56 * 1024 * 1024

_IN_SPLITS = (("qa", 512), ("ka", 128), ("va", 128), ("qb", 512), ("kc", 128), ("vc", 128),
              ("ks", 128), ("vs", 128), ("kw", 128), ("vw", 128), ("gl", 24))
IN_WIDTH = sum(w for _, w in _IN_SPLITS)
IN_PAD = 2176
_IN_OFF = {}
_o = 0
for _n, _w in _IN_SPLITS:
    _IN_OFF[_n] = _o
    _o += _w
_ROPED = ("qa", "ka", "qb", "kc", "ks", "kw")


def _gelu(x):
    return 0.5 * x * (1.0 + lax.erf(x * 0.7071067811865476))


def _params(*sem):
    return pltpu.CompilerParams(dimension_semantics=sem, vmem_limit_bytes=VMEM_LIMIT)


def _inproj_kernel(x_ref, pos_ref, g_ref, w_ref, invf_ref,
                   qa_ref, ka_ref, va_ref, qb_ref, kc_ref, vc_ref, ks_ref, vs_ref, kw_ref, vw_ref, gt_ref):
    x = x_ref[...]
    h = x * lax.rsqrt(jnp.mean(x * x, axis=-1, keepdims=True) + RMS_EPS) * g_ref[...]
    proj = jnp.dot(h.astype(BF16), w_ref[...], preferred_element_type=F32)
    ang = pos_ref[...].astype(F32) * invf_ref[...]
    lane = lax.broadcasted_iota(I32, ang.shape, 1)
    lo = (lane & (HEAD_DIM - 1)) < (HEAD_DIM // 2)
    cos = jnp.cos(ang)
    sin = jnp.sin(ang)
    sin_s = jnp.where(lo, -sin, sin)
    outs = {"qa": qa_ref, "ka": ka_ref, "va": va_ref, "qb": qb_ref, "kc": kc_ref, "vc": vc_ref,
            "ks": ks_ref, "vs": vs_ref, "kw": kw_ref, "vw": vw_ref}
    for name, width in _IN_SPLITS[:-1]:
        off = _IN_OFF[name]
        ref = outs[name]
        for i in range(width // LANES):
            v = proj[:, off + LANES * i: off + LANES * (i + 1)]
            if name in _ROPED:
                rot = jnp.where(lo, pltpu.roll(v, LANES - HEAD_DIM // 2, 1), pltpu.roll(v, HEAD_DIM // 2, 1))
                v = v * cos + rot * sin_s
            ref[:, LANES * i: LANES * (i + 1)] = v.astype(ref.dtype)
    gl = proj[:, _IN_OFF["gl"]: _IN_OFF["gl"] + LANES]
    gt_ref[...] = jax.nn.sigmoid(gl)


def _inproj(x2, pos2, g, w_pad, invf, tm):
    n, d = x2.shape
    row = lambda w: pl.BlockSpec((tm, w), lambda i: (i, 0))
    full = lambda a: pl.BlockSpec(a.shape, lambda i: (0,) * a.ndim)
    out_dt = {"qa": BF16, "ka": BF16, "va": BF16, "qb": BF16, "kc": F32, "vc": F32,
              "ks": BF16, "vs": BF16, "kw": BF16, "vw": BF16}
    names = [nm for nm, _ in _IN_SPLITS[:-1]]
    widths = dict(_IN_SPLITS)
    out_shape = [jax.ShapeDtypeStruct((n, widths[nm]), out_dt[nm]) for nm in names]
    out_shape.append(jax.ShapeDtypeStruct((n, LANES), F32))
    out_specs = [row(widths[nm]) for nm in names] + [row(LANES)]
    res = pl.pallas_call(
        _inproj_kernel,
        grid=(n // tm,),
        in_specs=[row(d), row(1), full(g), full(w_pad), full(invf)],
        out_specs=out_specs,
        out_shape=out_shape,
        compiler_params=_params("parallel"),
        name="inproj",
    )(x2, pos2, g, w_pad, invf)
    out = dict(zip(names, res[:-1]))
    out["gates"] = res[-1]
    return out


def _compress_kernel(kc_ref, vc_ref, kpos_ref, kw1_ref, kw2_ref, vpos_ref, vw1_ref, vw2_ref, ko_ref, vo_ref):
    ncp = ko_ref.shape[0]
    for src, pos_ref, w1_ref, w2_ref, out_ref in ((kc_ref, kpos_ref, kw1_ref, kw2_ref, ko_ref),
                                                  (vc_ref, vpos_ref, vw1_ref, vw2_ref, vo_ref)):
        bias = jnp.dot(pos_ref[...].astype(BF16), w1_ref[...], preferred_element_type=F32)
        acc_a = [jnp.zeros((ncp, CMP_HIDDEN), F32) for _ in range(2)]
        acc_b = [jnp.zeros((ncp, CMP_HIDDEN), F32) for _ in range(2)]
        for j in range(CMP_STRIDE):
            pj = src[pl.ds(j, ncp, stride=CMP_STRIDE), :].astype(BF16)
            wa = w1_ref[j * HEAD_DIM:(j + 1) * HEAD_DIM, :]
            wb = w1_ref[(CMP_STRIDE + j) * HEAD_DIM:(CMP_STRIDE + j + 1) * HEAD_DIM, :]
            for hh in range(2):
                ph = pj[:, hh * HEAD_DIM:(hh + 1) * HEAD_DIM]
                acc_a[hh] = acc_a[hh] + jnp.dot(ph, wa, preferred_element_type=F32)
                acc_b[hh] = acc_b[hh] + jnp.dot(ph, wb, preferred_element_type=F32)
        for hh in range(2):
            h1 = acc_a[hh] + pltpu.roll(acc_b[hh], ncp - 1, 0) + bias
            out = jnp.dot(_gelu(h1).astype(BF16), w2_ref[...], preferred_element_type=F32)
            out_ref[:, hh * HEAD_DIM:(hh + 1) * HEAD_DIM] = out


def _compress(kc3, vc3, kpos, kw1, kw2, vpos, vw1, vw2):
    b, t, w = kc3.shape
    ncp = t // CMP_STRIDE
    per_b = pl.BlockSpec((None, t, w), lambda i: (i, 0, 0))
    full = lambda a: pl.BlockSpec(a.shape, lambda i: (0,) * a.ndim)
    out_spec = pl.BlockSpec((None, ncp, w), lambda i: (i, 0, 0))
    return pl.pallas_call(
        _compress_kernel,
        grid=(b,),
        in_specs=[per_b, per_b, full(kpos), full(kw1), full(kw2), full(vpos), full(vw1), full(vw2)],
        out_specs=[out_spec, out_spec],
        out_shape=[jax.ShapeDtypeStruct((b, ncp, w), F32)] * 2,
        compiler_params=_params("parallel"),
        name="compress",
    )(kc3, vc3, kpos, kw1, kw2, vpos, vw1, vw2)


def _cmpsel_kernel(q_ref, kc_ref, vc_ref, ovt_ref, o_ref, sel_ref, *, tq, n_cmp):
    i = pl.program_id(1)
    t0 = i * tq
    g = NSA_HEADS // NSA_KV_HEADS
    ncp = kc_ref.shape[0]
    ns = ovt_ref.shape[0]
    rows = g * tq
    tpos = t0 + (lax.broadcasted_iota(I32, (rows, ncp), 0) & (tq - 1))
    cidx = lax.broadcasted_iota(I32, (rows, ncp), 1)
    vis = (cidx * CMP_STRIDE + (CMP_LEN - 1) <= tpos) & (cidx < n_cmp)
    jidx = lax.broadcasted_iota(I32, (ns, tq), 0)
    tcol = t0 + lax.broadcasted_iota(I32, (ns, tq), 1)
    cur = tcol // SEL_BLOCK
    forced = (jidx == 0) | (jidx == cur) | (jidx == cur - 1)
    valid = jidx * SEL_BLOCK <= tcol
    for hk in range(NSA_KV_HEADS):
        kk = kc_ref[:, hk * HEAD_DIM:(hk + 1) * HEAD_DIM].astype(BF16)
        vv = vc_ref[:, hk * HEAD_DIM:(hk + 1) * HEAD_DIM].astype(BF16)
        q4 = jnp.concatenate(
            [q_ref[:, (hk * g + gi) * HEAD_DIM:(hk * g + gi + 1) * HEAD_DIM] for gi in range(g)], axis=0)
        s = lax.dot_general(q4, kk, (((1,), (1,)), ((), ())), preferred_element_type=F32) * (HEAD_DIM ** -0.5)
        s = jnp.where(vis, s, NEG_BIG)
        m = jnp.max(s, axis=-1, keepdims=True)
        e = jnp.exp(s - m)
        p = jnp.where(vis, e / jnp.sum(e, axis=-1, keepdims=True), 0.0)
        o = jnp.dot(p.astype(BF16), vv, preferred_element_type=F32)
        for gi in range(g):
            h = hk * g + gi
            o_ref[:, h * HEAD_DIM:(h + 1) * HEAD_DIM] = o[gi * tq:(gi + 1) * tq]
        psum = p[0:tq]
        for gi in range(1, g):
            psum = psum + p[gi * tq:(gi + 1) * tq]
        imp_t = lax.dot_general(ovt_ref[...], psum, (((1,), (1,)), ((), ())),
                                precision=lax.Precision.HIGHEST, preferred_element_type=F32)
        score = jnp.where(forced, jnp.inf, jnp.where(valid, imp_t, -jnp.inf))
        cnt = jnp.zeros((ns, tq), F32)
        for r in range(ns):
            row = score[r:r + 1, :]
            beats = (row > score) | ((row >= score) & (jidx > r))
            cnt = cnt + jnp.where(beats, 1.0, 0.0)
        sel_ref[hk] = jnp.where((cnt < SEL_TOPN) & (score > -jnp.inf), 1.0, 0.0)


def _cmpsel(qb3, kcmp, vcmp, ovt, tq, n_cmp):
    b, t, w = qb3.shape
    ncp = kcmp.shape[1]
    ns = ovt.shape[0]
    kern = functools.partial(_cmpsel_kernel, tq=tq, n_cmp=n_cmp)
    return pl.pallas_call(
        kern,
        grid=(b, t // tq),
        in_specs=[pl.BlockSpec((None, tq, w), lambda bi, i: (bi, i, 0)),
                  pl.BlockSpec((None, ncp, LANES), lambda bi, i: (bi, 0, 0)),
                  pl.BlockSpec((None, ncp, LANES), lambda bi, i: (bi, 0, 0)),
                  pl.BlockSpec(ovt.shape, lambda bi, i: (0, 0))],
        out_specs=[pl.BlockSpec((None, tq, w), lambda bi, i: (bi, i, 0)),
                   pl.BlockSpec((None, NSA_KV_HEADS, ns, tq), lambda bi, i: (bi, 0, 0, i))],
        out_shape=[jax.ShapeDtypeStruct((b, t, w), F32), jax.ShapeDtypeStruct((b, NSA_KV_HEADS, ns, t), F32)],
        compiler_params=_params("parallel", "parallel"),
        name="cmpsel",
    )(qb3, kcmp, vcmp, ovt)


def _selattn_kernel(q_ref, k_ref, vt_ref, sel_ref, o_ref, m_sc, l_sc, acc_sc, *, tq, tk):
    i = pl.program_id(1)
    t0 = i * tq
    g = NSA_HEADS // NSA_KV_HEADS
    rows = g * tq
    assert tk % tq == 0, "one key tile must cover the whole diagonal of a query tile"
    n_below = (i * tq) // tk
    nb = tk // SEL_BLOCK
    qpos = t0 + (lax.broadcasted_iota(I32, (tk, rows), 1) & (tq - 1))
    krow = lax.broadcasted_iota(I32, (tk, rows), 0)
    for hk in range(NSA_KV_HEADS):
        qt = jnp.concatenate(
            [jnp.transpose(q_ref[:, (hk * g + gi) * HEAD_DIM:(hk * g + gi + 1) * HEAD_DIM].astype(F32))
             for gi in range(g)], axis=1)
        qt = (qt * (HEAD_DIM ** -0.5)).astype(BF16)
        m_sc[...] = jnp.full(m_sc.shape, NEG_BIG, F32)
        l_sc[...] = jnp.zeros(l_sc.shape, F32)
        acc_sc[...] = jnp.zeros(acc_sc.shape, F32)

        def tile(kt, diagonal):
            ks = pl.multiple_of(kt * tk, tk)
            kk = k_ref[pl.ds(ks, tk), hk * HEAD_DIM:(hk + 1) * HEAD_DIM]
            s = jnp.dot(kk, qt, preferred_element_type=F32)
            bias = jnp.concatenate(
                [jnp.broadcast_to(
                    jnp.concatenate([(1.0 - sel_ref[hk, pl.ds(kt * nb + jj, 1), :]) * NEG_BIG] * g, axis=1),
                    (SEL_BLOCK, rows)) for jj in range(nb)], axis=0)
            s = s + bias
            if diagonal:
                s = jnp.where(ks + krow <= qpos, s, NEG_BIG)
            m_old = m_sc[...]
            m_new = jnp.maximum(m_old, jnp.max(s, axis=0, keepdims=True))
            alpha = jnp.exp(m_old - m_new)
            p = jnp.exp(s - m_new)
            l_sc[...] = alpha * l_sc[...] + jnp.sum(p, axis=0, keepdims=True)
            vt = vt_ref[kt, hk * HEAD_DIM:(hk + 1) * HEAD_DIM, :]
            acc_sc[...] = alpha * acc_sc[...] + jnp.dot(vt, p.astype(BF16), preferred_element_type=F32)
            m_sc[...] = m_new

        def below_diagonal(kt, carry):
            tile(kt, False)
            return carry

        lax.fori_loop(0, n_below, below_diagonal, 0)
        tile(n_below, True)
        o_t = acc_sc[...] / l_sc[...]
        for gi in range(g):
            h = hk * g + gi
            o_ref[:, h * HEAD_DIM:(h + 1) * HEAD_DIM] = jnp.transpose(o_t[:, gi * tq:(gi + 1) * tq])


def _selattn(qb3, ks3, vst4, sel4, tq, tk):
    b, t, w = qb3.shape
    g = NSA_HEADS // NSA_KV_HEADS
    ns = sel4.shape[2]
    kern = functools.partial(_selattn_kernel, tq=tq, tk=tk)
    return pl.pallas_call(
        kern,
        grid=(b, t // tq),
        in_specs=[pl.BlockSpec((None, tq, w), lambda bi, i: (bi, i, 0)),
                  pl.BlockSpec((None, t, LANES), lambda bi, i: (bi, 0, 0)),
                  pl.BlockSpec((None, t // tk, LANES, tk), lambda bi, i: (bi, 0, 0, 0)),
                  pl.BlockSpec((None, NSA_KV_HEADS, ns, tq), lambda bi, i: (bi, 0, 0, i))],
        out_specs=pl.BlockSpec((None, tq, w), lambda bi, i: (bi, i, 0)),
        out_shape=jax.ShapeDtypeStruct((b, t, w), F32),
        scratch_shapes=[pltpu.VMEM((1, g * tq), F32), pltpu.VMEM((1, g * tq), F32),
                        pltpu.VMEM((HEAD_DIM, g * tq), F32)],
        compiler_params=_params("parallel", "arbitrary"),
        name="selattn",
    )(qb3, ks3, vst4, sel4)


def _band_kernel(*refs, window, tq, span, n_heads, n_kv, has_sink):
    if has_sink:
        sink_ref, q_ref, k_ref, v_ref, o_ref = refs
    else:
        q_ref, k_ref, v_ref, o_ref = refs
    i = pl.program_id(1)
    t0 = i * tq
    start = pl.multiple_of(jnp.maximum(t0 + tq - span, 0), tq)
    g = n_heads // n_kv
    rows = g * tq
    qpos = t0 + (lax.broadcasted_iota(I32, (rows, span), 0) & (tq - 1))
    kpos = start + lax.broadcasted_iota(I32, (rows, span), 1)
    diff = qpos - kpos
    mask = (diff >= 0) & (diff < window)
    for hk in range(n_kv):
        kk = k_ref[pl.ds(start, span), hk * HEAD_DIM:(hk + 1) * HEAD_DIM]
        vv = v_ref[pl.ds(start, span), hk * HEAD_DIM:(hk + 1) * HEAD_DIM]
        q4 = jnp.concatenate(
            [q_ref[:, (hk * g + gi) * HEAD_DIM:(hk * g + gi + 1) * HEAD_DIM] for gi in range(g)], axis=0)
        s = lax.dot_general(q4, kk, (((1,), (1,)), ((), ())), preferred_element_type=F32) * (HEAD_DIM ** -0.5)
        s = jnp.where(mask, s, NEG_BIG)
        m = jnp.max(s, axis=-1, keepdims=True)
        if has_sink:
            sink = jnp.concatenate([jnp.full((tq, 1), sink_ref[hk * g + gi], F32) for gi in range(g)], axis=0)
            m = jnp.maximum(m, sink)
        p = jnp.where(mask, jnp.exp(s - m), 0.0)
        l = jnp.sum(p, axis=-1, keepdims=True)
        if has_sink:
            l = l + jnp.exp(sink - m)
        o = jnp.dot(p.astype(BF16), vv, preferred_element_type=F32) / l
        for gi in range(g):
            h = hk * g + gi
            o_ref[:, h * HEAD_DIM:(h + 1) * HEAD_DIM] = o[gi * tq:(gi + 1) * tq]


def _band(q3, k3, v3, sinks, window, tq, n_heads, n_kv):
    b, t, w = q3.shape
    nprev = -(-(window - 1) // tq)
    span = min((nprev + 1) * tq, t)
    has_sink = sinks is not None
    kern = functools.partial(_band_kernel, window=window, tq=tq, span=span, n_heads=n_heads, n_kv=n_kv,
                             has_sink=has_sink)
    in_specs = [pl.BlockSpec((None, tq, w), lambda bi, i: (bi, i, 0)),
                pl.BlockSpec((None, t, LANES), lambda bi, i: (bi, 0, 0)),
                pl.BlockSpec((None, t, LANES), lambda bi, i: (bi, 0, 0))]
    args = [q3, k3, v3]
    if has_sink:
        in_specs = [pl.BlockSpec(memory_space=pltpu.SMEM)] + in_specs
        args = [sinks] + args
    return pl.pallas_call(
        kern,
        grid=(b, t // tq),
        in_specs=in_specs,
        out_specs=pl.BlockSpec((None, tq, w), lambda bi, i: (bi, i, 0)),
        out_shape=jax.ShapeDtypeStruct((b, t, w), F32),
        compiler_params=_params("parallel", "parallel"),
        name="band_sink" if has_sink else "band_win",
    )(*args)


def _topk_rows(s, k, ridx):
    big = float(s.shape[0])
    vals, idxs = [], []
    for _ in range(k):
        m = jnp.max(s, axis=0, keepdims=True)
        first = jnp.min(jnp.where(s == m, ridx, big), axis=0, keepdims=True)
        vals.append(m)
        idxs.append(first)
        s = jnp.where(ridx == first, -jnp.inf, s)
    return jnp.concatenate(vals, axis=0), jnp.concatenate(idxs, axis=0)


def _peer_candidates(t1, i1, t2, i2):
    vals = [t1[0:1] + t2]
    eids = [i1[0:1] * float(N_KEYS) + i2]
    for a in range(1, 8):
        vals.append(t1[a:a + 1] + t2[0:8])
        eids.append(i1[a:a + 1] * float(N_KEYS) + i2[0:8])
    vals.append(t1[8:16] + t2[0:1])
    eids.append(i1[8:16] * float(N_KEYS) + i2[0:1])
    return jnp.concatenate(vals, axis=0), jnp.concatenate(eids, axis=0)


def _route_kernel(x_ref, oa_ref, oc_ref, os_ref, ow_ref, gt_ref, gexp_ref, wo_ref, gf_ref, wq_ref,
                  k1_ref, k2_ref, x1_ref, eidx_ref, gate_ref):
    tm = x_ref.shape[0]
    gt = gt_ref[...]
    g_hi = gt.astype(BF16)
    g_lo = (gt - g_hi.astype(F32)).astype(BF16)
    ob = jnp.zeros(oc_ref.shape, F32)
    for j, br in enumerate((oc_ref, os_ref, ow_ref)):
        ex = gexp_ref[j]
        gj = (jnp.dot(g_hi, ex, preferred_element_type=F32) + jnp.dot(g_lo, ex, preferred_element_type=F32))
        ob = ob + gj * br[...]
    half = oa_ref.shape[1]
    mixed = (jnp.dot(oa_ref[...].astype(BF16), wo_ref[0:half, :], preferred_element_type=F32)
             + jnp.dot(ob.astype(BF16), wo_ref[half:, :], preferred_element_type=F32))
    x1 = x_ref[...] + mixed
    x1_ref[...] = x1
    h2 = x1 * lax.rsqrt(jnp.mean(x1 * x1, axis=-1, keepdims=True) + RMS_EPS) * gf_ref[...]
    q = jnp.dot(h2.astype(BF16), wq_ref[...], preferred_element_type=F32)
    hq = PEER_QDIM // 2
    ridx = lax.broadcasted_iota(I32, (N_KEYS, tm), 0).astype(F32)
    e_rows, g_rows = [], []
    for h in range(PEER_HEADS):
        q1 = q[:, h * PEER_QDIM: h * PEER_QDIM + hq].astype(BF16)
        q2 = q[:, h * PEER_QDIM + hq:(h + 1) * PEER_QDIM].astype(BF16)
        s1 = lax.dot_general(k1_ref[...], q1, (((1,), (1,)), ((), ())), preferred_element_type=F32)
        s2 = lax.dot_general(k2_ref[...], q2, (((1,), (1,)), ((), ())), preferred_element_type=F32)
        t1, i1 = _topk_rows(s1, PEER_TOPK, ridx)
        t2, i2 = _topk_rows(s2, PEER_TOPK, ridx)
        cand, cand_e = _peer_candidates(t1, i1, t2, i2)
        cidx = lax.broadcasted_iota(I32, cand.shape, 0).astype(F32)
        big = float(cand.shape[0])
        sc, ee = [], []
        for _ in range(PEER_TOPK):
            m = jnp.max(cand, axis=0, keepdims=True)
            first = jnp.min(jnp.where(cand == m, cidx, big), axis=0, keepdims=True)
            hit = cidx == first
            sc.append(m)
            ee.append(jnp.max(jnp.where(hit, cand_e, -1.0), axis=0, keepdims=True))
            cand = jnp.where(hit, -jnp.inf, cand)
        sc = jnp.concatenate(sc, axis=0)
        ex = jnp.exp(sc - sc[0:1])
        g_rows.append(ex / jnp.sum(ex, axis=0, keepdims=True))
        e_rows.append(jnp.concatenate(ee, axis=0))
    e_t = jnp.concatenate(e_rows, axis=0)
    g_t = jnp.concatenate(g_rows, axis=0)
    eidx_ref[...] = jnp.transpose(e_t).astype(I32)
    gate_ref[...] = jnp.transpose(g_t)


def _route(x2, oa, oc, osel, ow, gates, gexp, wo, gf, wq, k1, k2, tm):
    n, d = x2.shape
    row = lambda w: pl.BlockSpec((tm, w), lambda i: (i, 0))
    full = lambda a: pl.BlockSpec(a.shape, lambda i: (0,) * a.ndim)
    nk = PEER_HEADS * PEER_TOPK
    return pl.pallas_call(
        _route_kernel,
        grid=(n // tm,),
        in_specs=[row(d), row(oa.shape[1]), row(oc.shape[1]), row(osel.shape[1]), row(ow.shape[1]), row(LANES),
                  full(gexp), full(wo), full(gf), full(wq), full(k1), full(k2)],
        out_specs=[row(d), row(nk), row(nk)],
        out_shape=[jax.ShapeDtypeStruct((n, d), F32), jax.ShapeDtypeStruct((n, nk), I32),
                   jax.ShapeDtypeStruct((n, nk), F32)],
        compiler_params=_params("parallel"),
        name="route",
    )(x2, oa, oc, osel, ow, gates, gexp, wo, gf, wq, k1, k2)


def _fold8(p):
    sub = lax.broadcasted_iota(I32, (SUBLANES, LANES), 0)
    lo4 = sub < 4
    q = [jnp.where(lo4, p[j], p[j + 4]) + pltpu.roll(jnp.where(lo4, p[j + 4], p[j]), 4, 0) for j in range(4)]
    m2 = (sub & 3) < 2
    r = [jnp.where(m2, q[j] + pltpu.roll(q[j], 6, 0), q[j + 2] + pltpu.roll(q[j + 2], 2, 0)) for j in range(2)]
    m1 = (sub & 1) == 0
    return jnp.where(m1, r[0] + pltpu.roll(r[0], 7, 0), r[1] + pltpu.roll(r[1], 1, 0))


def _peer_kernel(ecur_ref, enxt_ref, x_ref, gate_ref, gffn_ref, gfin_ref, tab_ref, o_ref,
                 buf, sem, a_sc, cb_sc, *, tb):
    step = pl.program_id(0)
    nsteps = pl.num_programs(0)
    slot = step % 2
    nk = PEER_HEADS * PEER_TOPK
    npair = tb * nk

    def issue_rows(eref, dst_slot, t, k0, k1):
        erow = eref.at[t]
        for k in range(k0, k1):
            copy = pltpu.make_async_copy(tab_ref.at[erow[k]], buf.at[dst_slot, t * nk + k], sem.at[dst_slot])
            copy.start(priority=k % 2)

    def wait_slot(s):
        pltpu.make_async_copy(tab_ref.at[pl.ds(0, npair)], buf.at[s], sem.at[s]).wait()

    @pl.when(step == 0)
    def _():
        def first(t, carry):
            issue_rows(ecur_ref, 0, t, 0, nk)
            return carry
        lax.fori_loop(0, tb, first, 0)

    eye = jnp.where(lax.broadcasted_iota(I32, (nk, nk), 0) == lax.broadcasted_iota(I32, (nk, nk), 1),
                    1.0, 0.0).astype(BF16)
    gr = gate_ref[...]
    g0 = gr.astype(BF16)
    r1 = gr - g0.astype(F32)
    g1 = r1.astype(BF16)
    g2 = (r1 - g1.astype(F32)).astype(BF16)
    dn = (((1,), (1,)), ((), ()))
    gate_t = (lax.dot_general(eye, g0, dn, preferred_element_type=F32)
              + lax.dot_general(eye, g1, dn, preferred_element_type=F32)
              + lax.dot_general(eye, g2, dn, preferred_element_type=F32))

    wait_slot(slot)

    lane_tb = lax.broadcasted_iota(I32, (nk, tb), 1)
    inv_d = 1.0 / (SUBLANES * LANES)
    a_sc[...] = jnp.zeros(a_sc.shape, F32)

    ngroup = nk // SUBLANES
    per_group = nk // (2 * ngroup)

    def dots(t, carry):
        x8 = x_ref[t]
        ms = jnp.sum(jnp.sum(x8 * x8, axis=1, keepdims=True), axis=0, keepdims=True) * inv_d
        h8 = x8 * lax.rsqrt(ms + RMS_EPS) * gffn_ref[...]
        base = t * nk
        folded = []
        for gi in range(nk // SUBLANES):
            blk = buf[slot, pl.ds(base + gi * SUBLANES, SUBLANES), 0:SUBLANES, :]
            folded.append(_fold8([blk[j] * h8 for j in range(SUBLANES)]))
            issue_rows(enxt_ref, 1 - slot, t, gi * per_group, (gi + 1) * per_group)
        a = jnp.sum(jnp.concatenate(folded, axis=0), axis=1, keepdims=True)
        a_sc[...] = jnp.where(lane_tb == t, a, a_sc[...])
        return carry

    lax.fori_loop(0, tb, dots, 0, unroll=2)

    c_all = gate_t * _gelu(a_sc[...])
    for t in range(tb):
        cb_sc[t] = jnp.broadcast_to(c_all[:, t:t + 1], (nk, LANES))

    def combine(t, carry):
        base = t * nk
        accs = [jnp.zeros((SUBLANES, LANES), F32) for _ in range(4)]
        for gi in range(ngroup):
            for k in range(gi * SUBLANES, (gi + 1) * SUBLANES):
                ck = jnp.broadcast_to(cb_sc[t, k:k + 1, :], (SUBLANES, LANES))
                accs[k % 4] = accs[k % 4] + ck * buf[slot, base + k, SUBLANES:2 * SUBLANES, :]
            issue_rows(enxt_ref, 1 - slot, t, nk // 2 + gi * per_group, nk // 2 + (gi + 1) * per_group)
        y8 = x_ref[t] + ((accs[0] + accs[1]) + (accs[2] + accs[3]))
        ms2 = jnp.sum(jnp.sum(y8 * y8, axis=1, keepdims=True), axis=0, keepdims=True) * inv_d
        o_ref[t] = y8 * lax.rsqrt(ms2 + RMS_EPS) * gfin_ref[...]
        return carry

    lax.fori_loop(0, tb, combine, 0, unroll=4)

    @pl.when(step == nsteps - 1)
    def _():
        wait_slot(1 - slot)


def _peer(eidx, x1s, gate, gffn8, gfin8, table, tb):
    n = x1s.shape[0]
    nk = PEER_HEADS * PEER_TOPK
    nsteps = n // tb
    kern = functools.partial(_peer_kernel, tb=tb)
    return pl.pallas_call(
        kern,
        grid=(nsteps,),
        in_specs=[pl.BlockSpec((tb, nk), lambda i: (i, 0), memory_space=pltpu.SMEM),
                  pl.BlockSpec((tb, nk), lambda i: ((i + 1) % nsteps, 0), memory_space=pltpu.SMEM),
                  pl.BlockSpec((tb, SUBLANES, LANES), lambda i: (i, 0, 0)),
                  pl.BlockSpec((tb, nk), lambda i: (i, 0)),
                  pl.BlockSpec((SUBLANES, LANES), lambda i: (0, 0)),
                  pl.BlockSpec((SUBLANES, LANES), lambda i: (0, 0)),
                  pl.BlockSpec(memory_space=pl.ANY)],
        out_specs=pl.BlockSpec((tb, SUBLANES, LANES), lambda i: (i, 0, 0)),
        out_shape=jax.ShapeDtypeStruct((n, SUBLANES, LANES), F32),
        scratch_shapes=[pltpu.VMEM((2, tb * nk, 2 * SUBLANES, LANES), F32),
                        pltpu.SemaphoreType.DMA((2,)),
                        pltpu.VMEM((nk, tb), F32),
                        pltpu.VMEM((tb, nk, LANES), F32)],
        compiler_params=_params("arbitrary"),
        name="peer",
    )(eidx, eidx, x1s, gate, gffn8, gfin8, table)


def _overlap_t(t):
    nc = (t - CMP_LEN) // CMP_STRIDE + 1
    ns = t // SEL_BLOCK
    ncp = t // CMP_STRIDE
    cstart = np.arange(nc) * CMP_STRIDE
    bstart = np.arange(ns) * SEL_BLOCK
    lo = np.maximum(cstart[:, None], bstart[None, :])
    hi = np.minimum(cstart[:, None] + CMP_LEN, bstart[None, :] + SEL_BLOCK)
    ov = np.clip(hi - lo, 0, None).astype(np.float32) / CMP_LEN
    out = np.zeros((ns, ncp), np.float32)
    out[:, :nc] = ov.T
    return out, nc


def _gate_expand():
    ex = np.zeros((NSA_N_GATES, LANES, NSA_HEADS * HEAD_DIM), np.float32)
    for j in range(NSA_N_GATES):
        for h in range(NSA_HEADS):
            ex[j, h * NSA_N_GATES + j, h * HEAD_DIM:(h + 1) * HEAD_DIM] = 1.0
    return ex


def _layer(x, positions, norm_attn, w_in, attn_sinks, cmp_k_pos, cmp_k_w1, cmp_k_w2, cmp_v_pos, cmp_v_w1,
           cmp_v_w2, w_out, norm_ffn, w_query, sub_keys_1, sub_keys_2, expert_down, expert_up, norm_out,
           tm_in=512, tq_cmp=256, tq_sel=256, tk_sel=512, tq_band=128, tm_route=256, tb_peer=16):
    b, t, d = x.shape
    n = b * t
    x2 = x.reshape(n, d)
    half = HEAD_DIM // 2
    inv_freq = ROPE_THETA ** (-jnp.arange(half, dtype=F32) / half)
    invf = jnp.tile(inv_freq, LANES // half).reshape(1, LANES)
    w_pad = jnp.pad(w_in, ((0, 0), (0, IN_PAD - IN_WIDTH))).astype(BF16)
    pr = _inproj(x2, positions.reshape(n, 1), norm_attn.reshape(1, d), w_pad, invf, min(tm_in, n))
    r3 = lambda a: a.reshape(b, t, a.shape[-1])

    o_a = _band(r3(pr["qa"]), r3(pr["ka"]), r3(pr["va"]), attn_sinks.astype(F32), SWA_WINDOW, tq_band,
                SWA_HEADS, SWA_KV_HEADS)
    kcmp, vcmp = _compress(r3(pr["kc"]), r3(pr["vc"]),
                           cmp_k_pos.reshape(1, -1), cmp_k_w1.astype(BF16), cmp_k_w2.astype(BF16),
                           cmp_v_pos.reshape(1, -1), cmp_v_w1.astype(BF16), cmp_v_w2.astype(BF16))
    ovt, n_cmp = _overlap_t(t)
    qb3 = r3(pr["qb"])
    o_cmp, sel = _cmpsel(qb3, kcmp, vcmp, jnp.asarray(ovt), tq_cmp, n_cmp)
    tk_sel = min(tk_sel, t)
    vst4 = jnp.swapaxes(r3(pr["vs"]).reshape(b, t // tk_sel, tk_sel, LANES), 2, 3)
    o_sel = _selattn(qb3, r3(pr["ks"]), vst4, sel, tq_sel, tk_sel)
    o_win = _band(qb3, r3(pr["kw"]), r3(pr["vw"]), None, NSA_WINDOW, tq_band, NSA_HEADS, NSA_KV_HEADS)

    x1, eidx, gate = _route(x2, o_a.reshape(n, -1), o_cmp.reshape(n, -1), o_sel.reshape(n, -1),
                            o_win.reshape(n, -1), pr["gates"], jnp.asarray(_gate_expand(), BF16),
                            w_out.astype(BF16), norm_ffn.reshape(1, d), w_query.astype(BF16),
                            sub_keys_1.astype(BF16), sub_keys_2.astype(BF16), min(tm_route, n))
    ne = expert_down.shape[0]
    table = jnp.concatenate([expert_down.reshape(ne, SUBLANES, LANES), expert_up.reshape(ne, SUBLANES, LANES)],
                            axis=1)
    y = _peer(eidx, x1.reshape(n, SUBLANES, LANES), gate, norm_ffn.reshape(SUBLANES, LANES),
              norm_out.reshape(SUBLANES, LANES), table, tb_peer)
    return y.reshape(b, t, d)


def kernel(x, positions, norm_attn, w_in, attn_sinks, cmp_k_pos, cmp_k_w1, cmp_k_w2, cmp_v_pos, cmp_v_w1,
           cmp_v_w2, w_out, norm_ffn, peer_w_query, peer_sub_keys_1, peer_sub_keys_2, peer_expert_down,
           peer_expert_up, norm_f):
    assert norm_attn.shape[0] == 1, "single-layer block"
    return _layer(x, positions, norm_attn[0], w_in[0], attn_sinks[0], cmp_k_pos[0], cmp_k_w1[0], cmp_k_w2[0],
                  cmp_v_pos[0], cmp_v_w1[0], cmp_v_w2[0], w_out[0], norm_ffn[0], peer_w_query[0],
                  peer_sub_keys_1[0], peer_sub_keys_2[0], peer_expert_down[0], peer_expert_up[0], norm_f)
```

```python
import functools

import numpy as np
import jax
import jax.numpy as jnp
from jax import lax
from jax.experimental import pallas as pl
from jax.experimental.pallas import tpu as pltpu

F32 = jnp.float32
BF16 = jnp.bfloat16
I32 = jnp.int32

HEAD_DIM = 64
ROPE_THETA = 10000.0
RMS_EPS = 1e-6
NEG_BIG = -1e30

SWA_HEADS = 8
SWA_KV_HEADS = 2
SWA_WINDOW = 128

NSA_HEADS = 8
NSA_KV_HEADS = 2
CMP_LEN = 32
CMP_STRIDE = 16
CMP_HIDDEN = 256
SEL_BLOCK = 64
SEL_TOPN = 16
NSA_WINDOW = 512
NSA_N_GATES = 3

PEER_HEADS = 8
N_KEYS = 128
PEER_QDIM = 256
PEER_TOPK = 16
PEER_SLOTS = 3

LANES = 128
SUBLANES = 8
VMEM_LIMIT = 56 * 1024 * 1024

_IN_SPLITS = (("qa", 512), ("ka", 128), ("va", 128), ("qb", 512), ("kc", 128), ("vc", 128),
              ("ks", 128), ("vs", 128), ("kw", 128), ("vw", 128), ("gl", 24))
IN_WIDTH = sum(w for _, w in _IN_SPLITS)
IN_PAD = 2176
_IN_OFF = {}
_o = 0
for _n, _w in _IN_SPLITS:
    _IN_OFF[_n] = _o
    _o += _w
_ROPED = ("qa", "ka", "qb", "kc", "ks", "kw")


def _gelu(x):
    return 0.5 * x * (1.0 + lax.erf(x * 0.7071067811865476))


def _params(*sem):
    return pltpu.CompilerParams(dimension_semantics=sem, vmem_limit_bytes=VMEM_LIMIT)


def _inproj_kernel(x_ref, pos_ref, g_ref, w_ref, invf_ref,
                   qa_ref, ka_ref, va_ref, qb_ref, kc_ref, vc_ref, ks_ref, vs_ref, kw_ref, vw_ref, gt_ref):
    x = x_ref[...]
    h = x * lax.rsqrt(jnp.mean(x * x, axis=-1, keepdims=True) + RMS_EPS) * g_ref[...]
    proj = jnp.dot(h.astype(BF16), w_ref[...], preferred_element_type=F32)
    ang = pos_ref[...].astype(F32) * invf_ref[...]
    lane = lax.broadcasted_iota(I32, ang.shape, 1)
    lo = (lane & (HEAD_DIM - 1)) < (HEAD_DIM // 2)
    cos = jnp.cos(ang)
    sin = jnp.sin(ang)
    sin_s = jnp.where(lo, -sin, sin)
    outs = {"qa": qa_ref, "ka": ka_ref, "va": va_ref, "qb": qb_ref, "kc": kc_ref, "vc": vc_ref,
            "ks": ks_ref, "vs": vs_ref, "kw": kw_ref, "vw": vw_ref}
    for name, width in _IN_SPLITS[:-1]:
        off = _IN_OFF[name]
        ref = outs[name]
        for i in range(width // LANES):
            v = proj[:, off + LANES * i: off + LANES * (i + 1)]
            if name in _ROPED:
                rot = jnp.where(lo, pltpu.roll(v, LANES - HEAD_DIM // 2, 1), pltpu.roll(v, HEAD_DIM // 2, 1))
                v = v * cos + rot * sin_s
            ref[:, LANES * i: LANES * (i + 1)] = v.astype(ref.dtype)
    gl = proj[:, _IN_OFF["gl"]: _IN_OFF["gl"] + LANES]
    gt_ref[...] = jax.nn.sigmoid(gl)


def _inproj(x2, pos2, g, w_pad, invf, tm):
    n, d = x2.shape
    row = lambda w: pl.BlockSpec((tm, w), lambda i: (i, 0))
    full = lambda a: pl.BlockSpec(a.shape, lambda i: (0,) * a.ndim)
    out_dt = {"qa": BF16, "ka": BF16, "va": BF16, "qb": BF16, "kc": F32, "vc": F32,
              "ks": BF16, "vs": BF16, "kw": BF16, "vw": BF16}
    names = [nm for nm, _ in _IN_SPLITS[:-1]]
    widths = dict(_IN_SPLITS)
    out_shape = [jax.ShapeDtypeStruct((n, widths[nm]), out_dt[nm]) for nm in names]
    out_shape.append(jax.ShapeDtypeStruct((n, LANES), F32))
    out_specs = [row(widths[nm]) for nm in names] + [row(LANES)]
    res = pl.pallas_call(
        _inproj_kernel,
        grid=(n // tm,),
        in_specs=[row(d), row(1), full(g), full(w_pad), full(invf)],
        out_specs=out_specs,
        out_shape=out_shape,
        compiler_params=_params("parallel"),
        name="inproj",
    )(x2, pos2, g, w_pad, invf)
    out = dict(zip(names, res[:-1]))
    out["gates"] = res[-1]
    return out


def _compress_kernel(kc_ref, vc_ref, kpos_ref, kw1_ref, kw2_ref, vpos_ref, vw1_ref, vw2_ref, ko_ref, vo_ref):
    ncp = ko_ref.shape[0]
    for src, pos_ref, w1_ref, w2_ref, out_ref in ((kc_ref, kpos_ref, kw1_ref, kw2_ref, ko_ref),
                                                  (vc_ref, vpos_ref, vw1_ref, vw2_ref, vo_ref)):
        bias = jnp.dot(pos_ref[...].astype(BF16), w1_ref[...], preferred_element_type=F32)
        acc_a = [jnp.zeros((ncp, CMP_HIDDEN), F32) for _ in range(2)]
        acc_b = [jnp.zeros((ncp, CMP_HIDDEN), F32) for _ in range(2)]
        for j in range(CMP_STRIDE):
            pj = src[pl.ds(j, ncp, stride=CMP_STRIDE), :].astype(BF16)
            wa = w1_ref[j * HEAD_DIM:(j + 1) * HEAD_DIM, :]
            wb = w1_ref[(CMP_STRIDE + j) * HEAD_DIM:(CMP_STRIDE + j + 1) * HEAD_DIM, :]
            for hh in range(2):
                ph = pj[:, hh * HEAD_DIM:(hh + 1) * HEAD_DIM]
                acc_a[hh] = acc_a[hh] + jnp.dot(ph, wa, preferred_element_type=F32)
                acc_b[hh] = acc_b[hh] + jnp.dot(ph, wb, preferred_element_type=F32)
        for hh in range(2):
            h1 = acc_a[hh] + pltpu.roll(acc_b[hh], ncp - 1, 0) + bias
            out = jnp.dot(_gelu(h1).astype(BF16), w2_ref[...], preferred_element_type=F32)
            out_ref[:, hh * HEAD_DIM:(hh + 1) * HEAD_DIM] = out


def _compress(kc3, vc3, kpos, kw1, kw2, vpos, vw1, vw2):
    b, t, w = kc3.shape
    ncp = t // CMP_STRIDE
    per_b = pl.BlockSpec((None, t, w), lambda i: (i, 0, 0))
    full = lambda a: pl.BlockSpec(a.shape, lambda i: (0,) * a.ndim)
    out_spec = pl.BlockSpec((None, ncp, w), lambda i: (i, 0, 0))
    return pl.pallas_call(
        _compress_kernel,
        grid=(b,),
        in_specs=[per_b, per_b, full(kpos), full(kw1), full(kw2), full(vpos), full(vw1), full(vw2)],
        out_specs=[out_spec, out_spec],
        out_shape=[jax.ShapeDtypeStruct((b, ncp, w), F32)] * 2,
        compiler_params=_params("parallel"),
        name="compress",
    )(kc3, vc3, kpos, kw1, kw2, vpos, vw1, vw2)


def _cmpsel_kernel(q_ref, kc_ref, vc_ref, ovt_ref, o_ref, sel_ref, *, tq, n_cmp):
    i = pl.program_id(1)
    t0 = i * tq
    g = NSA_HEADS // NSA_KV_HEADS
    ncp = kc_ref.shape[0]
    ns = ovt_ref.shape[0]
    rows = g * tq
    tpos = t0 + (lax.broadcasted_iota(I32, (rows, ncp), 0) & (tq - 1))
    cidx = lax.broadcasted_iota(I32, (rows, ncp), 1)
    vis = (cidx * CMP_STRIDE + (CMP_LEN - 1) <= tpos) & (cidx < n_cmp)
    jidx = lax.broadcasted_iota(I32, (ns, tq), 0)
    tcol = t0 + lax.broadcasted_iota(I32, (ns, tq), 1)
    cur = tcol // SEL_BLOCK
    forced = (jidx == 0) | (jidx == cur) | (jidx == cur - 1)
    valid = jidx * SEL_BLOCK <= tcol
    for hk in range(NSA_KV_HEADS):
        kk = kc_ref[:, hk * HEAD_DIM:(hk + 1) * HEAD_DIM].astype(BF16)
        vv = vc_ref[:, hk * HEAD_DIM:(hk + 1) * HEAD_DIM].astype(BF16)
        q4 = jnp.concatenate(
            [q_ref[:, (hk * g + gi) * HEAD_DIM:(hk * g + gi + 1) * HEAD_DIM] for gi in range(g)], axis=0)
        s = lax.dot_general(q4, kk, (((1,), (1,)), ((), ())), preferred_element_type=F32) * (HEAD_DIM ** -0.5)
        s = jnp.where(vis, s, NEG_BIG)
        m = jnp.max(s, axis=-1, keepdims=True)
        e = jnp.exp(s - m)
        p = jnp.where(vis, e / jnp.sum(e, axis=-1, keepdims=True), 0.0)
        o = jnp.dot(p.astype(BF16), vv, preferred_element_type=F32)
        for gi in range(g):
            h = hk * g + gi
            o_ref[:, h * HEAD_DIM:(h + 1) * HEAD_DIM] = o[gi * tq:(gi + 1) * tq]
        psum = p[0:tq]
        for gi in range(1, g):
            psum = psum + p[gi * tq:(gi + 1) * tq]
        imp_t = lax.dot_general(ovt_ref[...], psum, (((1,), (1,)), ((), ())),
                                precision=lax.Precision.HIGHEST, preferred_element_type=F32)
        score = jnp.where(forced, jnp.inf, jnp.where(valid, imp_t, -jnp.inf))
        cnt = jnp.zeros((ns, tq), F32)
        for r in range(ns):
            row = score[r:r + 1, :]
            beats = (row > score) | ((row >= score) & (jidx > r))
            cnt = cnt + jnp.where(beats, 1.0, 0.0)
        sel_ref[hk] = jnp.where((cnt < SEL_TOPN) & (score > -jnp.inf), 1.0, 0.0)


def _cmpsel(qb3, kcmp, vcmp, ovt, tq, n_cmp):
    b, t, w = qb3.shape
    ncp = kcmp.shape[1]
    ns = ovt.shape[0]
    kern = functools.partial(_cmpsel_kernel, tq=tq, n_cmp=n_cmp)
    return pl.pallas_call(
        kern,
        grid=(b, t // tq),
        in_specs=[pl.BlockSpec((None, tq, w), lambda bi, i: (bi, i, 0)),
                  pl.BlockSpec((None, ncp, LANES), lambda bi, i: (bi, 0, 0)),
                  pl.BlockSpec((None, ncp, LANES), lambda bi, i: (bi, 0, 0)),
                  pl.BlockSpec(ovt.shape, lambda bi, i: (0, 0))],
        out_specs=[pl.BlockSpec((None, tq, w), lambda bi, i: (bi, i, 0)),
                   pl.BlockSpec((None, NSA_KV_HEADS, ns, tq), lambda bi, i: (bi, 0, 0, i))],
        out_shape=[jax.ShapeDtypeStruct((b, t, w), F32), jax.ShapeDtypeStruct((b, NSA_KV_HEADS, ns, t), F32)],
        compiler_params=_params("parallel", "parallel"),
        name="cmpsel",
    )(qb3, kcmp, vcmp, ovt)


def _selattn_kernel(q_ref, k_ref, vt_ref, sel_ref, o_ref, m_sc, l_sc, acc_sc, *, tq, tk):
    i = pl.program_id(1)
    t0 = i * tq
    g = NSA_HEADS // NSA_KV_HEADS
    rows = g * tq
    assert tk % tq == 0, "one key tile must cover the whole diagonal of a query tile"
    n_below = (i * tq) // tk
    nb = tk // SEL_BLOCK
    qpos = t0 + (lax.broadcasted_iota(I32, (tk, rows), 1) & (tq - 1))
    krow = lax.broadcasted_iota(I32, (tk, rows), 0)
    for hk in range(NSA_KV_HEADS):
        qt = jnp.concatenate(
            [jnp.transpose(q_ref[:, (hk * g + gi) * HEAD_DIM:(hk * g + gi + 1) * HEAD_DIM].astype(F32))
             for gi in range(g)], axis=1)
        qt = (qt * (HEAD_DIM ** -0.5)).astype(BF16)
        m_sc[...] = jnp.full(m_sc.shape, NEG_BIG, F32)
        l_sc[...] = jnp.zeros(l_sc.shape, F32)
        acc_sc[...] = jnp.zeros(acc_sc.shape, F32)

        def tile(kt, diagonal):
            ks = pl.multiple_of(kt * tk, tk)
            kk = k_ref[pl.ds(ks, tk), hk * HEAD_DIM:(hk + 1) * HEAD_DIM]
            s = jnp.dot(kk, qt, preferred_element_type=F32)
            bias = jnp.concatenate(
                [jnp.broadcast_to(
                    jnp.concatenate([(1.0 - sel_ref[hk, pl.ds(kt * nb + jj, 1), :]) * NEG_BIG] * g, axis=1),
                    (SEL_BLOCK, rows)) for jj in range(nb)], axis=0)
            s = s + bias
            if diagonal:
                s = jnp.where(ks + krow <= qpos, s, NEG_BIG)
            m_old = m_sc[...]
            m_new = jnp.maximum(m_old, jnp.max(s, axis=0, keepdims=True))
            alpha = jnp.exp(m_old - m_new)
            p = jnp.exp(s - m_new)
            l_sc[...] = alpha * l_sc[...] + jnp.sum(p, axis=0, keepdims=True)
            vt = vt_ref[kt, hk * HEAD_DIM:(hk + 1) * HEAD_DIM, :]
            acc_sc[...] = alpha * acc_sc[...] + jnp.dot(vt, p.astype(BF16), preferred_element_type=F32)
            m_sc[...] = m_new

        def below_diagonal(kt, carry):
            tile(kt, False)
            return carry

        lax.fori_loop(0, n_below, below_diagonal, 0)
        tile(n_below, True)
        o_t = acc_sc[...] / l_sc[...]
        for gi in range(g):
            h = hk * g + gi
            o_ref[:, h * HEAD_DIM:(h + 1) * HEAD_DIM] = jnp.transpose(o_t[:, gi * tq:(gi + 1) * tq])


def _selattn(qb3, ks3, vst4, sel4, tq, tk):
    b, t, w = qb3.shape
    g = NSA_HEADS // NSA_KV_HEADS
    ns = sel4.shape[2]
    kern = functools.partial(_selattn_kernel, tq=tq, tk=tk)
    return pl.pallas_call(
        kern,
        grid=(b, t // tq),
        in_specs=[pl.BlockSpec((None, tq, w), lambda bi, i: (bi, i, 0)),
                  pl.BlockSpec((None, t, LANES), lambda bi, i: (bi, 0, 0)),
                  pl.BlockSpec((None, t // tk, LANES, tk), lambda bi, i: (bi, 0, 0, 0)),
                  pl.BlockSpec((None, NSA_KV_HEADS, ns, tq), lambda bi, i: (bi, 0, 0, i))],
        out_specs=pl.BlockSpec((None, tq, w), lambda bi, i: (bi, i, 0)),
        out_shape=jax.ShapeDtypeStruct((b, t, w), F32),
        scratch_shapes=[pltpu.VMEM((1, g * tq), F32), pltpu.VMEM((1, g * tq), F32),
                        pltpu.VMEM((HEAD_DIM, g * tq), F32)],
        compiler_params=_params("parallel", "arbitrary"),
        name="selattn",
    )(qb3, ks3, vst4, sel4)


def _band_kernel(*refs, window, tq, span, n_heads, n_kv, has_sink):
    if has_sink:
        sink_ref, q_ref, k_ref, v_ref, o_ref = refs
    else:
        q_ref, k_ref, v_ref, o_ref = refs
    i = pl.program_id(1)
    t0 = i * tq
    start = pl.multiple_of(jnp.maximum(t0 + tq - span, 0), tq)
    g = n_heads // n_kv
    rows = g * tq
    qpos = t0 + (lax.broadcasted_iota(I32, (rows, span), 0) & (tq - 1))
    kpos = start + lax.broadcasted_iota(I32, (rows, span), 1)
    diff = qpos - kpos
    mask = (diff >= 0) & (diff < window)
    for hk in range(n_kv):
        kk = k_ref[pl.ds(start, span), hk * HEAD_DIM:(hk + 1) * HEAD_DIM]
        vv = v_ref[pl.ds(start, span), hk * HEAD_DIM:(hk + 1) * HEAD_DIM]
        q4 = jnp.concatenate(
            [q_ref[:, (hk * g + gi) * HEAD_DIM:(hk * g + gi + 1) * HEAD_DIM] for gi in range(g)], axis=0)
        s = lax.dot_general(q4, kk, (((1,), (1,)), ((), ())), preferred_element_type=F32) * (HEAD_DIM ** -0.5)
        s = jnp.where(mask, s, NEG_BIG)
        m = jnp.max(s, axis=-1, keepdims=True)
        if has_sink:
            sink = jnp.concatenate([jnp.full((tq, 1), sink_ref[hk * g + gi], F32) for gi in range(g)], axis=0)
            m = jnp.maximum(m, sink)
        p = jnp.where(mask, jnp.exp(s - m), 0.0)
        l = jnp.sum(p, axis=-1, keepdims=True)
        if has_sink:
            l = l + jnp.exp(sink - m)
        o = jnp.dot(p.astype(BF16), vv, preferred_element_type=F32) / l
        for gi in range(g):
            h = hk * g + gi
            o_ref[:, h * HEAD_DIM:(h + 1) * HEAD_DIM] = o[gi * tq:(gi + 1) * tq]


def _band(q3, k3, v3, sinks, window, tq, n_heads, n_kv):
    b, t, w = q3.shape
    nprev = -(-(window - 1) // tq)
    span = min((nprev + 1) * tq, t)
    has_sink = sinks is not None
    kern = functools.partial(_band_kernel, window=window, tq=tq, span=span, n_heads=n_heads, n_kv=n_kv,
                             has_sink=has_sink)
    in_specs = [pl.BlockSpec((None, tq, w), lambda bi, i: (bi, i, 0)),
                pl.BlockSpec((None, t, LANES), lambda bi, i: (bi, 0, 0)),
                pl.BlockSpec((None, t, LANES), lambda bi, i: (bi, 0, 0))]
    args = [q3, k3, v3]
    if has_sink:
        in_specs = [pl.BlockSpec(memory_space=pltpu.SMEM)] + in_specs
        args = [sinks] + args
    return pl.pallas_call(
        kern,
        grid=(b, t // tq),
        in_specs=in_specs,
        out_specs=pl.BlockSpec((None, tq, w), lambda bi, i: (bi, i, 0)),
        out_shape=jax.ShapeDtypeStruct((b, t, w), F32),
        compiler_params=_params("parallel", "parallel"),
        name="band_sink" if has_sink else "band_win",
    )(*args)


def _topk_rows(s, k, ridx):
    big = float(s.shape[0])
    vals, idxs = [], []
    for _ in range(k):
        m = jnp.max(s, axis=0, keepdims=True)
        first = jnp.min(jnp.where(s == m, ridx, big), axis=0, keepdims=True)
        vals.append(m)
        idxs.append(first)
        s = jnp.where(ridx == first, -jnp.inf, s)
    return jnp.concatenate(vals, axis=0), jnp.concatenate(idxs, axis=0)


def _peer_candidates(t1, i1, t2, i2):
    vals = [t1[0:1] + t2]
    eids = [i1[0:1] * float(N_KEYS) + i2]
    for a in range(1, 8):
        vals.append(t1[a:a + 1] + t2[0:8])
        eids.append(i1[a:a + 1] * float(N_KEYS) + i2[0:8])
    vals.append(t1[8:16] + t2[0:1])
    eids.append(i1[8:16] * float(N_KEYS) + i2[0:1])
    return jnp.concatenate(vals, axis=0), jnp.concatenate(eids, axis=0)


def _route_kernel(x_ref, oa_ref, oc_ref, os_ref, ow_ref, gt_ref, gexp_ref, wo_ref, gf_ref, wq_ref,
                  k1_ref, k2_ref, x1_ref, eidx_ref, gate_ref):
    tm = x_ref.shape[0]
    gt = gt_ref[...]
    g_hi = gt.astype(BF16)
    g_lo = (gt - g_hi.astype(F32)).astype(BF16)
    ob = jnp.zeros(oc_ref.shape, F32)
    for j, br in enumerate((oc_ref, os_ref, ow_ref)):
        ex = gexp_ref[j]
        gj = (jnp.dot(g_hi, ex, preferred_element_type=F32) + jnp.dot(g_lo, ex, preferred_element_type=F32))
        ob = ob + gj * br[...]
    half = oa_ref.shape[1]
    mixed = (jnp.dot(oa_ref[...].astype(BF16), wo_ref[0:half, :], preferred_element_type=F32)
             + jnp.dot(ob.astype(BF16), wo_ref[half:, :], preferred_element_type=F32))
    x1 = x_ref[...] + mixed
    x1_ref[...] = x1
    h2 = x1 * lax.rsqrt(jnp.mean(x1 * x1, axis=-1, keepdims=True) + RMS_EPS) * gf_ref[...]
    q = jnp.dot(h2.astype(BF16), wq_ref[...], preferred_element_type=F32)
    hq = PEER_QDIM // 2
    ridx = lax.broadcasted_iota(I32, (N_KEYS, tm), 0).astype(F32)
    e_rows, g_rows = [], []
    for h in range(PEER_HEADS):
        q1 = q[:, h * PEER_QDIM: h * PEER_QDIM + hq].astype(BF16)
        q2 = q[:, h * PEER_QDIM + hq:(h + 1) * PEER_QDIM].astype(BF16)
        s1 = lax.dot_general(k1_ref[...], q1, (((1,), (1,)), ((), ())), preferred_element_type=F32)
        s2 = lax.dot_general(k2_ref[...], q2, (((1,), (1,)), ((), ())), preferred_element_type=F32)
        t1, i1 = _topk_rows(s1, PEER_TOPK, ridx)
        t2, i2 = _topk_rows(s2, PEER_TOPK, ridx)
        cand, cand_e = _peer_candidates(t1, i1, t2, i2)
        cidx = lax.broadcasted_iota(I32, cand.shape, 0).astype(F32)
        big = float(cand.shape[0])
        sc, ee = [], []
        for _ in range(PEER_TOPK):
            m = jnp.max(cand, axis=0, keepdims=True)
            first = jnp.min(jnp.where(cand == m, cidx, big), axis=0, keepdims=True)
            hit = cidx == first
            sc.append(m)
            ee.append(jnp.max(jnp.where(hit, cand_e, -1.0), axis=0, keepdims=True))
            cand = jnp.where(hit, -jnp.inf, cand)
        sc = jnp.concatenate(sc, axis=0)
        ex = jnp.exp(sc - sc[0:1])
        g_rows.append(ex / jnp.sum(ex, axis=0, keepdims=True))
        e_rows.append(jnp.concatenate(ee, axis=0))
    e_t = jnp.concatenate(e_rows, axis=0)
    g_t = jnp.concatenate(g_rows, axis=0)
    eidx_ref[...] = jnp.transpose(e_t).astype(I32)
    gate_ref[...] = jnp.transpose(g_t)


def _route(x2, oa, oc, osel, ow, gates, gexp, wo, gf, wq, k1, k2, tm):
    n, d = x2.shape
    row = lambda w: pl.BlockSpec((tm, w), lambda i: (i, 0))
    full = lambda a: pl.BlockSpec(a.shape, lambda i: (0,) * a.ndim)
    nk = PEER_HEADS * PEER_TOPK
    return pl.pallas_call(
        _route_kernel,
        grid=(n // tm,),
        in_specs=[row(d), row(oa.shape[1]), row(oc.shape[1]), row(osel.shape[1]), row(ow.shape[1]), row(LANES),
                  full(gexp), full(wo), full(gf), full(wq), full(k1), full(k2)],
        out_specs=[row(d), row(nk), row(nk)],
        out_shape=[jax.ShapeDtypeStruct((n, d), F32), jax.ShapeDtypeStruct((n, nk), I32),
                   jax.ShapeDtypeStruct((n, nk), F32)],
        compiler_params=_params("parallel"),
        name="route",
    )(x2, oa, oc, osel, ow, gates, gexp, wo, gf, wq, k1, k2)


def _fold8(p):
    sub = lax.broadcasted_iota(I32, (SUBLANES, LANES), 0)
    lo4 = sub < 4
    q = [jnp.where(lo4, p[j], p[j + 4]) + pltpu.roll(jnp.where(lo4, p[j + 4], p[j]), 4, 0) for j in range(4)]
    m2 = (sub & 3) < 2
    r = [jnp.where(m2, q[j] + pltpu.roll(q[j], 6, 0), q[j + 2] + pltpu.roll(q[j + 2], 2, 0)) for j in range(2)]
    m1 = (sub & 1) == 0
    return jnp.where(m1, r[0] + pltpu.roll(r[0], 7, 0), r[1] + pltpu.roll(r[1], 1, 0))


def _peer_kernel(e0_ref, e1_ref, e2_ref, x_ref, gate_ref, gffn_ref, gfin_ref, tab_ref, o_ref,
                 buf, sem, a_sc, cb_sc, *, tb):
    step = pl.program_id(0)
    nsteps = pl.num_programs(0)
    slot = step % PEER_SLOTS
    fill = (step + PEER_SLOTS - 1) % PEER_SLOTS
    nk = PEER_HEADS * PEER_TOPK
    npair = tb * nk

    def issue_rows(eref, dst_slot, t, k0, k1):
        erow = eref.at[t]
        for k in range(k0, k1):
            copy = pltpu.make_async_copy(tab_ref.at[erow[k]], buf.at[dst_slot, t * nk + k], sem.at[dst_slot])
            copy.start(priority=k % 2)

    def wait_slot(s):
        pltpu.make_async_copy(tab_ref.at[pl.ds(0, npair)], buf.at[s], sem.at[s]).wait()

    @pl.when(step == 0)
    def _():
        def first(t, carry):
            issue_rows(e0_ref, 0, t, 0, nk)
            return carry
        lax.fori_loop(0, tb, first, 0)
        def second(t, carry):
            issue_rows(e1_ref, 1, t, 0, nk)
            return carry
        lax.fori_loop(0, tb, second, 0)

    eye = jnp.where(lax.broadcasted_iota(I32, (nk, nk), 0) == lax.broadcasted_iota(I32, (nk, nk), 1),
                    1.0, 0.0).astype(BF16)
    gr = gate_ref[...]
    g0 = gr.astype(BF16)
    r1 = gr - g0.astype(F32)
    g1 = r1.astype(BF16)
    g2 = (r1 - g1.astype(F32)).astype(BF16)
    dn = (((1,), (1,)), ((), ()))
    gate_t = (lax.dot_general(eye, g0, dn, preferred_element_type=F32)
              + lax.dot_general(eye, g1, dn, preferred_element_type=F32)
              + lax.dot_general(eye, g2, dn, preferred_element_type=F32))

    wait_slot(slot)

    lane_tb = lax.broadcasted_iota(I32, (nk, tb), 1)
    inv_d = 1.0 / (SUBLANES * LANES)
    a_sc[...] = jnp.zeros(a_sc.shape, F32)

    ngroup = nk // SUBLANES
    per_group = nk // (2 * ngroup)

    def dots(t, carry):
        x8 = x_ref[t]
        ms = jnp.sum(jnp.sum(x8 * x8, axis=1, keepdims=True), axis=0, keepdims=True) * inv_d
        h8 = x8 * lax.rsqrt(ms + RMS_EPS) * gffn_ref[...]
        base = t * nk
        folded = []
        for gi in range(nk // SUBLANES):
            blk = buf[slot, pl.ds(base + gi * SUBLANES, SUBLANES), 0:SUBLANES, :]
            folded.append(_fold8([blk[j] * h8 for j in range(SUBLANES)]))
            issue_rows(e2_ref, fill, t, gi * per_group, (gi + 1) * per_group)
        a = jnp.sum(jnp.concatenate(folded, axis=0), axis=1, keepdims=True)
        a_sc[...] = jnp.where(lane_tb == t, a, a_sc[...])
        return carry

    lax.fori_loop(0, tb, dots, 0, unroll=2)

    c_all = gate_t * _gelu(a_sc[...])
    for t in range(tb):
        cb_sc[t] = jnp.broadcast_to(c_all[:, t:t + 1], (nk, LANES))

    def combine(t, carry):
        base = t * nk
        accs = [jnp.zeros((SUBLANES, LANES), F32) for _ in range(4)]
        for gi in range(ngroup):
            for k in range(gi * SUBLANES, (gi + 1) * SUBLANES):
                ck = jnp.broadcast_to(cb_sc[t, k:k + 1, :], (SUBLANES, LANES))
                accs[k % 4] = accs[k % 4] + ck * buf[slot, base + k, SUBLANES:2 * SUBLANES, :]
            issue_rows(e2_ref, fill, t, nk // 2 + gi * per_group, nk // 2 + (gi + 1) * per_group)
        y8 = x_ref[t] + ((accs[0] + accs[1]) + (accs[2] + accs[3]))
        ms2 = jnp.sum(jnp.sum(y8 * y8, axis=1, keepdims=True), axis=0, keepdims=True) * inv_d
        o_ref[t] = y8 * lax.rsqrt(ms2 + RMS_EPS) * gfin_ref[...]
        return carry

    lax.fori_loop(0, tb, combine, 0, unroll=4)

    @pl.when(step == nsteps - 1)
    def _():
        for ahead in range(1, PEER_SLOTS):
            wait_slot((step + ahead) % PEER_SLOTS)


def _peer(eidx, x1s, gate, gffn8, gfin8, table, tb):
    n = x1s.shape[0]
    nk = PEER_HEADS * PEER_TOPK
    nsteps = n // tb
    kern = functools.partial(_peer_kernel, tb=tb)
    return pl.pallas_call(
        kern,
        grid=(nsteps,),
        in_specs=[pl.BlockSpec((tb, nk), lambda i: (i, 0), memory_space=pltpu.SMEM),
                  pl.BlockSpec((tb, nk), lambda i: ((i + 1) % nsteps, 0), memory_space=pltpu.SMEM),
                  pl.BlockSpec((tb, nk), lambda i: ((i + PEER_SLOTS - 1) % nsteps, 0), memory_space=pltpu.SMEM),
                  pl.BlockSpec((tb, SUBLANES, LANES), lambda i: (i, 0, 0)),
                  pl.BlockSpec((tb, nk), lambda i: (i, 0)),
                  pl.BlockSpec((SUBLANES, LANES), lambda i: (0, 0)),
                  pl.BlockSpec((SUBLANES, LANES), lambda i: (0, 0)),
                  pl.BlockSpec(memory_space=pl.ANY)],
        out_specs=pl.BlockSpec((tb, SUBLANES, LANES), lambda i: (i, 0, 0)),
        out_shape=jax.ShapeDtypeStruct((n, SUBLANES, LANES), F32),
        scratch_shapes=[pltpu.VMEM((PEER_SLOTS, tb * nk, 2 * SUBLANES, LANES), F32),
                        pltpu.SemaphoreType.DMA((PEER_SLOTS,)),
                        pltpu.VMEM((nk, tb), F32),
                        pltpu.VMEM((tb, nk, LANES), F32)],
        compiler_params=_params("arbitrary"),
        name="peer",
    )(eidx, eidx, eidx, x1s, gate, gffn8, gfin8, table)


def _overlap_t(t):
    nc = (t - CMP_LEN) // CMP_STRIDE + 1
    ns = t // SEL_BLOCK
    ncp = t // CMP_STRIDE
    cstart = np.arange(nc) * CMP_STRIDE
    bstart = np.arange(ns) * SEL_BLOCK
    lo = np.maximum(cstart[:, None], bstart[None, :])
    hi = np.minimum(cstart[:, None] + CMP_LEN, bstart[None, :] + SEL_BLOCK)
    ov = np.clip(hi - lo, 0, None).astype(np.float32) / CMP_LEN
    out = np.zeros((ns, ncp), np.float32)
    out[:, :nc] = ov.T
    return out, nc


def _gate_expand():
    ex = np.zeros((NSA_N_GATES, LANES, NSA_HEADS * HEAD_DIM), np.float32)
    for j in range(NSA_N_GATES):
        for h in range(NSA_HEADS):
            ex[j, h * NSA_N_GATES + j, h * HEAD_DIM:(h + 1) * HEAD_DIM] = 1.0
    return ex


def _layer(x, positions, norm_attn, w_in, attn_sinks, cmp_k_pos, cmp_k_w1, cmp_k_w2, cmp_v_pos, cmp_v_w1,
           cmp_v_w2, w_out, norm_ffn, w_query, sub_keys_1, sub_keys_2, expert_down, expert_up, norm_out,
           tm_in=512, tq_cmp=256, tq_sel=256, tk_sel=512, tq_band=128, tm_route=256, tb_peer=16):
    b, t, d = x.shape
    n = b * t
    x2 = x.reshape(n, d)
    half = HEAD_DIM // 2
    inv_freq = ROPE_THETA ** (-jnp.arange(half, dtype=F32) / half)
    invf = jnp.tile(inv_freq, LANES // half).reshape(1, LANES)
    w_pad = jnp.pad(w_in, ((0, 0), (0, IN_PAD - IN_WIDTH))).astype(BF16)
    pr = _inproj(x2, positions.reshape(n, 1), norm_attn.reshape(1, d), w_pad, invf, min(tm_in, n))
    r3 = lambda a: a.reshape(b, t, a.shape[-1])

    o_a = _band(r3(pr["qa"]), r3(pr["ka"]), r3(pr["va"]), attn_sinks.astype(F32), SWA_WINDOW, tq_band,
                SWA_HEADS, SWA_KV_HEADS)
    kcmp, vcmp = _compress(r3(pr["kc"]), r3(pr["vc"]),
                           cmp_k_pos.reshape(1, -1), cmp_k_w1.astype(BF16), cmp_k_w2.astype(BF16),
                           cmp_v_pos.reshape(1, -1), cmp_v_w1.astype(BF16), cmp_v_w2.astype(BF16))
    ovt, n_cmp = _overlap_t(t)
    qb3 = r3(pr["qb"])
    o_cmp, sel = _cmpsel(qb3, kcmp, vcmp, jnp.asarray(ovt), tq_cmp, n_cmp)
    tk_sel = min(tk_sel, t)
    vst4 = jnp.swapaxes(r3(pr["vs"]).reshape(b, t // tk_sel, tk_sel, LANES), 2, 3)
    o_sel = _selattn(qb3, r3(pr["ks"]), vst4, sel, tq_sel, tk_sel)
    o_win = _band(qb3, r3(pr["kw"]), r3(pr["vw"]), None, NSA_WINDOW, tq_band, NSA_HEADS, NSA_KV_HEADS)

    x1, eidx, gate = _route(x2, o_a.reshape(n, -1), o_cmp.reshape(n, -1), o_sel.reshape(n, -1),
                            o_win.reshape(n, -1), pr["gates"], jnp.asarray(_gate_expand(), BF16),
                            w_out.astype(BF16), norm_ffn.reshape(1, d), w_query.astype(BF16),
                            sub_keys_1.astype(BF16), sub_keys_2.astype(BF16), min(tm_route, n))
    ne = expert_down.shape[0]
    table = jnp.concatenate([expert_down.reshape(ne, SUBLANES, LANES), expert_up.reshape(ne, SUBLANES, LANES)],
                            axis=1)
    y = _peer(eidx, x1.reshape(n, SUBLANES, LANES), gate, norm_ffn.reshape(SUBLANES, LANES),
              norm_out.reshape(SUBLANES, LANES), table, tb_peer)
    return y.reshape(b, t, d)


def kernel(x, positions, norm_attn, w_in, attn_sinks, cmp_k_pos, cmp_k_w1, cmp_k_w2, cmp_v_pos, cmp_v_w1,
           cmp_v_w2, w_out, norm_ffn, peer_w_query, peer_sub_keys_1, peer_sub_keys_2, peer_expert_down,
           peer_expert_up, norm_f):
    assert norm_attn.shape[0] == 1, "single-layer block"
    return _layer(x, positions, norm_attn[0], w_in[0], attn_sinks[0], cmp_k_pos[0], cmp_k_w1[0], cmp_k_w2[0],
                  cmp_v_pos[0], cmp_v_w1[0], cmp_v_w2[0], w_out[0], norm_ffn[0], peer_w_query[0],
                  peer_sub_keys_1[0], peer_sub_keys_2[0], peer_expert_down[0], peer_expert_up[0], norm_f)
```

```python
import functools

import numpy as np
import jax
import jax.numpy as jnp
from jax import lax
from jax.experimental import pallas as pl
from jax.experimental.pallas import tpu as pltpu

F32 = jnp.float32
BF16 = jnp.bfloat16
I32 = jnp.int32

HEAD_DIM = 64
ROPE_THETA = 10000.0
RMS_EPS = 1e-6
NEG_BIG = -1e30

SWA_HEADS = 8
SWA_KV_HEADS = 2
SWA_WINDOW = 128

NSA_HEADS = 8
NSA_KV_HEADS = 2
CMP_LEN = 32
CMP_STRIDE = 16
CMP_HIDDEN = 256
SEL_BLOCK = 64
SEL_TOPN = 16
NSA_WINDOW = 512
NSA_N_GATES = 3

PEER_HEADS = 8
N_KEYS = 128
PEER_QDIM = 256
PEER_TOPK = 16
PEER_SLOTS = 3

LANES = 128
SUBLANES = 8
VMEM_LIMIT = 56 * 1024 * 1024

_IN_SPLITS = (("qa", 512), ("ka", 128), ("va", 128), ("qb", 512), ("kc", 128), ("vc", 128),
              ("ks", 128), ("vs", 128), ("kw", 128), ("vw", 128), ("gl", 24))
IN_WIDTH = sum(w for _, w in _IN_SPLITS)
IN_PAD = 2176
_IN_OFF = {}
_o = 0
for _n, _w in _IN_SPLITS:
    _IN_OFF[_n] = _o
    _o += _w
_ROPED = ("qa", "ka", "qb", "kc", "ks", "kw")


def _gelu(x):
    return 0.5 * x * (1.0 + lax.erf(x * 0.7071067811865476))


def _params(*sem):
    return pltpu.CompilerParams(dimension_semantics=sem, vmem_limit_bytes=VMEM_LIMIT)


def _inproj_kernel(x_ref, pos_ref, g_ref, w_ref, invf_ref,
                   qa_ref, ka_ref, va_ref, qb_ref, kc_ref, vc_ref, ks_ref, vs_ref, kw_ref, vw_ref, gt_ref):
    x = x_ref[...]
    h = x * lax.rsqrt(jnp.mean(x * x, axis=-1, keepdims=True) + RMS_EPS) * g_ref[...]
    proj = jnp.dot(h.astype(BF16), w_ref[...], preferred_element_type=F32)
    ang = pos_ref[...].astype(F32) * invf_ref[...]
    lane = lax.broadcasted_iota(I32, ang.shape, 1)
    lo = (lane & (HEAD_DIM - 1)) < (HEAD_DIM // 2)
    cos = jnp.cos(ang)
    sin = jnp.sin(ang)
    sin_s = jnp.where(lo, -sin, sin)
    outs = {"qa": qa_ref, "ka": ka_ref, "va": va_ref, "qb": qb_ref, "kc": kc_ref, "vc": vc_ref,
            "ks": ks_ref, "vs": vs_ref, "kw": kw_ref, "vw": vw_ref}
    for name, width in _IN_SPLITS[:-1]:
        off = _IN_OFF[name]
        ref = outs[name]
        for i in range(width // LANES):
            v = proj[:, off + LANES * i: off + LANES * (i + 1)]
            if name in _ROPED:
                rot = jnp.where(lo, pltpu.roll(v, LANES - HEAD_DIM // 2, 1), pltpu.roll(v, HEAD_DIM // 2, 1))
                v = v * cos + rot * sin_s
            ref[:, LANES * i: LANES * (i + 1)] = v.astype(ref.dtype)
    gl = proj[:, _IN_OFF["gl"]: _IN_OFF["gl"] + LANES]
    gt_ref[...] = jax.nn.sigmoid(gl)


def _inproj(x2, pos2, g, w_pad, invf, tm):
    n, d = x2.shape
    row = lambda w: pl.BlockSpec((tm, w), lambda i: (i, 0))
    full = lambda a: pl.BlockSpec(a.shape, lambda i: (0,) * a.ndim)
    out_dt = {"qa": BF16, "ka": BF16, "va": BF16, "qb": BF16, "kc": F32, "vc": F32,
              "ks": BF16, "vs": BF16, "kw": BF16, "vw": BF16}
    names = [nm for nm, _ in _IN_SPLITS[:-1]]
    widths = dict(_IN_SPLITS)
    out_shape = [jax.ShapeDtypeStruct((n, widths[nm]), out_dt[nm]) for nm in names]
    out_shape.append(jax.ShapeDtypeStruct((n, LANES), F32))
    out_specs = [row(widths[nm]) for nm in names] + [row(LANES)]
    res = pl.pallas_call(
        _inproj_kernel,
        grid=(n // tm,),
        in_specs=[row(d), row(1), full(g), full(w_pad), full(invf)],
        out_specs=out_specs,
        out_shape=out_shape,
        compiler_params=_params("parallel"),
        name="inproj",
    )(x2, pos2, g, w_pad, invf)
    out = dict(zip(names, res[:-1]))
    out["gates"] = res[-1]
    return out


def _compress_kernel(kc_ref, vc_ref, kpos_ref, kw1_ref, kw2_ref, vpos_ref, vw1_ref, vw2_ref, ko_ref, vo_ref):
    ncp = ko_ref.shape[0]
    for src, pos_ref, w1_ref, w2_ref, out_ref in ((kc_ref, kpos_ref, kw1_ref, kw2_ref, ko_ref),
                                                  (vc_ref, vpos_ref, vw1_ref, vw2_ref, vo_ref)):
        bias = jnp.dot(pos_ref[...].astype(BF16), w1_ref[...], preferred_element_type=F32)
        acc_a = [jnp.zeros((ncp, CMP_HIDDEN), F32) for _ in range(2)]
        acc_b = [jnp.zeros((ncp, CMP_HIDDEN), F32) for _ in range(2)]
        for j in range(CMP_STRIDE):
            pj = src[pl.ds(j, ncp, stride=CMP_STRIDE), :].astype(BF16)
            wa = w1_ref[j * HEAD_DIM:(j + 1) * HEAD_DIM, :]
            wb = w1_ref[(CMP_STRIDE + j) * HEAD_DIM:(CMP_STRIDE + j + 1) * HEAD_DIM, :]
            for hh in range(2):
                ph = pj[:, hh * HEAD_DIM:(hh + 1) * HEAD_DIM]
                acc_a[hh] = acc_a[hh] + jnp.dot(ph, wa, preferred_element_type=F32)
                acc_b[hh] = acc_b[hh] + jnp.dot(ph, wb, preferred_element_type=F32)
        for hh in range(2):
            h1 = acc_a[hh] + pltpu.roll(acc_b[hh], ncp - 1, 0) + bias
            out = jnp.dot(_gelu(h1).astype(BF16), w2_ref[...], preferred_element_type=F32)
            out_ref[:, hh * HEAD_DIM:(hh + 1) * HEAD_DIM] = out


def _compress(kc3, vc3, kpos, kw1, kw2, vpos, vw1, vw2):
    b, t, w = kc3.shape
    ncp = t // CMP_STRIDE
    per_b = pl.BlockSpec((None, t, w), lambda i: (i, 0, 0))
    full = lambda a: pl.BlockSpec(a.shape, lambda i: (0,) * a.ndim)
    out_spec = pl.BlockSpec((None, ncp, w), lambda i: (i, 0, 0))
    return pl.pallas_call(
        _compress_kernel,
        grid=(b,),
        in_specs=[per_b, per_b, full(kpos), full(kw1), full(kw2), full(vpos), full(vw1), full(vw2)],
        out_specs=[out_spec, out_spec],
        out_shape=[jax.ShapeDtypeStruct((b, ncp, w), F32)] * 2,
        compiler_params=_params("parallel"),
        name="compress",
    )(kc3, vc3, kpos, kw1, kw2, vpos, vw1, vw2)


def _cmpsel_kernel(q_ref, kc_ref, vc_ref, ovt_ref, o_ref, sel_ref, *, tq, n_cmp):
    i = pl.program_id(1)
    t0 = i * tq
    g = NSA_HEADS // NSA_KV_HEADS
    ncp = kc_ref.shape[0]
    ns = ovt_ref.shape[0]
    rows = g * tq
    tpos = t0 + (lax.broadcasted_iota(I32, (rows, ncp), 0) & (tq - 1))
    cidx = lax.broadcasted_iota(I32, (rows, ncp), 1)
    vis = (cidx * CMP_STRIDE + (CMP_LEN - 1) <= tpos) & (cidx < n_cmp)
    jidx = lax.broadcasted_iota(I32, (ns, tq), 0)
    tcol = t0 + lax.broadcasted_iota(I32, (ns, tq), 1)
    cur = tcol // SEL_BLOCK
    forced = (jidx == 0) | (jidx == cur) | (jidx == cur - 1)
    valid = jidx * SEL_BLOCK <= tcol
    for hk in range(NSA_KV_HEADS):
        kk = kc_ref[:, hk * HEAD_DIM:(hk + 1) * HEAD_DIM].astype(BF16)
        vv = vc_ref[:, hk * HEAD_DIM:(hk + 1) * HEAD_DIM].astype(BF16)
        q4 = jnp.concatenate(
            [q_ref[:, (hk * g + gi) * HEAD_DIM:(hk * g + gi + 1) * HEAD_DIM] for gi in range(g)], axis=0)
        s = lax.dot_general(q4, kk, (((1,), (1,)), ((), ())), preferred_element_type=F32) * (HEAD_DIM ** -0.5)
        s = jnp.where(vis, s, NEG_BIG)
        m = jnp.max(s, axis=-1, keepdims=True)
        e = jnp.exp(s - m)
        p = jnp.where(vis, e / jnp.sum(e, axis=-1, keepdims=True), 0.0)
        o = jnp.dot(p.astype(BF16), vv, preferred_element_type=F32)
        for gi in range(g):
            h = hk * g + gi
            o_ref[:, h * HEAD_DIM:(h + 1) * HEAD_DIM] = o[gi * tq:(gi + 1) * tq]
        psum = p[0:tq]
        for gi in range(1, g):
            psum = psum + p[gi * tq:(gi + 1) * tq]
        imp_t = lax.dot_general(ovt_ref[...], psum, (((1,), (1,)), ((), ())),
                                precision=lax.Precision.HIGHEST, preferred_element_type=F32)
        score = jnp.where(forced, jnp.inf, jnp.where(valid, imp_t, -jnp.inf))
        cnt = jnp.zeros((ns, tq), F32)
        for r in range(ns):
            row = score[r:r + 1, :]
            beats = (row > score) | ((row >= score) & (jidx > r))
            cnt = cnt + jnp.where(beats, 1.0, 0.0)
        sel_ref[hk] = jnp.where((cnt < SEL_TOPN) & (score > -jnp.inf), 1.0, 0.0)


def _cmpsel(qb3, kcmp, vcmp, ovt, tq, n_cmp):
    b, t, w = qb3.shape
    ncp = kcmp.shape[1]
    ns = ovt.shape[0]
    kern = functools.partial(_cmpsel_kernel, tq=tq, n_cmp=n_cmp)
    return pl.pallas_call(
        kern,
        grid=(b, t // tq),
        in_specs=[pl.BlockSpec((None, tq, w), lambda bi, i: (bi, i, 0)),
                  pl.BlockSpec((None, ncp, LANES), lambda bi, i: (bi, 0, 0)),
                  pl.BlockSpec((None, ncp, LANES), lambda bi, i: (bi, 0, 0)),
                  pl.BlockSpec(ovt.shape, lambda bi, i: (0, 0))],
        out_specs=[pl.BlockSpec((None, tq, w), lambda bi, i: (bi, i, 0)),
                   pl.BlockSpec((None, NSA_KV_HEADS, ns, tq), lambda bi, i: (bi, 0, 0, i))],
        out_shape=[jax.ShapeDtypeStruct((b, t, w), F32), jax.ShapeDtypeStruct((b, NSA_KV_HEADS, ns, t), F32)],
        compiler_params=_params("parallel", "parallel"),
        name="cmpsel",
    )(qb3, kcmp, vcmp, ovt)


def _selattn_kernel(q_ref, k_ref, vt_ref, sel_ref, o_ref, m_sc, l_sc, acc_sc, *, tq, tk):
    i = pl.program_id(1)
    t0 = i * tq
    g = NSA_HEADS // NSA_KV_HEADS
    rows = g * tq
    assert tk % tq == 0, "one key tile must cover the whole diagonal of a query tile"
    n_below = (i * tq) // tk
    nb = tk // SEL_BLOCK
    qpos = t0 + (lax.broadcasted_iota(I32, (tk, rows), 1) & (tq - 1))
    krow = lax.broadcasted_iota(I32, (tk, rows), 0)
    for hk in range(NSA_KV_HEADS):
        qt = jnp.concatenate(
            [jnp.transpose(q_ref[:, (hk * g + gi) * HEAD_DIM:(hk * g + gi + 1) * HEAD_DIM].astype(F32))
             for gi in range(g)], axis=1)
        qt = (qt * (HEAD_DIM ** -0.5)).astype(BF16)
        m_sc[...] = jnp.full(m_sc.shape, NEG_BIG, F32)
        l_sc[...] = jnp.zeros(l_sc.shape, F32)
        acc_sc[...] = jnp.zeros(acc_sc.shape, F32)

        def tile(kt, diagonal):
            ks = pl.multiple_of(kt * tk, tk)
            kk = k_ref[pl.ds(ks, tk), hk * HEAD_DIM:(hk + 1) * HEAD_DIM]
            s = jnp.dot(kk, qt, preferred_element_type=F32)
            bias = jnp.concatenate(
                [jnp.broadcast_to(
                    jnp.concatenate([(1.0 - sel_ref[hk, pl.ds(kt * nb + jj, 1), :]) * NEG_BIG] * g, axis=1),
                    (SEL_BLOCK, rows)) for jj in range(nb)], axis=0)
            s = s + bias
            if diagonal:
                s = jnp.where(ks + krow <= qpos, s, NEG_BIG)
            m_old = m_sc[...]
            m_new = jnp.maximum(m_old, jnp.max(s, axis=0, keepdims=True))
            alpha = jnp.exp(m_old - m_new)
            p = jnp.exp(s - m_new)
            l_sc[...] = alpha * l_sc[...] + jnp.sum(p, axis=0, keepdims=True)
            vt = vt_ref[kt, hk * HEAD_DIM:(hk + 1) * HEAD_DIM, :]
            acc_sc[...] = alpha * acc_sc[...] + jnp.dot(vt, p.astype(BF16), preferred_element_type=F32)
            m_sc[...] = m_new

        def below_diagonal(kt, carry):
            tile(kt, False)
            return carry

        lax.fori_loop(0, n_below, below_diagonal, 0)
        tile(n_below, True)
        o_t = acc_sc[...] / l_sc[...]
        for gi in range(g):
            h = hk * g + gi
            o_ref[:, h * HEAD_DIM:(h + 1) * HEAD_DIM] = jnp.transpose(o_t[:, gi * tq:(gi + 1) * tq])


def _selattn(qb3, ks3, vst4, sel4, tq, tk):
    b, t, w = qb3.shape
    g = NSA_HEADS // NSA_KV_HEADS
    ns = sel4.shape[2]
    kern = functools.partial(_selattn_kernel, tq=tq, tk=tk)
    return pl.pallas_call(
        kern,
        grid=(b, t // tq),
        in_specs=[pl.BlockSpec((None, tq, w), lambda bi, i: (bi, i, 0)),
                  pl.BlockSpec((None, t, LANES), lambda bi, i: (bi, 0, 0)),
                  pl.BlockSpec((None, t // tk, LANES, tk), lambda bi, i: (bi, 0, 0, 0)),
                  pl.BlockSpec((None, NSA_KV_HEADS, ns, tq), lambda bi, i: (bi, 0, 0, i))],
        out_specs=pl.BlockSpec((None, tq, w), lambda bi, i: (bi, i, 0)),
        out_shape=jax.ShapeDtypeStruct((b, t, w), F32),
        scratch_shapes=[pltpu.VMEM((1, g * tq), F32), pltpu.VMEM((1, g * tq), F32),
                        pltpu.VMEM((HEAD_DIM, g * tq), F32)],
        compiler_params=_params("parallel", "arbitrary"),
        name="selattn",
    )(qb3, ks3, vst4, sel4)


def _band_kernel(*refs, window, tq, span, n_heads, n_kv, has_sink):
    if has_sink:
        sink_ref, q_ref, k_ref, v_ref, o_ref = refs
    else:
        q_ref, k_ref, v_ref, o_ref = refs
    i = pl.program_id(1)
    t0 = i * tq
    start = pl.multiple_of(jnp.maximum(t0 + tq - span, 0), tq)
    g = n_heads // n_kv
    rows = g * tq
    qpos = t0 + (lax.broadcasted_iota(I32, (rows, span), 0) & (tq - 1))
    kpos = start + lax.broadcasted_iota(I32, (rows, span), 1)
    diff = qpos - kpos
    mask = (diff >= 0) & (diff < window)
    for hk in range(n_kv):
        kk = k_ref[pl.ds(start, span), hk * HEAD_DIM:(hk + 1) * HEAD_DIM]
        vv = v_ref[pl.ds(start, span), hk * HEAD_DIM:(hk + 1) * HEAD_DIM]
        q4 = jnp.concatenate(
            [q_ref[:, (hk * g + gi) * HEAD_DIM:(hk * g + gi + 1) * HEAD_DIM] for gi in range(g)], axis=0)
        s = lax.dot_general(q4, kk, (((1,), (1,)), ((), ())), preferred_element_type=F32) * (HEAD_DIM ** -0.5)
        s = jnp.where(mask, s, NEG_BIG)
        m = jnp.max(s, axis=-1, keepdims=True)
        if has_sink:
            sink = jnp.concatenate([jnp.full((tq, 1), sink_ref[hk * g + gi], F32) for gi in range(g)], axis=0)
            m = jnp.maximum(m, sink)
        p = jnp.where(mask, jnp.exp(s - m), 0.0)
        l = jnp.sum(p, axis=-1, keepdims=True)
        if has_sink:
            l = l + jnp.exp(sink - m)
        o = jnp.dot(p.astype(BF16), vv, preferred_element_type=F32) / l
        for gi in range(g):
            h = hk * g + gi
            o_ref[:, h * HEAD_DIM:(h + 1) * HEAD_DIM] = o[gi * tq:(gi + 1) * tq]


def _band(q3, k3, v3, sinks, window, tq, n_heads, n_kv):
    b, t, w = q3.shape
    nprev = -(-(window - 1) // tq)
    span = min((nprev + 1) * tq, t)
    has_sink = sinks is not None
    kern = functools.partial(_band_kernel, window=window, tq=tq, span=span, n_heads=n_heads, n_kv=n_kv,
                             has_sink=has_sink)
    in_specs = [pl.BlockSpec((None, tq, w), lambda bi, i: (bi, i, 0)),
                pl.BlockSpec((None, t, LANES), lambda bi, i: (bi, 0, 0)),
                pl.BlockSpec((None, t, LANES), lambda bi, i: (bi, 0, 0))]
    args = [q3, k3, v3]
    if has_sink:
        in_specs = [pl.BlockSpec(memory_space=pltpu.SMEM)] + in_specs
        args = [sinks] + args
    return pl.pallas_call(
        kern,
        grid=(b, t // tq),
        in_specs=in_specs,
        out_specs=pl.BlockSpec((None, tq, w), lambda bi, i: (bi, i, 0)),
        out_shape=jax.ShapeDtypeStruct((b, t, w), F32),
        compiler_params=_params("parallel", "parallel"),
        name="band_sink" if has_sink else "band_win",
    )(*args)


def _topk_rows(s, k, ridx):
    big = float(s.shape[0])
    vals, idxs = [], []
    for _ in range(k):
        m = jnp.max(s, axis=0, keepdims=True)
        first = jnp.min(jnp.where(s == m, ridx, big), axis=0, keepdims=True)
        vals.append(m)
        idxs.append(first)
        s = jnp.where(ridx == first, -jnp.inf, s)
    return jnp.concatenate(vals, axis=0), jnp.concatenate(idxs, axis=0)


def _peer_candidates(t1, i1, t2, i2):
    vals = [t1[0:1] + t2]
    eids = [i1[0:1] * float(N_KEYS) + i2]
    for a in range(1, 8):
        vals.append(t1[a:a + 1] + t2[0:8])
        eids.append(i1[a:a + 1] * float(N_KEYS) + i2[0:8])
    vals.append(t1[8:16] + t2[0:1])
    eids.append(i1[8:16] * float(N_KEYS) + i2[0:1])
    return jnp.concatenate(vals, axis=0), jnp.concatenate(eids, axis=0)


def _route_kernel(x_ref, oa_ref, oc_ref, os_ref, ow_ref, gt_ref, gexp_ref, wo_ref, gf_ref, wq_ref,
                  k1_ref, k2_ref, x1_ref, eidx_ref, gate_ref):
    tm = x_ref.shape[0]
    gt = gt_ref[...]
    g_hi = gt.astype(BF16)
    g_lo = (gt - g_hi.astype(F32)).astype(BF16)
    ob = jnp.zeros(oc_ref.shape, F32)
    for j, br in enumerate((oc_ref, os_ref, ow_ref)):
        ex = gexp_ref[j]
        gj = (jnp.dot(g_hi, ex, preferred_element_type=F32) + jnp.dot(g_lo, ex, preferred_element_type=F32))
        ob = ob + gj * br[...]
    half = oa_ref.shape[1]
    mixed = (jnp.dot(oa_ref[...].astype(BF16), wo_ref[0:half, :], preferred_element_type=F32)
             + jnp.dot(ob.astype(BF16), wo_ref[half:, :], preferred_element_type=F32))
    x1 = x_ref[...] + mixed
    x1_ref[...] = x1
    h2 = x1 * lax.rsqrt(jnp.mean(x1 * x1, axis=-1, keepdims=True) + RMS_EPS) * gf_ref[...]
    q = jnp.dot(h2.astype(BF16), wq_ref[...], preferred_element_type=F32)
    hq = PEER_QDIM // 2
    ridx = lax.broadcasted_iota(I32, (N_KEYS, tm), 0).astype(F32)
    e_rows, g_rows = [], []
    for h in range(PEER_HEADS):
        q1 = q[:, h * PEER_QDIM: h * PEER_QDIM + hq].astype(BF16)
        q2 = q[:, h * PEER_QDIM + hq:(h + 1) * PEER_QDIM].astype(BF16)
        s1 = lax.dot_general(k1_ref[...], q1, (((1,), (1,)), ((), ())), preferred_element_type=F32)
        s2 = lax.dot_general(k2_ref[...], q2, (((1,), (1,)), ((), ())), preferred_element_type=F32)
        t1, i1 = _topk_rows(s1, PEER_TOPK, ridx)
        t2, i2 = _topk_rows(s2, PEER_TOPK, ridx)
        cand, cand_e = _peer_candidates(t1, i1, t2, i2)
        cidx = lax.broadcasted_iota(I32, cand.shape, 0).astype(F32)
        big = float(cand.shape[0])
        sc, ee = [], []
        for _ in range(PEER_TOPK):
            m = jnp.max(cand, axis=0, keepdims=True)
            first = jnp.min(jnp.where(cand == m, cidx, big), axis=0, keepdims=True)
            hit = cidx == first
            sc.append(m)
            ee.append(jnp.max(jnp.where(hit, cand_e, -1.0), axis=0, keepdims=True))
            cand = jnp.where(hit, -jnp.inf, cand)
        sc = jnp.concatenate(sc, axis=0)
        ex = jnp.exp(sc - sc[0:1])
        g_rows.append(ex / jnp.sum(ex, axis=0, keepdims=True))
        e_rows.append(jnp.concatenate(ee, axis=0))
    e_t = jnp.concatenate(e_rows, axis=0)
    g_t = jnp.concatenate(g_rows, axis=0)
    eidx_ref[...] = jnp.transpose(e_t).astype(I32)
    gate_ref[...] = jnp.transpose(g_t)


def _route(x2, oa, oc, osel, ow, gates, gexp, wo, gf, wq, k1, k2, tm):
    n, d = x2.shape
    row = lambda w: pl.BlockSpec((tm, w), lambda i: (i, 0))
    full = lambda a: pl.BlockSpec(a.shape, lambda i: (0,) * a.ndim)
    nk = PEER_HEADS * PEER_TOPK
    return pl.pallas_call(
        _route_kernel,
        grid=(n // tm,),
        in_specs=[row(d), row(oa.shape[1]), row(oc.shape[1]), row(osel.shape[1]), row(ow.shape[1]), row(LANES),
                  full(gexp), full(wo), full(gf), full(wq), full(k1), full(k2)],
        out_specs=[row(d), row(nk), row(nk)],
        out_shape=[jax.ShapeDtypeStruct((n, d), F32), jax.ShapeDtypeStruct((n, nk), I32),
                   jax.ShapeDtypeStruct((n, nk), F32)],
        compiler_params=_params("parallel"),
        name="route",
    )(x2, oa, oc, osel, ow, gates, gexp, wo, gf, wq, k1, k2)


def _fold8(p):
    sub = lax.broadcasted_iota(I32, (SUBLANES, LANES), 0)
    lo4 = sub < 4
    q = [jnp.where(lo4, p[j], p[j + 4]) + pltpu.roll(jnp.where(lo4, p[j + 4], p[j]), 4, 0) for j in range(4)]
    m2 = (sub & 3) < 2
    r = [jnp.where(m2, q[j] + pltpu.roll(q[j], 6, 0), q[j + 2] + pltpu.roll(q[j + 2], 2, 0)) for j in range(2)]
    m1 = (sub & 1) == 0
    return jnp.where(m1, r[0] + pltpu.roll(r[0], 7, 0), r[1] + pltpu.roll(r[1], 1, 0))


def _peer_kernel(e0_ref, e1_ref, e2_ref, x_ref, gate_ref, gffn_ref, gfin_ref, tab_ref, o_ref,
                 buf, sem, a_sc, cb_sc, *, tb):
    step = pl.program_id(0)
    nsteps = pl.num_programs(0)
    slot = step % PEER_SLOTS
    fill = (step + PEER_SLOTS - 1) % PEER_SLOTS
    nk = PEER_HEADS * PEER_TOPK
    npair = tb * nk

    def issue_rows(eref, dst_slot, t, k0, k1):
        erow = eref.at[t]
        for k in range(k0, k1):
            copy = pltpu.make_async_copy(tab_ref.at[erow[k]], buf.at[dst_slot, t * nk + k], sem.at[dst_slot])
            copy.start(priority=k % 2)

    def wait_slot(s):
        pltpu.make_async_copy(tab_ref.at[pl.ds(0, npair)], buf.at[s], sem.at[s]).wait()

    @pl.when(step == 0)
    def _():
        def first(t, carry):
            issue_rows(e0_ref, 0, t, 0, nk)
            return carry
        lax.fori_loop(0, tb, first, 0)
        def second(t, carry):
            issue_rows(e1_ref, 1, t, 0, nk)
            return carry
        lax.fori_loop(0, tb, second, 0)

    eye = jnp.where(lax.broadcasted_iota(I32, (nk, nk), 0) == lax.broadcasted_iota(I32, (nk, nk), 1),
                    1.0, 0.0).astype(BF16)
    gr = gate_ref[...]
    g0 = gr.astype(BF16)
    r1 = gr - g0.astype(F32)
    g1 = r1.astype(BF16)
    g2 = (r1 - g1.astype(F32)).astype(BF16)
    dn = (((1,), (1,)), ((), ()))
    gate_t = (lax.dot_general(eye, g0, dn, preferred_element_type=F32)
              + lax.dot_general(eye, g1, dn, preferred_element_type=F32)
              + lax.dot_general(eye, g2, dn, preferred_element_type=F32))

    wait_slot(slot)

    lane_tb = lax.broadcasted_iota(I32, (nk, tb), 1)
    inv_d = 1.0 / (SUBLANES * LANES)
    a_sc[...] = jnp.zeros(a_sc.shape, F32)

    ngroup = nk // SUBLANES
    rows_p1 = 6
    rows_p2 = SUBLANES - rows_p1

    def dots(t, carry):
        x8 = x_ref[t]
        ms = jnp.sum(jnp.sum(x8 * x8, axis=1, keepdims=True), axis=0, keepdims=True) * inv_d
        h8 = x8 * lax.rsqrt(ms + RMS_EPS) * gffn_ref[...]
        base = t * nk
        folded = []
        for gi in range(nk // SUBLANES):
            blk = buf[slot, pl.ds(base + gi * SUBLANES, SUBLANES), 0:SUBLANES, :]
            folded.append(_fold8([blk[j] * h8 for j in range(SUBLANES)]))
            issue_rows(e2_ref, fill, t, gi * rows_p1, (gi + 1) * rows_p1)
        a = jnp.sum(jnp.concatenate(folded, axis=0), axis=1, keepdims=True)
        a_sc[...] = jnp.where(lane_tb == t, a, a_sc[...])
        return carry

    lax.fori_loop(0, tb, dots, 0, unroll=2)

    a_sc[...] = gate_t * _gelu(a_sc[...])

    def combine(t, carry):
        base = t * nk
        w_col = jnp.sum(jnp.where(lane_tb == t, a_sc[...], 0.0), axis=1, keepdims=True)
        cb_sc[t] = jnp.broadcast_to(w_col, (nk, LANES))
        accs = [jnp.zeros((SUBLANES, LANES), F32) for _ in range(4)]
        for gi in range(ngroup):
            for k in range(gi * SUBLANES, (gi + 1) * SUBLANES):
                ck = jnp.broadcast_to(cb_sc[t, k:k + 1, :], (SUBLANES, LANES))
                accs[k % 4] = accs[k % 4] + ck * buf[slot, base + k, SUBLANES:2 * SUBLANES, :]
            issue_rows(e2_ref, fill, t, ngroup * rows_p1 + gi * rows_p2, ngroup * rows_p1 + (gi + 1) * rows_p2)
        y8 = x_ref[t] + ((accs[0] + accs[1]) + (accs[2] + accs[3]))
        ms2 = jnp.sum(jnp.sum(y8 * y8, axis=1, keepdims=True), axis=0, keepdims=True) * inv_d
        o_ref[t] = y8 * lax.rsqrt(ms2 + RMS_EPS) * gfin_ref[...]
        return carry

    lax.fori_loop(0, tb, combine, 0, unroll=4)

    @pl.when(step == nsteps - 1)
    def _():
        for ahead in range(1, PEER_SLOTS):
            wait_slot((step + ahead) % PEER_SLOTS)


def _peer(eidx, x1s, gate, gffn8, gfin8, table, tb):
    n = x1s.shape[0]
    nk = PEER_HEADS * PEER_TOPK
    nsteps = n // tb
    kern = functools.partial(_peer_kernel, tb=tb)
    return pl.pallas_call(
        kern,
        grid=(nsteps,),
        in_specs=[pl.BlockSpec((tb, nk), lambda i: (i, 0), memory_space=pltpu.SMEM),
                  pl.BlockSpec((tb, nk), lambda i: ((i + 1) % nsteps, 0), memory_space=pltpu.SMEM),
                  pl.BlockSpec((tb, nk), lambda i: ((i + PEER_SLOTS - 1) % nsteps, 0), memory_space=pltpu.SMEM),
                  pl.BlockSpec((tb, SUBLANES, LANES), lambda i: (i, 0, 0)),
                  pl.BlockSpec((tb, nk), lambda i: (i, 0)),
                  pl.BlockSpec((SUBLANES, LANES), lambda i: (0, 0)),
                  pl.BlockSpec((SUBLANES, LANES), lambda i: (0, 0)),
                  pl.BlockSpec(memory_space=pl.ANY)],
        out_specs=pl.BlockSpec((tb, SUBLANES, LANES), lambda i: (i, 0, 0)),
        out_shape=jax.ShapeDtypeStruct((n, SUBLANES, LANES), F32),
        scratch_shapes=[pltpu.VMEM((PEER_SLOTS, tb * nk, 2 * SUBLANES, LANES), F32),
                        pltpu.SemaphoreType.DMA((PEER_SLOTS,)),
                        pltpu.VMEM((nk, tb), F32),
                        pltpu.VMEM((tb, nk, LANES), F32)],
        compiler_params=_params("arbitrary"),
        name="peer",
    )(eidx, eidx, eidx, x1s, gate, gffn8, gfin8, table)


def _overlap_t(t):
    nc = (t - CMP_LEN) // CMP_STRIDE + 1
    ns = t // SEL_BLOCK
    ncp = t // CMP_STRIDE
    cstart = np.arange(nc) * CMP_STRIDE
    bstart = np.arange(ns) * SEL_BLOCK
    lo = np.maximum(cstart[:, None], bstart[None, :])
    hi = np.minimum(cstart[:, None] + CMP_LEN, bstart[None, :] + SEL_BLOCK)
    ov = np.clip(hi - lo, 0, None).astype(np.float32) / CMP_LEN
    out = np.zeros((ns, ncp), np.float32)
    out[:, :nc] = ov.T
    return out, nc


def _gate_expand():
    ex = np.zeros((NSA_N_GATES, LANES, NSA_HEADS * HEAD_DIM), np.float32)
    for j in range(NSA_N_GATES):
        for h in range(NSA_HEADS):
            ex[j, h * NSA_N_GATES + j, h * HEAD_DIM:(h + 1) * HEAD_DIM] = 1.0
    return ex


def _layer(x, positions, norm_attn, w_in, attn_sinks, cmp_k_pos, cmp_k_w1, cmp_k_w2, cmp_v_pos, cmp_v_w1,
           cmp_v_w2, w_out, norm_ffn, w_query, sub_keys_1, sub_keys_2, expert_down, expert_up, norm_out,
           tm_in=512, tq_cmp=256, tq_sel=256, tk_sel=512, tq_band=128, tm_route=256, tb_peer=16):
    b, t, d = x.shape
    n = b * t
    x2 = x.reshape(n, d)
    half = HEAD_DIM // 2
    inv_freq = ROPE_THETA ** (-jnp.arange(half, dtype=F32) / half)
    invf = jnp.tile(inv_freq, LANES // half).reshape(1, LANES)
    w_pad = jnp.pad(w_in, ((0, 0), (0, IN_PAD - IN_WIDTH))).astype(BF16)
    pr = _inproj(x2, positions.reshape(n, 1), norm_attn.reshape(1, d), w_pad, invf, min(tm_in, n))
    r3 = lambda a: a.reshape(b, t, a.shape[-1])

    o_a = _band(r3(pr["qa"]), r3(pr["ka"]), r3(pr["va"]), attn_sinks.astype(F32), SWA_WINDOW, tq_band,
                SWA_HEADS, SWA_KV_HEADS)
    kcmp, vcmp = _compress(r3(pr["kc"]), r3(pr["vc"]),
                           cmp_k_pos.reshape(1, -1), cmp_k_w1.astype(BF16), cmp_k_w2.astype(BF16),
                           cmp_v_pos.reshape(1, -1), cmp_v_w1.astype(BF16), cmp_v_w2.astype(BF16))
    ovt, n_cmp = _overlap_t(t)
    qb3 = r3(pr["qb"])
    o_cmp, sel = _cmpsel(qb3, kcmp, vcmp, jnp.asarray(ovt), tq_cmp, n_cmp)
    tk_sel = min(tk_sel, t)
    vst4 = jnp.swapaxes(r3(pr["vs"]).reshape(b, t // tk_sel, tk_sel, LANES), 2, 3)
    o_sel = _selattn(qb3, r3(pr["ks"]), vst4, sel, tq_sel, tk_sel)
    o_win = _band(qb3, r3(pr["kw"]), r3(pr["vw"]), None, NSA_WINDOW, tq_band, NSA_HEADS, NSA_KV_HEADS)

    x1, eidx, gate = _route(x2, o_a.reshape(n, -1), o_cmp.reshape(n, -1), o_sel.reshape(n, -1),
                            o_win.reshape(n, -1), pr["gates"], jnp.asarray(_gate_expand(), BF16),
                            w_out.astype(BF16), norm_ffn.reshape(1, d), w_query.astype(BF16),
                            sub_keys_1.astype(BF16), sub_keys_2.astype(BF16), min(tm_route, n))
    ne = expert_down.shape[0]
    table = jnp.concatenate([expert_down.reshape(ne, SUBLANES, LANES), expert_up.reshape(ne, SUBLANES, LANES)],
                            axis=1)
    y = _peer(eidx, x1.reshape(n, SUBLANES, LANES), gate, norm_ffn.reshape(SUBLANES, LANES),
              norm_out.reshape(SUBLANES, LANES), table, tb_peer)
    return y.reshape(b, t, d)


def kernel(x, positions, norm_attn, w_in, attn_sinks, cmp_k_pos, cmp_k_w1, cmp_k_w2, cmp_v_pos, cmp_v_w1,
           cmp_v_w2, w_out, norm_ffn, peer_w_query, peer_sub_keys_1, peer_sub_keys_2, peer_expert_down,
           peer_expert_up, norm_f):
    assert norm_attn.shape[0] == 1, "single-layer block"
    return _layer(x, positions, norm_attn[0], w_in[0], attn_sinks[0], cmp_k_pos[0], cmp_k_w1[0], cmp_k_w2[0],
                  cmp_v_pos[0], cmp_v_w1[0], cmp_v_w2[0], w_out[0], norm_ffn[0], peer_w_query[0],
                  peer_sub_keys_1[0], peer_sub_keys_2[0], peer_expert_down[0], peer_expert_up[0], norm_f)
```

```python
import functools

import numpy as np
import jax
import jax.numpy as jnp
from jax import lax
from jax.experimental import pallas as pl
from jax.experimental.pallas import tpu as pltpu

F32 = jnp.float32
BF16 = jnp.bfloat16
I32 = jnp.int32

HEAD_DIM = 64
ROPE_THETA = 10000.0
RMS_EPS = 1e-6
NEG_BIG = -1e30

SWA_HEADS = 8
SWA_KV_HEADS = 2
SWA_WINDOW = 128

NSA_HEADS = 8
NSA_KV_HEADS = 2
CMP_LEN = 32
CMP_STRIDE = 16
CMP_HIDDEN = 256
SEL_BLOCK = 64
SEL_TOPN = 16
NSA_WINDOW = 512
NSA_N_GATES = 3

PEER_HEADS = 8
N_KEYS = 128
PEER_QDIM = 256
PEER_TOPK = 16
PEER_SLOTS = 3

LANES = 128
SUBLANES = 8
VMEM_LIMIT = 56 * 1024 * 1024

_IN_SPLITS = (("qa", 512), ("ka", 128), ("va", 128), ("qb", 512), ("kc", 128), ("vc", 128),
              ("ks", 128), ("vs", 128), ("kw", 128), ("vw", 128), ("gl", 24))
IN_WIDTH = sum(w for _, w in _IN_SPLITS)
IN_PAD = 2176
_IN_OFF = {}
_o = 0
for _n, _w in _IN_SPLITS:
    _IN_OFF[_n] = _o
    _o += _w
_ROPED = ("qa", "ka", "qb", "kc", "ks", "kw")


def _gelu(x):
    return 0.5 * x * (1.0 + lax.erf(x * 0.7071067811865476))


def _params(*sem):
    return pltpu.CompilerParams(dimension_semantics=sem, vmem_limit_bytes=VMEM_LIMIT)


def _inproj_kernel(x_ref, pos_ref, g_ref, w_ref, invf_ref,
                   qa_ref, ka_ref, va_ref, qb_ref, kc_ref, vc_ref, ks_ref, vs_ref, kw_ref, vw_ref, gt_ref):
    x = x_ref[...]
    h = x * lax.rsqrt(jnp.mean(x * x, axis=-1, keepdims=True) + RMS_EPS) * g_ref[...]
    proj = jnp.dot(h.astype(BF16), w_ref[...], preferred_element_type=F32)
    ang = pos_ref[...].astype(F32) * invf_ref[...]
    lane = lax.broadcasted_iota(I32, ang.shape, 1)
    lo = (lane & (HEAD_DIM - 1)) < (HEAD_DIM // 2)
    cos = jnp.cos(ang)
    sin = jnp.sin(ang)
    sin_s = jnp.where(lo, -sin, sin)
    outs = {"qa": qa_ref, "ka": ka_ref, "va": va_ref, "qb": qb_ref, "kc": kc_ref, "vc": vc_ref,
            "ks": ks_ref, "vs": vs_ref, "kw": kw_ref, "vw": vw_ref}
    for name, width in _IN_SPLITS[:-1]:
        off = _IN_OFF[name]
        ref = outs[name]
        for i in range(width // LANES):
            v = proj[:, off + LANES * i: off + LANES * (i + 1)]
            if name in _ROPED:
                rot = jnp.where(lo, pltpu.roll(v, LANES - HEAD_DIM // 2, 1), pltpu.roll(v, HEAD_DIM // 2, 1))
                v = v * cos + rot * sin_s
            ref[:, LANES * i: LANES * (i + 1)] = v.astype(ref.dtype)
    gl = proj[:, _IN_OFF["gl"]: _IN_OFF["gl"] + LANES]
    gt_ref[...] = jax.nn.sigmoid(gl)


def _inproj(x2, pos2, g, w_pad, invf, tm):
    n, d = x2.shape
    row = lambda w: pl.BlockSpec((tm, w), lambda i: (i, 0))
    full = lambda a: pl.BlockSpec(a.shape, lambda i: (0,) * a.ndim)
    out_dt = {"qa": BF16, "ka": BF16, "va": BF16, "qb": BF16, "kc": F32, "vc": F32,
              "ks": BF16, "vs": BF16, "kw": BF16, "vw": BF16}
    names = [nm for nm, _ in _IN_SPLITS[:-1]]
    widths = dict(_IN_SPLITS)
    out_shape = [jax.ShapeDtypeStruct((n, widths[nm]), out_dt[nm]) for nm in names]
    out_shape.append(jax.ShapeDtypeStruct((n, LANES), F32))
    out_specs = [row(widths[nm]) for nm in names] + [row(LANES)]
    res = pl.pallas_call(
        _inproj_kernel,
        grid=(n // tm,),
        in_specs=[row(d), row(1), full(g), full(w_pad), full(invf)],
        out_specs=out_specs,
        out_shape=out_shape,
        compiler_params=_params("parallel"),
        name="inproj",
    )(x2, pos2, g, w_pad, invf)
    out = dict(zip(names, res[:-1]))
    out["gates"] = res[-1]
    return out


def _compress_kernel(kc_ref, vc_ref, kpos_ref, kw1_ref, kw2_ref, vpos_ref, vw1_ref, vw2_ref, ko_ref, vo_ref):
    ncp = ko_ref.shape[0]
    for src, pos_ref, w1_ref, w2_ref, out_ref in ((kc_ref, kpos_ref, kw1_ref, kw2_ref, ko_ref),
                                                  (vc_ref, vpos_ref, vw1_ref, vw2_ref, vo_ref)):
        bias = jnp.dot(pos_ref[...].astype(BF16), w1_ref[...], preferred_element_type=F32)
        acc_a = [jnp.zeros((ncp, CMP_HIDDEN), F32) for _ in range(2)]
        acc_b = [jnp.zeros((ncp, CMP_HIDDEN), F32) for _ in range(2)]
        for j in range(CMP_STRIDE):
            pj = src[pl.ds(j, ncp, stride=CMP_STRIDE), :].astype(BF16)
            wa = w1_ref[j * HEAD_DIM:(j + 1) * HEAD_DIM, :]
            wb = w1_ref[(CMP_STRIDE + j) * HEAD_DIM:(CMP_STRIDE + j + 1) * HEAD_DIM, :]
            for hh in range(2):
                ph = pj[:, hh * HEAD_DIM:(hh + 1) * HEAD_DIM]
                acc_a[hh] = acc_a[hh] + jnp.dot(ph, wa, preferred_element_type=F32)
                acc_b[hh] = acc_b[hh] + jnp.dot(ph, wb, preferred_element_type=F32)
        for hh in range(2):
            h1 = acc_a[hh] + pltpu.roll(acc_b[hh], ncp - 1, 0) + bias
            out = jnp.dot(_gelu(h1).astype(BF16), w2_ref[...], preferred_element_type=F32)
            out_ref[:, hh * HEAD_DIM:(hh + 1) * HEAD_DIM] = out


def _compress(kc3, vc3, kpos, kw1, kw2, vpos, vw1, vw2):
    b, t, w = kc3.shape
    ncp = t // CMP_STRIDE
    per_b = pl.BlockSpec((None, t, w), lambda i: (i, 0, 0))
    full = lambda a: pl.BlockSpec(a.shape, lambda i: (0,) * a.ndim)
    out_spec = pl.BlockSpec((None, ncp, w), lambda i: (i, 0, 0))
    return pl.pallas_call(
        _compress_kernel,
        grid=(b,),
        in_specs=[per_b, per_b, full(kpos), full(kw1), full(kw2), full(vpos), full(vw1), full(vw2)],
        out_specs=[out_spec, out_spec],
        out_shape=[jax.ShapeDtypeStruct((b, ncp, w), F32)] * 2,
        compiler_params=_params("parallel"),
        name="compress",
    )(kc3, vc3, kpos, kw1, kw2, vpos, vw1, vw2)


def _cmpsel_kernel(q_ref, kc_ref, vc_ref, ovt_ref, o_ref, sel_ref, *, tq, n_cmp):
    i = pl.program_id(1)
    t0 = i * tq
    g = NSA_HEADS // NSA_KV_HEADS
    ncp = kc_ref.shape[0]
    ns = ovt_ref.shape[0]
    rows = g * tq
    tpos = t0 + (lax.broadcasted_iota(I32, (rows, ncp), 0) & (tq - 1))
    cidx = lax.broadcasted_iota(I32, (rows, ncp), 1)
    vis = (cidx * CMP_STRIDE + (CMP_LEN - 1) <= tpos) & (cidx < n_cmp)
    jidx = lax.broadcasted_iota(I32, (ns, tq), 0)
    tcol = t0 + lax.broadcasted_iota(I32, (ns, tq), 1)
    cur = tcol // SEL_BLOCK
    forced = (jidx == 0) | (jidx == cur) | (jidx == cur - 1)
    valid = jidx * SEL_BLOCK <= tcol
    for hk in range(NSA_KV_HEADS):
        kk = kc_ref[:, hk * HEAD_DIM:(hk + 1) * HEAD_DIM].astype(BF16)
        vv = vc_ref[:, hk * HEAD_DIM:(hk + 1) * HEAD_DIM].astype(BF16)
        q4 = jnp.concatenate(
            [q_ref[:, (hk * g + gi) * HEAD_DIM:(hk * g + gi + 1) * HEAD_DIM] for gi in range(g)], axis=0)
        s = lax.dot_general(q4, kk, (((1,), (1,)), ((), ())), preferred_element_type=F32) * (HEAD_DIM ** -0.5)
        s = jnp.where(vis, s, NEG_BIG)
        m = jnp.max(s, axis=-1, keepdims=True)
        e = jnp.exp(s - m)
        p = jnp.where(vis, e / jnp.sum(e, axis=-1, keepdims=True), 0.0)
        o = jnp.dot(p.astype(BF16), vv, preferred_element_type=F32)
        for gi in range(g):
            h = hk * g + gi
            o_ref[:, h * HEAD_DIM:(h + 1) * HEAD_DIM] = o[gi * tq:(gi + 1) * tq]
        psum = p[0:tq]
        for gi in range(1, g):
            psum = psum + p[gi * tq:(gi + 1) * tq]
        imp_t = lax.dot_general(ovt_ref[...], psum, (((1,), (1,)), ((), ())),
                                precision=lax.Precision.HIGHEST, preferred_element_type=F32)
        score = jnp.where(forced, jnp.inf, jnp.where(valid, imp_t, -jnp.inf))
        cnt = jnp.zeros((ns, tq), F32)
        for r in range(ns):
            row = score[r:r + 1, :]
            beats = (row > score) | ((row >= score) & (jidx > r))
            cnt = cnt + jnp.where(beats, 1.0, 0.0)
        sel_ref[hk] = jnp.where((cnt < SEL_TOPN) & (score > -jnp.inf), 1.0, 0.0)


def _cmpsel(qb3, kcmp, vcmp, ovt, tq, n_cmp):
    b, t, w = qb3.shape
    ncp = kcmp.shape[1]
    ns = ovt.shape[0]
    kern = functools.partial(_cmpsel_kernel, tq=tq, n_cmp=n_cmp)
    return pl.pallas_call(
        kern,
        grid=(b, t // tq),
        in_specs=[pl.BlockSpec((None, tq, w), lambda bi, i: (bi, i, 0)),
                  pl.BlockSpec((None, ncp, LANES), lambda bi, i: (bi, 0, 0)),
                  pl.BlockSpec((None, ncp, LANES), lambda bi, i: (bi, 0, 0)),
                  pl.BlockSpec(ovt.shape, lambda bi, i: (0, 0))],
        out_specs=[pl.BlockSpec((None, tq, w), lambda bi, i: (bi, i, 0)),
                   pl.BlockSpec((None, NSA_KV_HEADS, ns, tq), lambda bi, i: (bi, 0, 0, i))],
        out_shape=[jax.ShapeDtypeStruct((b, t, w), F32), jax.ShapeDtypeStruct((b, NSA_KV_HEADS, ns, t), F32)],
        compiler_params=_params("parallel", "parallel"),
        name="cmpsel",
    )(qb3, kcmp, vcmp, ovt)


def _selattn_kernel(q_ref, k_ref, vt_ref, sel_ref, o_ref, m_sc, l_sc, acc_sc, *, tq, tk):
    i = pl.program_id(1)
    t0 = i * tq
    g = NSA_HEADS // NSA_KV_HEADS
    rows = g * tq
    assert tk % tq == 0, "one key tile must cover the whole diagonal of a query tile"
    n_below = (i * tq) // tk
    nb = tk // SEL_BLOCK
    qpos = t0 + (lax.broadcasted_iota(I32, (tk, rows), 1) & (tq - 1))
    krow = lax.broadcasted_iota(I32, (tk, rows), 0)
    for hk in range(NSA_KV_HEADS):
        qt = jnp.concatenate(
            [jnp.transpose(q_ref[:, (hk * g + gi) * HEAD_DIM:(hk * g + gi + 1) * HEAD_DIM].astype(F32))
             for gi in range(g)], axis=1)
        qt = (qt * (HEAD_DIM ** -0.5)).astype(BF16)
        m_sc[...] = jnp.full(m_sc.shape, NEG_BIG, F32)
        l_sc[...] = jnp.zeros(l_sc.shape, F32)
        acc_sc[...] = jnp.zeros(acc_sc.shape, F32)

        def tile(kt, diagonal):
            ks = pl.multiple_of(kt * tk, tk)
            kk = k_ref[pl.ds(ks, tk), hk * HEAD_DIM:(hk + 1) * HEAD_DIM]
            s = jnp.dot(kk, qt, preferred_element_type=F32)
            bias = jnp.concatenate(
                [jnp.broadcast_to(
                    jnp.concatenate([(1.0 - sel_ref[hk, pl.ds(kt * nb + jj, 1), :]) * NEG_BIG] * g, axis=1),
                    (SEL_BLOCK, rows)) for jj in range(nb)], axis=0)
            s = s + bias
            if diagonal:
                s = jnp.where(ks + krow <= qpos, s, NEG_BIG)
            m_old = m_sc[...]
            m_new = jnp.maximum(m_old, jnp.max(s, axis=0, keepdims=True))
            alpha = jnp.exp(m_old - m_new)
            p = jnp.exp(s - m_new)
            l_sc[...] = alpha * l_sc[...] + jnp.sum(p, axis=0, keepdims=True)
            vt = vt_ref[kt, hk * HEAD_DIM:(hk + 1) * HEAD_DIM, :]
            acc_sc[...] = alpha * acc_sc[...] + jnp.dot(vt, p.astype(BF16), preferred_element_type=F32)
            m_sc[...] = m_new

        def below_diagonal(kt, carry):
            tile(kt, False)
            return carry

        lax.fori_loop(0, n_below, below_diagonal, 0)
        tile(n_below, True)
        o_t = acc_sc[...] / l_sc[...]
        for gi in range(g):
            h = hk * g + gi
            o_ref[:, h * HEAD_DIM:(h + 1) * HEAD_DIM] = jnp.transpose(o_t[:, gi * tq:(gi + 1) * tq])


def _selattn(qb3, ks3, vst4, sel4, tq, tk):
    b, t, w = qb3.shape
    g = NSA_HEADS // NSA_KV_HEADS
    ns = sel4.shape[2]
    kern = functools.partial(_selattn_kernel, tq=tq, tk=tk)
    return pl.pallas_call(
        kern,
        grid=(b, t // tq),
        in_specs=[pl.BlockSpec((None, tq, w), lambda bi, i: (bi, i, 0)),
                  pl.BlockSpec((None, t, LANES), lambda bi, i: (bi, 0, 0)),
                  pl.BlockSpec((None, t // tk, LANES, tk), lambda bi, i: (bi, 0, 0, 0)),
                  pl.BlockSpec((None, NSA_KV_HEADS, ns, tq), lambda bi, i: (bi, 0, 0, i))],
        out_specs=pl.BlockSpec((None, tq, w), lambda bi, i: (bi, i, 0)),
        out_shape=jax.ShapeDtypeStruct((b, t, w), F32),
        scratch_shapes=[pltpu.VMEM((1, g * tq), F32), pltpu.VMEM((1, g * tq), F32),
                        pltpu.VMEM((HEAD_DIM, g * tq), F32)],
        compiler_params=_params("parallel", "arbitrary"),
        name="selattn",
    )(qb3, ks3, vst4, sel4)


def _band_kernel(*refs, window, tq, span, n_heads, n_kv, has_sink):
    if has_sink:
        sink_ref, q_ref, k_ref, vt_ref, o_ref = refs
    else:
        q_ref, k_ref, vt_ref, o_ref = refs
    i = pl.program_id(1)
    t0 = i * tq
    start = pl.multiple_of(jnp.maximum(t0 + tq - span, 0), tq)
    tile0 = start // LANES
    g = n_heads // n_kv
    rows = g * tq
    qpos = t0 + (lax.broadcasted_iota(I32, (span, rows), 1) & (tq - 1))
    kpos = start + lax.broadcasted_iota(I32, (span, rows), 0)
    diff = qpos - kpos
    mask = (diff >= 0) & (diff < window)
    for hk in range(n_kv):
        kk = k_ref[pl.ds(start, span), hk * HEAD_DIM:(hk + 1) * HEAD_DIM]
        qt = jnp.concatenate(
            [jnp.transpose(q_ref[:, (hk * g + gi) * HEAD_DIM:(hk * g + gi + 1) * HEAD_DIM].astype(F32))
             for gi in range(g)], axis=1)
        qt = (qt * (HEAD_DIM ** -0.5)).astype(BF16)
        s = jnp.where(mask, jnp.dot(kk, qt, preferred_element_type=F32), NEG_BIG)
        m = jnp.max(s, axis=0, keepdims=True)
        if has_sink:
            sink = jnp.concatenate([jnp.full((1, tq), sink_ref[hk * g + gi], F32) for gi in range(g)], axis=1)
            m = jnp.maximum(m, sink)
        p = jnp.exp(s - m)
        l = jnp.sum(p, axis=0, keepdims=True)
        if has_sink:
            l = l + jnp.exp(sink - m)
        vt = jnp.concatenate([vt_ref[tile0 + j, hk * HEAD_DIM:(hk + 1) * HEAD_DIM, :]
                              for j in range(span // LANES)], axis=1)
        o_t = jnp.dot(vt, p.astype(BF16), preferred_element_type=F32) / l
        for gi in range(g):
            h = hk * g + gi
            o_ref[:, h * HEAD_DIM:(h + 1) * HEAD_DIM] = jnp.transpose(o_t[:, gi * tq:(gi + 1) * tq])


def _band(q3, k3, v3, sinks, window, tq, n_heads, n_kv):
    b, t, w = q3.shape
    nprev = -(-(window - 1) // tq)
    span = min((nprev + 1) * tq, t)
    has_sink = sinks is not None
    vt4 = jnp.swapaxes(v3.reshape(b, t // LANES, LANES, LANES), 2, 3)
    kern = functools.partial(_band_kernel, window=window, tq=tq, span=span, n_heads=n_heads, n_kv=n_kv,
                             has_sink=has_sink)
    in_specs = [pl.BlockSpec((None, tq, w), lambda bi, i: (bi, i, 0)),
                pl.BlockSpec((None, t, LANES), lambda bi, i: (bi, 0, 0)),
                pl.BlockSpec((None, t // LANES, LANES, LANES), lambda bi, i: (bi, 0, 0, 0))]
    args = [q3, k3, vt4]
    if has_sink:
        in_specs = [pl.BlockSpec(memory_space=pltpu.SMEM)] + in_specs
        args = [sinks] + args
    return pl.pallas_call(
        kern,
        grid=(b, t // tq),
        in_specs=in_specs,
        out_specs=pl.BlockSpec((None, tq, w), lambda bi, i: (bi, i, 0)),
        out_shape=jax.ShapeDtypeStruct((b, t, w), F32),
        compiler_params=_params("parallel", "parallel"),
        name="band_sink" if has_sink else "band_win",
    )(*args)


def _first_max_rows(val, payload=None):
    rows, n = val.shape
    sub = lax.broadcasted_iota(I32, (SUBLANES, n), 0).astype(F32)
    parts = [[val[r:r + SUBLANES] for r in range(0, rows, SUBLANES)],
             [sub + float(r) for r in range(0, rows, SUBLANES)]]
    if payload is not None:
        parts.append([payload[r:r + SUBLANES] for r in range(0, rows, SUBLANES)])
    while len(parts[0]) > 1:
        nxt = [[] for _ in parts]
        for j in range(0, len(parts[0]) - 1, 2):
            take_right = parts[0][j + 1] > parts[0][j]
            for dst, src in zip(nxt, parts):
                dst.append(jnp.where(take_right, src[j + 1], src[j]))
        if len(parts[0]) % 2:
            for dst, src in zip(nxt, parts):
                dst.append(src[-1])
        parts = nxt
    v8, i8 = parts[0][0], parts[1][0]
    m = jnp.max(v8, axis=0, keepdims=True)
    first = jnp.min(jnp.where(v8 == m, i8, float(rows)), axis=0, keepdims=True)
    if payload is None:
        return m, first
    return m, first, jnp.max(jnp.where(i8 == first, parts[2][0], -1.0), axis=0, keepdims=True)


def _topk_rows(s, k, ridx):
    vals, idxs = [], []
    for _ in range(k):
        m, first = _first_max_rows(s)
        vals.append(m)
        idxs.append(first)
        s = jnp.where(ridx == first, -jnp.inf, s)
    return jnp.concatenate(vals, axis=0), jnp.concatenate(idxs, axis=0)


def _peer_candidates(t1, i1, t2, i2):
    vals = [t1[0:1] + t2]
    eids = [i1[0:1] * float(N_KEYS) + i2]
    for a in range(1, 8):
        vals.append(t1[a:a + 1] + t2[0:8])
        eids.append(i1[a:a + 1] * float(N_KEYS) + i2[0:8])
    vals.append(t1[8:16] + t2[0:1])
    eids.append(i1[8:16] * float(N_KEYS) + i2[0:1])
    return jnp.concatenate(vals, axis=0), jnp.concatenate(eids, axis=0)


def _route_kernel(x_ref, oa_ref, oc_ref, os_ref, ow_ref, gt_ref, gexp_ref, wo_ref, gf_ref, wq_ref,
                  k1_ref, k2_ref, x1_ref, eidx_ref, gate_ref):
    tm = x_ref.shape[0]
    gt = gt_ref[...]
    g_hi = gt.astype(BF16)
    g_lo = (gt - g_hi.astype(F32)).astype(BF16)
    ob = jnp.zeros(oc_ref.shape, F32)
    for j, br in enumerate((oc_ref, os_ref, ow_ref)):
        ex = gexp_ref[j]
        gj = (jnp.dot(g_hi, ex, preferred_element_type=F32) + jnp.dot(g_lo, ex, preferred_element_type=F32))
        ob = ob + gj * br[...]
    half = oa_ref.shape[1]
    mixed = (jnp.dot(oa_ref[...].astype(BF16), wo_ref[0:half, :], preferred_element_type=F32)
             + jnp.dot(ob.astype(BF16), wo_ref[half:, :], preferred_element_type=F32))
    x1 = x_ref[...] + mixed
    x1_ref[...] = x1
    h2 = x1 * lax.rsqrt(jnp.mean(x1 * x1, axis=-1, keepdims=True) + RMS_EPS) * gf_ref[...]
    q = jnp.dot(h2.astype(BF16), wq_ref[...], preferred_element_type=F32)
    hq = PEER_QDIM // 2
    ridx = lax.broadcasted_iota(I32, (N_KEYS, tm), 0).astype(F32)
    e_rows, g_rows = [], []
    for h in range(PEER_HEADS):
        q1 = q[:, h * PEER_QDIM: h * PEER_QDIM + hq].astype(BF16)
        q2 = q[:, h * PEER_QDIM + hq:(h + 1) * PEER_QDIM].astype(BF16)
        s1 = lax.dot_general(k1_ref[...], q1, (((1,), (1,)), ((), ())), preferred_element_type=F32)
        s2 = lax.dot_general(k2_ref[...], q2, (((1,), (1,)), ((), ())), preferred_element_type=F32)
        t1, i1 = _topk_rows(s1, PEER_TOPK, ridx)
        t2, i2 = _topk_rows(s2, PEER_TOPK, ridx)
        cand, cand_e = _peer_candidates(t1, i1, t2, i2)
        cidx = lax.broadcasted_iota(I32, cand.shape, 0).astype(F32)
        sc, ee = [], []
        for _ in range(PEER_TOPK):
            m, first, e_first = _first_max_rows(cand, cand_e)
            sc.append(m)
            ee.append(e_first)
            cand = jnp.where(cidx == first, -jnp.inf, cand)
        sc = jnp.concatenate(sc, axis=0)
        ex = jnp.exp(sc - sc[0:1])
        g_rows.append(ex / jnp.sum(ex, axis=0, keepdims=True))
        e_rows.append(jnp.concatenate(ee, axis=0))
    e_t = jnp.concatenate(e_rows, axis=0)
    g_t = jnp.concatenate(g_rows, axis=0)
    eidx_ref[...] = jnp.transpose(e_t).astype(I32)
    gate_ref[...] = jnp.transpose(g_t)


def _route(x2, oa, oc, osel, ow, gates, gexp, wo, gf, wq, k1, k2, tm):
    n, d = x2.shape
    row = lambda w: pl.BlockSpec((tm, w), lambda i: (i, 0))
    full = lambda a: pl.BlockSpec(a.shape, lambda i: (0,) * a.ndim)
    nk = PEER_HEADS * PEER_TOPK
    return pl.pallas_call(
        _route_kernel,
        grid=(n // tm,),
        in_specs=[row(d), row(oa.shape[1]), row(oc.shape[1]), row(osel.shape[1]), row(ow.shape[1]), row(LANES),
                  full(gexp), full(wo), full(gf), full(wq), full(k1), full(k2)],
        out_specs=[row(d), row(nk), row(nk)],
        out_shape=[jax.ShapeDtypeStruct((n, d), F32), jax.ShapeDtypeStruct((n, nk), I32),
                   jax.ShapeDtypeStruct((n, nk), F32)],
        compiler_params=_params("parallel"),
        name="route",
    )(x2, oa, oc, osel, ow, gates, gexp, wo, gf, wq, k1, k2)


def _fold8(p):
    sub = lax.broadcasted_iota(I32, (SUBLANES, LANES), 0)
    lo4 = sub < 4
    q = [jnp.where(lo4, p[j], p[j + 4]) + pltpu.roll(jnp.where(lo4, p[j + 4], p[j]), 4, 0) for j in range(4)]
    m2 = (sub & 3) < 2
    r = [jnp.where(m2, q[j] + pltpu.roll(q[j], 6, 0), q[j + 2] + pltpu.roll(q[j + 2], 2, 0)) for j in range(2)]
    m1 = (sub & 1) == 0
    return jnp.where(m1, r[0] + pltpu.roll(r[0], 7, 0), r[1] + pltpu.roll(r[1], 1, 0))


def _peer_kernel(e0_ref, e1_ref, e2_ref, x_ref, gate_ref, gffn_ref, gfin_ref, tab_ref, o_ref,
                 buf, sem, a_sc, cb_sc, *, tb):
    step = pl.program_id(0)
    nsteps = pl.num_programs(0)
    slot = step % PEER_SLOTS
    fill = (step + PEER_SLOTS - 1) % PEER_SLOTS
    nk = PEER_HEADS * PEER_TOPK
    npair = tb * nk

    def issue_rows(eref, dst_slot, t, k0, k1):
        erow = eref.at[t]
        for k in range(k0, k1):
            copy = pltpu.make_async_copy(tab_ref.at[erow[k]], buf.at[dst_slot, t * nk + k], sem.at[dst_slot])
            copy.start(priority=k % 2)

    def wait_slot(s):
        pltpu.make_async_copy(tab_ref.at[pl.ds(0, npair)], buf.at[s], sem.at[s]).wait()

    @pl.when(step == 0)
    def _():
        def first(t, carry):
            issue_rows(e0_ref, 0, t, 0, nk)
            return carry
        lax.fori_loop(0, tb, first, 0)
        def second(t, carry):
            issue_rows(e1_ref, 1, t, 0, nk)
            return carry
        lax.fori_loop(0, tb, second, 0)

    eye = jnp.where(lax.broadcasted_iota(I32, (nk, nk), 0) == lax.broadcasted_iota(I32, (nk, nk), 1),
                    1.0, 0.0).astype(BF16)
    gr = gate_ref[...]
    g0 = gr.astype(BF16)
    r1 = gr - g0.astype(F32)
    g1 = r1.astype(BF16)
    g2 = (r1 - g1.astype(F32)).astype(BF16)
    dn = (((1,), (1,)), ((), ()))
    gate_t = (lax.dot_general(eye, g0, dn, preferred_element_type=F32)
              + lax.dot_general(eye, g1, dn, preferred_element_type=F32)
              + lax.dot_general(eye, g2, dn, preferred_element_type=F32))

    wait_slot(slot)

    lane_tb = lax.broadcasted_iota(I32, (nk, tb), 1)
    inv_d = 1.0 / (SUBLANES * LANES)
    a_sc[...] = jnp.zeros(a_sc.shape, F32)

    ngroup = nk // SUBLANES
    rows_p1 = 6
    rows_p2 = SUBLANES - rows_p1

    def dots(t, carry):
        x8 = x_ref[t]
        ms = jnp.sum(jnp.sum(x8 * x8, axis=1, keepdims=True), axis=0, keepdims=True) * inv_d
        h8 = x8 * lax.rsqrt(ms + RMS_EPS) * gffn_ref[...]
        base = t * nk
        folded = []
        for gi in range(nk // SUBLANES):
            blk = buf[slot, pl.ds(base + gi * SUBLANES, SUBLANES), 0:SUBLANES, :]
            folded.append(_fold8([blk[j] * h8 for j in range(SUBLANES)]))
            issue_rows(e2_ref, fill, t, gi * rows_p1, (gi + 1) * rows_p1)
        a = jnp.sum(jnp.concatenate(folded, axis=0), axis=1, keepdims=True)
        a_sc[...] = jnp.where(lane_tb == t, a, a_sc[...])
        return carry

    lax.fori_loop(0, tb, dots, 0, unroll=2)

    a_sc[...] = gate_t * _gelu(a_sc[...])

    def combine(t, carry):
        base = t * nk
        w_col = jnp.sum(jnp.where(lane_tb == t, a_sc[...], 0.0), axis=1, keepdims=True)
        cb_sc[t] = jnp.broadcast_to(w_col, (nk, LANES))
        accs = [jnp.zeros((SUBLANES, LANES), F32) for _ in range(4)]
        for gi in range(ngroup):
            for k in range(gi * SUBLANES, (gi + 1) * SUBLANES):
                ck = jnp.broadcast_to(cb_sc[t, k:k + 1, :], (SUBLANES, LANES))
                accs[k % 4] = accs[k % 4] + ck * buf[slot, base + k, SUBLANES:2 * SUBLANES, :]
            issue_rows(e2_ref, fill, t, ngroup * rows_p1 + gi * rows_p2, ngroup * rows_p1 + (gi + 1) * rows_p2)
        y8 = x_ref[t] + ((accs[0] + accs[1]) + (accs[2] + accs[3]))
        ms2 = jnp.sum(jnp.sum(y8 * y8, axis=1, keepdims=True), axis=0, keepdims=True) * inv_d
        o_ref[t] = y8 * lax.rsqrt(ms2 + RMS_EPS) * gfin_ref[...]
        return carry

    lax.fori_loop(0, tb, combine, 0, unroll=4)

    @pl.when(step == nsteps - 1)
    def _():
        for ahead in range(1, PEER_SLOTS):
            wait_slot((step + ahead) % PEER_SLOTS)


def _peer(eidx, x1s, gate, gffn8, gfin8, table, tb):
    n = x1s.shape[0]
    nk = PEER_HEADS * PEER_TOPK
    nsteps = n // tb
    kern = functools.partial(_peer_kernel, tb=tb)
    return pl.pallas_call(
        kern,
        grid=(nsteps,),
        in_specs=[pl.BlockSpec((tb, nk), lambda i: (i, 0), memory_space=pltpu.SMEM),
                  pl.BlockSpec((tb, nk), lambda i: ((i + 1) % nsteps, 0), memory_space=pltpu.SMEM),
                  pl.BlockSpec((tb, nk), lambda i: ((i + PEER_SLOTS - 1) % nsteps, 0), memory_space=pltpu.SMEM),
                  pl.BlockSpec((tb, SUBLANES, LANES), lambda i: (i, 0, 0)),
                  pl.BlockSpec((tb, nk), lambda i: (i, 0)),
                  pl.BlockSpec((SUBLANES, LANES), lambda i: (0, 0)),
                  pl.BlockSpec((SUBLANES, LANES), lambda i: (0, 0)),
                  pl.BlockSpec(memory_space=pl.ANY)],
        out_specs=pl.BlockSpec((tb, SUBLANES, LANES), lambda i: (i, 0, 0)),
        out_shape=jax.ShapeDtypeStruct((n, SUBLANES, LANES), F32),
        scratch_shapes=[pltpu.VMEM((PEER_SLOTS, tb * nk, 2 * SUBLANES, LANES), F32),
                        pltpu.SemaphoreType.DMA((PEER_SLOTS,)),
                        pltpu.VMEM((nk, tb), F32),
                        pltpu.VMEM((tb, nk, LANES), F32)],
        compiler_params=_params("arbitrary"),
        name="peer",
    )(eidx, eidx, eidx, x1s, gate, gffn8, gfin8, table)


def _overlap_t(t):
    nc = (t - CMP_LEN) // CMP_STRIDE + 1
    ns = t // SEL_BLOCK
    ncp = t // CMP_STRIDE
    cstart = np.arange(nc) * CMP_STRIDE
    bstart = np.arange(ns) * SEL_BLOCK
    lo = np.maximum(cstart[:, None], bstart[None, :])
    hi = np.minimum(cstart[:, None] + CMP_LEN, bstart[None, :] + SEL_BLOCK)
    ov = np.clip(hi - lo, 0, None).astype(np.float32) / CMP_LEN
    out = np.zeros((ns, ncp), np.float32)
    out[:, :nc] = ov.T
    return out, nc


def _gate_expand():
    ex = np.zeros((NSA_N_GATES, LANES, NSA_HEADS * HEAD_DIM), np.float32)
    for j in range(NSA_N_GATES):
        for h in range(NSA_HEADS):
            ex[j, h * NSA_N_GATES + j, h * HEAD_DIM:(h + 1) * HEAD_DIM] = 1.0
    return ex


def _layer(x, positions, norm_attn, w_in, attn_sinks, cmp_k_pos, cmp_k_w1, cmp_k_w2, cmp_v_pos, cmp_v_w1,
           cmp_v_w2, w_out, norm_ffn, w_query, sub_keys_1, sub_keys_2, expert_down, expert_up, norm_out,
           tm_in=512, tq_cmp=256, tq_sel=256, tk_sel=512, tq_band=128, tm_route=256, tb_peer=16):
    b, t, d = x.shape
    n = b * t
    x2 = x.reshape(n, d)
    half = HEAD_DIM // 2
    inv_freq = ROPE_THETA ** (-jnp.arange(half, dtype=F32) / half)
    invf = jnp.tile(inv_freq, LANES // half).reshape(1, LANES)
    w_pad = jnp.pad(w_in, ((0, 0), (0, IN_PAD - IN_WIDTH))).astype(BF16)
    pr = _inproj(x2, positions.reshape(n, 1), norm_attn.reshape(1, d), w_pad, invf, min(tm_in, n))
    r3 = lambda a: a.reshape(b, t, a.shape[-1])

    o_a = _band(r3(pr["qa"]), r3(pr["ka"]), r3(pr["va"]), attn_sinks.astype(F32), SWA_WINDOW, tq_band,
                SWA_HEADS, SWA_KV_HEADS)
    kcmp, vcmp = _compress(r3(pr["kc"]), r3(pr["vc"]),
                           cmp_k_pos.reshape(1, -1), cmp_k_w1.astype(BF16), cmp_k_w2.astype(BF16),
                           cmp_v_pos.reshape(1, -1), cmp_v_w1.astype(BF16), cmp_v_w2.astype(BF16))
    ovt, n_cmp = _overlap_t(t)
    qb3 = r3(pr["qb"])
    o_cmp, sel = _cmpsel(qb3, kcmp, vcmp, jnp.asarray(ovt), tq_cmp, n_cmp)
    tk_sel = min(tk_sel, t)
    vst4 = jnp.swapaxes(r3(pr["vs"]).reshape(b, t // tk_sel, tk_sel, LANES), 2, 3)
    o_sel = _selattn(qb3, r3(pr["ks"]), vst4, sel, tq_sel, tk_sel)
    o_win = _band(qb3, r3(pr["kw"]), r3(pr["vw"]), None, NSA_WINDOW, tq_band, NSA_HEADS, NSA_KV_HEADS)

    x1, eidx, gate = _route(x2, o_a.reshape(n, -1), o_cmp.reshape(n, -1), o_sel.reshape(n, -1),
                            o_win.reshape(n, -1), pr["gates"], jnp.asarray(_gate_expand(), BF16),
                            w_out.astype(BF16), norm_ffn.reshape(1, d), w_query.astype(BF16),
                            sub_keys_1.astype(BF16), sub_keys_2.astype(BF16), min(tm_route, n))
    ne = expert_down.shape[0]
    table = jnp.concatenate([expert_down.reshape(ne, SUBLANES, LANES), expert_up.reshape(ne, SUBLANES, LANES)],
                            axis=1)
    y = _peer(eidx, x1.reshape(n, SUBLANES, LANES), gate, norm_ffn.reshape(SUBLANES, LANES),
              norm_out.reshape(SUBLANES, LANES), table, tb_peer)
    return y.reshape(b, t, d)


def kernel(x, positions, norm_attn, w_in, attn_sinks, cmp_k_pos, cmp_k_w1, cmp_k_w2, cmp_v_pos, cmp_v_w1,
           cmp_v_w2, w_out, norm_ffn, peer_w_query, peer_sub_keys_1, peer_sub_keys_2, peer_expert_down,
           peer_expert_up, norm_f):
    assert norm_attn.shape[0] == 1, "single-layer block"
    return _layer(x, positions, norm_attn[0], w_in[0], attn_sinks[0], cmp_k_pos[0], cmp_k_w1[0], cmp_k_w2[0],
                  cmp_v_pos[0], cmp_v_w1[0], cmp_v_w2[0], w_out[0], norm_ffn[0], peer_w_query[0],
                  peer_sub_keys_1[0], peer_sub_keys_2[0], peer_expert_down[0], peer_expert_up[0], norm_f)
```

```python
import functools

import numpy as np
import jax
import jax.numpy as jnp
from jax import lax
from jax.experimental import pallas as pl
from jax.experimental.pallas import tpu as pltpu

F32 = jnp.float32
BF16 = jnp.bfloat16
I32 = jnp.int32

HEAD_DIM = 64
ROPE_THETA = 10000.0
RMS_EPS = 1e-6
NEG_BIG = -1e30

SWA_HEADS = 8
SWA_KV_HEADS = 2
SWA_WINDOW = 128

NSA_HEADS = 8
NSA_KV_HEADS = 2
CMP_LEN = 32
CMP_STRIDE = 16
CMP_HIDDEN = 256
SEL_BLOCK = 64
SEL_TOPN = 16
NSA_WINDOW = 512
NSA_N_GATES = 3

PEER_HEADS = 8
N_KEYS = 128
PEER_QDIM = 256
PEER_TOPK = 16
PEER_SLOTS = 3

LANES = 128
SUBLANES = 8
VMEM_LIMIT = 56 * 1024 * 1024

_IN_SPLITS = (("qa", 512), ("ka", 128), ("va", 128), ("qb", 512), ("kc", 128), ("vc", 128),
              ("ks", 128), ("vs", 128), ("kw", 128), ("vw", 128), ("gl", 24))
IN_WIDTH = sum(w for _, w in _IN_SPLITS)
IN_PAD = -(-IN_WIDTH // LANES) * LANES
_IN_OFF = {}
_o = 0
for _n, _w in _IN_SPLITS:
    _IN_OFF[_n] = _o
    _o += _w
_ROPED = ("qa", "ka", "qb", "kc", "ks", "kw")


def _gelu(x):
    return 0.5 * x * (1.0 + lax.erf(x * 0.7071067811865476))


def _params(*sem):
    return pltpu.CompilerParams(dimension_semantics=sem, vmem_limit_bytes=VMEM_LIMIT)


def _inproj_kernel(x_ref, pos_ref, g_ref, w_ref, invf_ref,
                   qa_ref, ka_ref, va_ref, qb_ref, kc_ref, vc_ref, ks_ref, vs_ref, kw_ref, vw_ref, gt_ref):
    x = x_ref[...]
    h = x * lax.rsqrt(jnp.mean(x * x, axis=-1, keepdims=True) + RMS_EPS) * g_ref[...]
    proj = jnp.dot(h.astype(BF16), w_ref[...], preferred_element_type=F32)
    ang = pos_ref[...].astype(F32) * invf_ref[...]
    lane = lax.broadcasted_iota(I32, ang.shape, 1)
    lo = (lane & (HEAD_DIM - 1)) < (HEAD_DIM // 2)
    cos = jnp.cos(ang)
    sin = jnp.sin(ang)
    sin_s = jnp.where(lo, -sin, sin)
    outs = {"qa": qa_ref, "ka": ka_ref, "va": va_ref, "qb": qb_ref, "kc": kc_ref, "vc": vc_ref,
            "ks": ks_ref, "vs": vs_ref, "kw": kw_ref, "vw": vw_ref}
    for name, width in _IN_SPLITS[:-1]:
        off = _IN_OFF[name]
        ref = outs[name]
        for i in range(width // LANES):
            v = proj[:, off + LANES * i: off + LANES * (i + 1)]
            if name in _ROPED:
                rot = jnp.where(lo, pltpu.roll(v, LANES - HEAD_DIM // 2, 1), pltpu.roll(v, HEAD_DIM // 2, 1))
                v = v * cos + rot * sin_s
            ref[:, LANES * i: LANES * (i + 1)] = v.astype(ref.dtype)
    gl = proj[:, _IN_OFF["gl"]: _IN_OFF["gl"] + LANES]
    gt_ref[...] = jax.nn.sigmoid(gl)


def _inproj(x2, pos2, g, w_pad, invf, tm):
    n, d = x2.shape
    row = lambda w: pl.BlockSpec((tm, w), lambda i: (i, 0))
    full = lambda a: pl.BlockSpec(a.shape, lambda i: (0,) * a.ndim)
    out_dt = {"qa": BF16, "ka": BF16, "va": BF16, "qb": BF16, "kc": F32, "vc": F32,
              "ks": BF16, "vs": BF16, "kw": BF16, "vw": BF16}
    names = [nm for nm, _ in _IN_SPLITS[:-1]]
    widths = dict(_IN_SPLITS)
    out_shape = [jax.ShapeDtypeStruct((n, widths[nm]), out_dt[nm]) for nm in names]
    out_shape.append(jax.ShapeDtypeStruct((n, LANES), F32))
    out_specs = [row(widths[nm]) for nm in names] + [row(LANES)]
    res = pl.pallas_call(
        _inproj_kernel,
        grid=(n // tm,),
        in_specs=[row(d), row(1), full(g), full(w_pad), full(invf)],
        out_specs=out_specs,
        out_shape=out_shape,
        compiler_params=_params("parallel"),
        name="inproj",
    )(x2, pos2, g, w_pad, invf)
    out = dict(zip(names, res[:-1]))
    out["gates"] = res[-1]
    return out


def _compress_kernel(kc_ref, vc_ref, kpos_ref, kw1_ref, kw2_ref, vpos_ref, vw1_ref, vw2_ref, ko_ref, vo_ref):
    ncp = ko_ref.shape[0]
    for src, pos_ref, w1_ref, w2_ref, out_ref in ((kc_ref, kpos_ref, kw1_ref, kw2_ref, ko_ref),
                                                  (vc_ref, vpos_ref, vw1_ref, vw2_ref, vo_ref)):
        bias = jnp.dot(pos_ref[...].astype(BF16), w1_ref[...], preferred_element_type=F32)
        acc_a = [jnp.zeros((ncp, CMP_HIDDEN), F32) for _ in range(2)]
        acc_b = [jnp.zeros((ncp, CMP_HIDDEN), F32) for _ in range(2)]
        for j in range(CMP_STRIDE):
            pj = src[pl.ds(j, ncp, stride=CMP_STRIDE), :].astype(BF16)
            wa = w1_ref[j * HEAD_DIM:(j + 1) * HEAD_DIM, :]
            wb = w1_ref[(CMP_STRIDE + j) * HEAD_DIM:(CMP_STRIDE + j + 1) * HEAD_DIM, :]
            for hh in range(2):
                ph = pj[:, hh * HEAD_DIM:(hh + 1) * HEAD_DIM]
                acc_a[hh] = acc_a[hh] + jnp.dot(ph, wa, preferred_element_type=F32)
                acc_b[hh] = acc_b[hh] + jnp.dot(ph, wb, preferred_element_type=F32)
        for hh in range(2):
            h1 = acc_a[hh] + pltpu.roll(acc_b[hh], ncp - 1, 0) + bias
            out = jnp.dot(_gelu(h1).astype(BF16), w2_ref[...], preferred_element_type=F32)
            out_ref[:, hh * HEAD_DIM:(hh + 1) * HEAD_DIM] = out


def _compress(kc3, vc3, kpos, kw1, kw2, vpos, vw1, vw2):
    b, t, w = kc3.shape
    ncp = t // CMP_STRIDE
    per_b = pl.BlockSpec((None, t, w), lambda i: (i, 0, 0))
    full = lambda a: pl.BlockSpec(a.shape, lambda i: (0,) * a.ndim)
    out_spec = pl.BlockSpec((None, ncp, w), lambda i: (i, 0, 0))
    return pl.pallas_call(
        _compress_kernel,
        grid=(b,),
        in_specs=[per_b, per_b, full(kpos), full(kw1), full(kw2), full(vpos), full(vw1), full(vw2)],
        out_specs=[out_spec, out_spec],
        out_shape=[jax.ShapeDtypeStruct((b, ncp, w), F32)] * 2,
        compiler_params=_params("parallel"),
        name="compress",
    )(kc3, vc3, kpos, kw1, kw2, vpos, vw1, vw2)


def _cmpsel_kernel(q_ref, kc_ref, vc_ref, ovt_ref, o_ref, sel_ref, *, tq, n_cmp):
    i = pl.program_id(1)
    t0 = i * tq
    g = NSA_HEADS // NSA_KV_HEADS
    ncp = kc_ref.shape[0]
    ns = ovt_ref.shape[0]
    rows = g * tq
    tpos = t0 + (lax.broadcasted_iota(I32, (rows, ncp), 0) & (tq - 1))
    cidx = lax.broadcasted_iota(I32, (rows, ncp), 1)
    vis = (cidx * CMP_STRIDE + (CMP_LEN - 1) <= tpos) & (cidx < n_cmp)
    jidx = lax.broadcasted_iota(I32, (ns, tq), 0)
    tcol = t0 + lax.broadcasted_iota(I32, (ns, tq), 1)
    cur = tcol // SEL_BLOCK
    forced = (jidx == 0) | (jidx == cur) | (jidx == cur - 1)
    valid = jidx * SEL_BLOCK <= tcol
    for hk in range(NSA_KV_HEADS):
        kk = kc_ref[:, hk * HEAD_DIM:(hk + 1) * HEAD_DIM].astype(BF16)
        vv = vc_ref[:, hk * HEAD_DIM:(hk + 1) * HEAD_DIM].astype(BF16)
        q4 = jnp.concatenate(
            [q_ref[:, (hk * g + gi) * HEAD_DIM:(hk * g + gi + 1) * HEAD_DIM] for gi in range(g)], axis=0)
        s = lax.dot_general(q4, kk, (((1,), (1,)), ((), ())), preferred_element_type=F32) * (HEAD_DIM ** -0.5)
        s = jnp.where(vis, s, NEG_BIG)
        m = jnp.max(s, axis=-1, keepdims=True)
        e = jnp.exp(s - m)
        p = jnp.where(vis, e / jnp.sum(e, axis=-1, keepdims=True), 0.0)
        o = jnp.dot(p.astype(BF16), vv, preferred_element_type=F32)
        for gi in range(g):
            h = hk * g + gi
            o_ref[:, h * HEAD_DIM:(h + 1) * HEAD_DIM] = o[gi * tq:(gi + 1) * tq]
        psum = p[0:tq]
        for gi in range(1, g):
            psum = psum + p[gi * tq:(gi + 1) * tq]
        imp_t = lax.dot_general(ovt_ref[...], psum, (((1,), (1,)), ((), ())),
                                precision=lax.Precision.HIGHEST, preferred_element_type=F32)
        score = jnp.where(forced, jnp.inf, jnp.where(valid, imp_t, -jnp.inf))
        cnt = jnp.zeros((ns, tq), F32)
        for r in range(ns):
            row = score[r:r + 1, :]
            beats = (row > score) | ((row >= score) & (jidx > r))
            cnt = cnt + jnp.where(beats, 1.0, 0.0)
        sel_ref[hk] = jnp.where((cnt < SEL_TOPN) & (score > -jnp.inf), 1.0, 0.0)


def _cmpsel(qb3, kcmp, vcmp, ovt, tq, n_cmp):
    b, t, w = qb3.shape
    ncp = kcmp.shape[1]
    ns = ovt.shape[0]
    kern = functools.partial(_cmpsel_kernel, tq=tq, n_cmp=n_cmp)
    return pl.pallas_call(
        kern,
        grid=(b, t // tq),
        in_specs=[pl.BlockSpec((None, tq, w), lambda bi, i: (bi, i, 0)),
                  pl.BlockSpec((None, ncp, LANES), lambda bi, i: (bi, 0, 0)),
                  pl.BlockSpec((None, ncp, LANES), lambda bi, i: (bi, 0, 0)),
                  pl.BlockSpec(ovt.shape, lambda bi, i: (0, 0))],
        out_specs=[pl.BlockSpec((None, tq, w), lambda bi, i: (bi, i, 0)),
                   pl.BlockSpec((None, NSA_KV_HEADS, ns, tq), lambda bi, i: (bi, 0, 0, i))],
        out_shape=[jax.ShapeDtypeStruct((b, t, w), F32), jax.ShapeDtypeStruct((b, NSA_KV_HEADS, ns, t), F32)],
        compiler_params=_params("parallel", "parallel"),
        name="cmpsel",
    )(qb3, kcmp, vcmp, ovt)


def _selattn_kernel(q_ref, k_ref, vt_ref, sel_ref, o_ref, m_sc, l_sc, acc_sc, *, tq, tk):
    i = pl.program_id(1)
    t0 = i * tq
    g = NSA_HEADS // NSA_KV_HEADS
    rows = g * tq
    assert tk % tq == 0, "one key tile must cover the whole diagonal of a query tile"
    n_below = (i * tq) // tk
    nb = tk // SEL_BLOCK
    qpos = t0 + (lax.broadcasted_iota(I32, (tk, rows), 1) & (tq - 1))
    krow = lax.broadcasted_iota(I32, (tk, rows), 0)
    for hk in range(NSA_KV_HEADS):
        qt = jnp.concatenate(
            [jnp.transpose(q_ref[:, (hk * g + gi) * HEAD_DIM:(hk * g + gi + 1) * HEAD_DIM].astype(F32))
             for gi in range(g)], axis=1)
        qt = (qt * (HEAD_DIM ** -0.5)).astype(BF16)
        m_sc[...] = jnp.full(m_sc.shape, NEG_BIG, F32)
        l_sc[...] = jnp.zeros(l_sc.shape, F32)
        acc_sc[...] = jnp.zeros(acc_sc.shape, F32)

        def tile(kt, diagonal):
            ks = pl.multiple_of(kt * tk, tk)
            kk = k_ref[pl.ds(ks, tk), hk * HEAD_DIM:(hk + 1) * HEAD_DIM]
            s = jnp.dot(kk, qt, preferred_element_type=F32)
            bias = jnp.concatenate(
                [jnp.broadcast_to(
                    jnp.concatenate([(1.0 - sel_ref[hk, pl.ds(kt * nb + jj, 1), :]) * NEG_BIG] * g, axis=1),
                    (SEL_BLOCK, rows)) for jj in range(nb)], axis=0)
            s = s + bias
            if diagonal:
                s = jnp.where(ks + krow <= qpos, s, NEG_BIG)
            m_old = m_sc[...]
            m_new = jnp.maximum(m_old, jnp.max(s, axis=0, keepdims=True))
            alpha = jnp.exp(m_old - m_new)
            p = jnp.exp(s - m_new)
            l_sc[...] = alpha * l_sc[...] + jnp.sum(p, axis=0, keepdims=True)
            vt = vt_ref[kt, hk * HEAD_DIM:(hk + 1) * HEAD_DIM, :]
            acc_sc[...] = alpha * acc_sc[...] + jnp.dot(vt, p.astype(BF16), preferred_element_type=F32)
            m_sc[...] = m_new

        def below_diagonal(kt, carry):
            tile(kt, False)
            return carry

        lax.fori_loop(0, n_below, below_diagonal, 0)
        tile(n_below, True)
        o_t = acc_sc[...] / l_sc[...]
        for gi in range(g):
            h = hk * g + gi
            o_ref[:, h * HEAD_DIM:(h + 1) * HEAD_DIM] = jnp.transpose(o_t[:, gi * tq:(gi + 1) * tq])


def _selattn(qb3, ks3, vst4, sel4, tq, tk):
    b, t, w = qb3.shape
    g = NSA_HEADS // NSA_KV_HEADS
    ns = sel4.shape[2]
    kern = functools.partial(_selattn_kernel, tq=tq, tk=tk)
    return pl.pallas_call(
        kern,
        grid=(b, t // tq),
        in_specs=[pl.BlockSpec((None, tq, w), lambda bi, i: (bi, i, 0)),
                  pl.BlockSpec((None, t, LANES), lambda bi, i: (bi, 0, 0)),
                  pl.BlockSpec((None, t // tk, LANES, tk), lambda bi, i: (bi, 0, 0, 0)),
                  pl.BlockSpec((None, NSA_KV_HEADS, ns, tq), lambda bi, i: (bi, 0, 0, i))],
        out_specs=pl.BlockSpec((None, tq, w), lambda bi, i: (bi, i, 0)),
        out_shape=jax.ShapeDtypeStruct((b, t, w), F32),
        scratch_shapes=[pltpu.VMEM((1, g * tq), F32), pltpu.VMEM((1, g * tq), F32),
                        pltpu.VMEM((HEAD_DIM, g * tq), F32)],
        compiler_params=_params("parallel", "arbitrary"),
        name="selattn",
    )(qb3, ks3, vst4, sel4)


def _band_kernel(*refs, window, tq, span, n_heads, n_kv, has_sink):
    if has_sink:
        sink_ref, q_ref, k_ref, vt_ref, o_ref = refs
    else:
        q_ref, k_ref, vt_ref, o_ref = refs
    i = pl.program_id(1)
    t0 = i * tq
    start = pl.multiple_of(jnp.maximum(t0 + tq - span, 0), tq)
    tile0 = start // LANES
    g = n_heads // n_kv
    rows = g * tq
    qpos = t0 + (lax.broadcasted_iota(I32, (span, rows), 1) & (tq - 1))
    kpos = start + lax.broadcasted_iota(I32, (span, rows), 0)
    diff = qpos - kpos
    mask = (diff >= 0) & (diff < window)
    for hk in range(n_kv):
        kk = k_ref[pl.ds(start, span), hk * HEAD_DIM:(hk + 1) * HEAD_DIM]
        qt = jnp.concatenate(
            [jnp.transpose(q_ref[:, (hk * g + gi) * HEAD_DIM:(hk * g + gi + 1) * HEAD_DIM].astype(F32))
             for gi in range(g)], axis=1)
        qt = (qt * (HEAD_DIM ** -0.5)).astype(BF16)
        s = jnp.where(mask, jnp.dot(kk, qt, preferred_element_type=F32), NEG_BIG)
        m = jnp.max(s, axis=0, keepdims=True)
        if has_sink:
            sink = jnp.concatenate([jnp.full((1, tq), sink_ref[hk * g + gi], F32) for gi in range(g)], axis=1)
            m = jnp.maximum(m, sink)
        p = jnp.exp(s - m)
        l = jnp.sum(p, axis=0, keepdims=True)
        if has_sink:
            l = l + jnp.exp(sink - m)
        vt = jnp.concatenate([vt_ref[tile0 + j, hk * HEAD_DIM:(hk + 1) * HEAD_DIM, :]
                              for j in range(span // LANES)], axis=1)
        o_t = jnp.dot(vt, p.astype(BF16), preferred_element_type=F32) / l
        for gi in range(g):
            h = hk * g + gi
            o_ref[:, h * HEAD_DIM:(h + 1) * HEAD_DIM] = jnp.transpose(o_t[:, gi * tq:(gi + 1) * tq])


def _band(q3, k3, v3, sinks, window, tq, n_heads, n_kv):
    b, t, w = q3.shape
    nprev = -(-(window - 1) // tq)
    span = min((nprev + 1) * tq, t)
    has_sink = sinks is not None
    vt4 = jnp.swapaxes(v3.reshape(b, t // LANES, LANES, LANES), 2, 3)
    kern = functools.partial(_band_kernel, window=window, tq=tq, span=span, n_heads=n_heads, n_kv=n_kv,
                             has_sink=has_sink)
    in_specs = [pl.BlockSpec((None, tq, w), lambda bi, i: (bi, i, 0)),
                pl.BlockSpec((None, t, LANES), lambda bi, i: (bi, 0, 0)),
                pl.BlockSpec((None, t // LANES, LANES, LANES), lambda bi, i: (bi, 0, 0, 0))]
    args = [q3, k3, vt4]
    if has_sink:
        in_specs = [pl.BlockSpec(memory_space=pltpu.SMEM)] + in_specs
        args = [sinks] + args
    return pl.pallas_call(
        kern,
        grid=(b, t // tq),
        in_specs=in_specs,
        out_specs=pl.BlockSpec((None, tq, w), lambda bi, i: (bi, i, 0)),
        out_shape=jax.ShapeDtypeStruct((b, t, w), F32),
        compiler_params=_params("parallel", "parallel"),
        name="band_sink" if has_sink else "band_win",
    )(*args)


def _first_max_rows(val, payload=None):
    rows, n = val.shape
    sub = lax.broadcasted_iota(I32, (SUBLANES, n), 0).astype(F32)
    parts = [[val[r:r + SUBLANES] for r in range(0, rows, SUBLANES)],
             [sub + float(r) for r in range(0, rows, SUBLANES)]]
    if payload is not None:
        parts.append([payload[r:r + SUBLANES] for r in range(0, rows, SUBLANES)])
    while len(parts[0]) > 1:
        nxt = [[] for _ in parts]
        for j in range(0, len(parts[0]) - 1, 2):
            take_right = parts[0][j + 1] > parts[0][j]
            for dst, src in zip(nxt, parts):
                dst.append(jnp.where(take_right, src[j + 1], src[j]))
        if len(parts[0]) % 2:
            for dst, src in zip(nxt, parts):
                dst.append(src[-1])
        parts = nxt
    v8, i8 = parts[0][0], parts[1][0]
    m = jnp.max(v8, axis=0, keepdims=True)
    first = jnp.min(jnp.where(v8 == m, i8, float(rows)), axis=0, keepdims=True)
    if payload is None:
        return m, first
    return m, first, jnp.max(jnp.where(i8 == first, parts[2][0], -1.0), axis=0, keepdims=True)


def _topk_rows(s, k, ridx):
    vals, idxs = [], []
    for _ in range(k):
        m, first = _first_max_rows(s)
        vals.append(m)
        idxs.append(first)
        s = jnp.where(ridx == first, -jnp.inf, s)
    return jnp.concatenate(vals, axis=0), jnp.concatenate(idxs, axis=0)


def _peer_candidates(t1, i1, t2, i2):
    vals = [t1[0:1] + t2]
    eids = [i1[0:1] * float(N_KEYS) + i2]
    for a in range(1, 8):
        vals.append(t1[a:a + 1] + t2[0:8])
        eids.append(i1[a:a + 1] * float(N_KEYS) + i2[0:8])
    vals.append(t1[8:16] + t2[0:1])
    eids.append(i1[8:16] * float(N_KEYS) + i2[0:1])
    return jnp.concatenate(vals, axis=0), jnp.concatenate(eids, axis=0)


def _route_kernel(x_ref, oa_ref, oc_ref, os_ref, ow_ref, gt_ref, gexp_ref, wo_ref, gf_ref, wq_ref,
                  k1_ref, k2_ref, x1_ref, eidx_ref, gate_ref):
    tm = x_ref.shape[0]
    gt = gt_ref[...]
    g_hi = gt.astype(BF16)
    g_lo = (gt - g_hi.astype(F32)).astype(BF16)
    ob = jnp.zeros(oc_ref.shape, F32)
    for j, br in enumerate((oc_ref, os_ref, ow_ref)):
        ex = gexp_ref[j]
        gj = (jnp.dot(g_hi, ex, preferred_element_type=F32) + jnp.dot(g_lo, ex, preferred_element_type=F32))
        ob = ob + gj * br[...]
    half = oa_ref.shape[1]
    mixed = (jnp.dot(oa_ref[...].astype(BF16), wo_ref[0:half, :], preferred_element_type=F32)
             + jnp.dot(ob.astype(BF16), wo_ref[half:, :], preferred_element_type=F32))
    x1 = x_ref[...] + mixed
    x1_ref[...] = x1
    h2 = x1 * lax.rsqrt(jnp.mean(x1 * x1, axis=-1, keepdims=True) + RMS_EPS) * gf_ref[...]
    q = jnp.dot(h2.astype(BF16), wq_ref[...], preferred_element_type=F32)
    hq = PEER_QDIM // 2
    ridx = lax.broadcasted_iota(I32, (N_KEYS, tm), 0).astype(F32)
    e_rows, g_rows = [], []
    for h in range(PEER_HEADS):
        q1 = q[:, h * PEER_QDIM: h * PEER_QDIM + hq].astype(BF16)
        q2 = q[:, h * PEER_QDIM + hq:(h + 1) * PEER_QDIM].astype(BF16)
        s1 = lax.dot_general(k1_ref[...], q1, (((1,), (1,)), ((), ())), preferred_element_type=F32)
        s2 = lax.dot_general(k2_ref[...], q2, (((1,), (1,)), ((), ())), preferred_element_type=F32)
        t1, i1 = _topk_rows(s1, PEER_TOPK, ridx)
        t2, i2 = _topk_rows(s2, PEER_TOPK, ridx)
        cand, cand_e = _peer_candidates(t1, i1, t2, i2)
        cidx = lax.broadcasted_iota(I32, cand.shape, 0).astype(F32)
        sc, ee = [], []
        for _ in range(PEER_TOPK):
            m, first, e_first = _first_max_rows(cand, cand_e)
            sc.append(m)
            ee.append(e_first)
            cand = jnp.where(cidx == first, -jnp.inf, cand)
        sc = jnp.concatenate(sc, axis=0)
        ex = jnp.exp(sc - sc[0:1])
        g_rows.append(ex / jnp.sum(ex, axis=0, keepdims=True))
        e_rows.append(jnp.concatenate(ee, axis=0))
    e_t = jnp.concatenate(e_rows, axis=0)
    g_t = jnp.concatenate(g_rows, axis=0)
    eidx_ref[...] = jnp.transpose(e_t).astype(I32)
    gate_ref[...] = jnp.transpose(g_t)


def _route(x2, oa, oc, osel, ow, gates, gexp, wo, gf, wq, k1, k2, tm):
    n, d = x2.shape
    row = lambda w: pl.BlockSpec((tm, w), lambda i: (i, 0))
    full = lambda a: pl.BlockSpec(a.shape, lambda i: (0,) * a.ndim)
    nk = PEER_HEADS * PEER_TOPK
    return pl.pallas_call(
        _route_kernel,
        grid=(n // tm,),
        in_specs=[row(d), row(oa.shape[1]), row(oc.shape[1]), row(osel.shape[1]), row(ow.shape[1]), row(LANES),
                  full(gexp), full(wo), full(gf), full(wq), full(k1), full(k2)],
        out_specs=[row(d), row(nk), row(nk)],
        out_shape=[jax.ShapeDtypeStruct((n, d), F32), jax.ShapeDtypeStruct((n, nk), I32),
                   jax.ShapeDtypeStruct((n, nk), F32)],
        compiler_params=_params("parallel"),
        name="route",
    )(x2, oa, oc, osel, ow, gates, gexp, wo, gf, wq, k1, k2)


def _fold8(p):
    sub = lax.broadcasted_iota(I32, (SUBLANES, LANES), 0)
    lo4 = sub < 4
    q = [jnp.where(lo4, p[j], p[j + 4]) + pltpu.roll(jnp.where(lo4, p[j + 4], p[j]), 4, 0) for j in range(4)]
    m2 = (sub & 3) < 2
    r = [jnp.where(m2, q[j] + pltpu.roll(q[j], 6, 0), q[j + 2] + pltpu.roll(q[j + 2], 2, 0)) for j in range(2)]
    m1 = (sub & 1) == 0
    return jnp.where(m1, r[0] + pltpu.roll(r[0], 7, 0), r[1] + pltpu.roll(r[1], 1, 0))


def _peer_kernel(e0_ref, e1_ref, e2_ref, x_ref, gate_ref, gffn_ref, gfin_ref, tab_ref, o_ref,
                 buf, sem, a_sc, cb_sc, *, tb):
    step = pl.program_id(0)
    nsteps = pl.num_programs(0)
    slot = step % PEER_SLOTS
    fill = (step + PEER_SLOTS - 1) % PEER_SLOTS
    nk = PEER_HEADS * PEER_TOPK
    npair = tb * nk

    def issue_rows(eref, dst_slot, t, k0, k1):
        erow = eref.at[t]
        for k in range(k0, k1):
            copy = pltpu.make_async_copy(tab_ref.at[erow[k]], buf.at[dst_slot, t * nk + k], sem.at[dst_slot])
            copy.start(priority=k % 2)

    def wait_slot(s):
        pltpu.make_async_copy(tab_ref.at[pl.ds(0, npair)], buf.at[s], sem.at[s]).wait()

    @pl.when(step == 0)
    def _():
        def first(t, carry):
            issue_rows(e0_ref, 0, t, 0, nk)
            return carry
        lax.fori_loop(0, tb, first, 0)
        def second(t, carry):
            issue_rows(e1_ref, 1, t, 0, nk)
            return carry
        lax.fori_loop(0, tb, second, 0)

    eye = jnp.where(lax.broadcasted_iota(I32, (nk, nk), 0) == lax.broadcasted_iota(I32, (nk, nk), 1),
                    1.0, 0.0).astype(BF16)
    gr = gate_ref[...]
    g0 = gr.astype(BF16)
    r1 = gr - g0.astype(F32)
    g1 = r1.astype(BF16)
    g2 = (r1 - g1.astype(F32)).astype(BF16)
    dn = (((1,), (1,)), ((), ()))
    gate_t = (lax.dot_general(eye, g0, dn, preferred_element_type=F32)
              + lax.dot_general(eye, g1, dn, preferred_element_type=F32)
              + lax.dot_general(eye, g2, dn, preferred_element_type=F32))

    wait_slot(slot)

    lane_tb = lax.broadcasted_iota(I32, (nk, tb), 1)
    inv_d = 1.0 / (SUBLANES * LANES)
    a_sc[...] = jnp.zeros(a_sc.shape, F32)

    ngroup = nk // SUBLANES
    rows_p1 = 6
    rows_p2 = SUBLANES - rows_p1

    def dots(t, carry):
        x8 = x_ref[t]
        ms = jnp.sum(jnp.sum(x8 * x8, axis=1, keepdims=True), axis=0, keepdims=True) * inv_d
        h8 = x8 * lax.rsqrt(ms + RMS_EPS) * gffn_ref[...]
        base = t * nk
        folded = []
        for gi in range(nk // SUBLANES):
            blk = buf[slot, pl.ds(base + gi * SUBLANES, SUBLANES), 0:SUBLANES, :]
            folded.append(_fold8([blk[j] * h8 for j in range(SUBLANES)]))
            issue_rows(e2_ref, fill, t, gi * rows_p1, (gi + 1) * rows_p1)
        a = jnp.sum(jnp.concatenate(folded, axis=0), axis=1, keepdims=True)
        a_sc[...] = jnp.where(lane_tb == t, a, a_sc[...])
        return carry

    lax.fori_loop(0, tb, dots, 0, unroll=2)

    a_sc[...] = gate_t * _gelu(a_sc[...])

    def combine(t, carry):
        base = t * nk
        w_col = jnp.sum(jnp.where(lane_tb == t, a_sc[...], 0.0), axis=1, keepdims=True)
        cb_sc[t] = jnp.broadcast_to(w_col, (nk, LANES))
        accs = [jnp.zeros((SUBLANES, LANES), F32) for _ in range(4)]
        for gi in range(ngroup):
            for k in range(gi * SUBLANES, (gi + 1) * SUBLANES):
                ck = jnp.broadcast_to(cb_sc[t, k:k + 1, :], (SUBLANES, LANES))
                accs[k % 4] = accs[k % 4] + ck * buf[slot, base + k, SUBLANES:2 * SUBLANES, :]
            issue_rows(e2_ref, fill, t, ngroup * rows_p1 + gi * rows_p2, ngroup * rows_p1 + (gi + 1) * rows_p2)
        y8 = x_ref[t] + ((accs[0] + accs[1]) + (accs[2] + accs[3]))
        ms2 = jnp.sum(jnp.sum(y8 * y8, axis=1, keepdims=True), axis=0, keepdims=True) * inv_d
        o_ref[t] = y8 * lax.rsqrt(ms2 + RMS_EPS) * gfin_ref[...]
        return carry

    lax.fori_loop(0, tb, combine, 0, unroll=4)

    @pl.when(step == nsteps - 1)
    def _():
        for ahead in range(1, PEER_SLOTS):
            wait_slot((step + ahead) % PEER_SLOTS)


def _peer(eidx, x1s, gate, gffn8, gfin8, table, tb):
    n = x1s.shape[0]
    nk = PEER_HEADS * PEER_TOPK
    nsteps = n // tb
    kern = functools.partial(_peer_kernel, tb=tb)
    return pl.pallas_call(
        kern,
        grid=(nsteps,),
        in_specs=[pl.BlockSpec((tb, nk), lambda i: (i, 0), memory_space=pltpu.SMEM),
                  pl.BlockSpec((tb, nk), lambda i: ((i + 1) % nsteps, 0), memory_space=pltpu.SMEM),
                  pl.BlockSpec((tb, nk), lambda i: ((i + PEER_SLOTS - 1) % nsteps, 0), memory_space=pltpu.SMEM),
                  pl.BlockSpec((tb, SUBLANES, LANES), lambda i: (i, 0, 0)),
                  pl.BlockSpec((tb, nk), lambda i: (i, 0)),
                  pl.BlockSpec((SUBLANES, LANES), lambda i: (0, 0)),
                  pl.BlockSpec((SUBLANES, LANES), lambda i: (0, 0)),
                  pl.BlockSpec(memory_space=pl.ANY)],
        out_specs=pl.BlockSpec((tb, SUBLANES, LANES), lambda i: (i, 0, 0)),
        out_shape=jax.ShapeDtypeStruct((n, SUBLANES, LANES), F32),
        scratch_shapes=[pltpu.VMEM((PEER_SLOTS, tb * nk, 2 * SUBLANES, LANES), F32),
                        pltpu.SemaphoreType.DMA((PEER_SLOTS,)),
                        pltpu.VMEM((nk, tb), F32),
                        pltpu.VMEM((tb, nk, LANES), F32)],
        compiler_params=_params("arbitrary"),
        name="peer",
    )(eidx, eidx, eidx, x1s, gate, gffn8, gfin8, table)


def _overlap_t(t):
    nc = (t - CMP_LEN) // CMP_STRIDE + 1
    ns = t // SEL_BLOCK
    ncp = t // CMP_STRIDE
    cstart = np.arange(nc) * CMP_STRIDE
    bstart = np.arange(ns) * SEL_BLOCK
    lo = np.maximum(cstart[:, None], bstart[None, :])
    hi = np.minimum(cstart[:, None] + CMP_LEN, bstart[None, :] + SEL_BLOCK)
    ov = np.clip(hi - lo, 0, None).astype(np.float32) / CMP_LEN
    out = np.zeros((ns, ncp), np.float32)
    out[:, :nc] = ov.T
    return out, nc


def _gate_expand():
    ex = np.zeros((NSA_N_GATES, LANES, NSA_HEADS * HEAD_DIM), np.float32)
    for j in range(NSA_N_GATES):
        for h in range(NSA_HEADS):
            ex[j, h * NSA_N_GATES + j, h * HEAD_DIM:(h + 1) * HEAD_DIM] = 1.0
    return ex


def _layer(x, positions, norm_attn, w_in, attn_sinks, cmp_k_pos, cmp_k_w1, cmp_k_w2, cmp_v_pos, cmp_v_w1,
           cmp_v_w2, w_out, norm_ffn, w_query, sub_keys_1, sub_keys_2, expert_down, expert_up, norm_out,
           tm_in=512, tq_cmp=256, tq_sel=256, tk_sel=512, tq_swa=128, tq_win=256, tm_route=256, tb_peer=16):
    b, t, d = x.shape
    n = b * t
    x2 = x.reshape(n, d)
    half = HEAD_DIM // 2
    inv_freq = ROPE_THETA ** (-jnp.arange(half, dtype=F32) / half)
    invf = jnp.tile(inv_freq, LANES // half).reshape(1, LANES)
    w_pad = jnp.pad(w_in, ((0, 0), (0, IN_PAD - IN_WIDTH))).astype(BF16)
    pr = _inproj(x2, positions.reshape(n, 1), norm_attn.reshape(1, d), w_pad, invf, min(tm_in, n))
    r3 = lambda a: a.reshape(b, t, a.shape[-1])

    o_a = _band(r3(pr["qa"]), r3(pr["ka"]), r3(pr["va"]), attn_sinks.astype(F32), SWA_WINDOW, tq_swa,
                SWA_HEADS, SWA_KV_HEADS)
    kcmp, vcmp = _compress(r3(pr["kc"]), r3(pr["vc"]),
                           cmp_k_pos.reshape(1, -1), cmp_k_w1.astype(BF16), cmp_k_w2.astype(BF16),
                           cmp_v_pos.reshape(1, -1), cmp_v_w1.astype(BF16), cmp_v_w2.astype(BF16))
    ovt, n_cmp = _overlap_t(t)
    qb3 = r3(pr["qb"])
    o_cmp, sel = _cmpsel(qb3, kcmp, vcmp, jnp.asarray(ovt), tq_cmp, n_cmp)
    tk_sel = min(tk_sel, t)
    vst4 = jnp.swapaxes(r3(pr["vs"]).reshape(b, t // tk_sel, tk_sel, LANES), 2, 3)
    o_sel = _selattn(qb3, r3(pr["ks"]), vst4, sel, tq_sel, tk_sel)
    o_win = _band(qb3, r3(pr["kw"]), r3(pr["vw"]), None, NSA_WINDOW, tq_win, NSA_HEADS, NSA_KV_HEADS)

    x1, eidx, gate = _route(x2, o_a.reshape(n, -1), o_cmp.reshape(n, -1), o_sel.reshape(n, -1),
                            o_win.reshape(n, -1), pr["gates"], jnp.asarray(_gate_expand(), BF16),
                            w_out.astype(BF16), norm_ffn.reshape(1, d), w_query.astype(BF16),
                            sub_keys_1.astype(BF16), sub_keys_2.astype(BF16), min(tm_route, n))
    ne = expert_down.shape[0]
    table = jnp.concatenate([expert_down.reshape(ne, SUBLANES, LANES), expert_up.reshape(ne, SUBLANES, LANES)],
                            axis=1)
    y = _peer(eidx, x1.reshape(n, SUBLANES, LANES), gate, norm_ffn.reshape(SUBLANES, LANES),
              norm_out.reshape(SUBLANES, LANES), table, tb_peer)
    return y.reshape(b, t, d)


def kernel(x, positions, norm_attn, w_in, attn_sinks, cmp_k_pos, cmp_k_w1, cmp_k_w2, cmp_v_pos, cmp_v_w1,
           cmp_v_w2, w_out, norm_ffn, peer_w_query, peer_sub_keys_1, peer_sub_keys_2, peer_expert_down,
           peer_expert_up, norm_f):
    assert norm_attn.shape[0] == 1, "single-layer block"
    return _layer(x, positions, norm_attn[0], w_in[0], attn_sinks[0], cmp_k_pos[0], cmp_k_w1[0], cmp_k_w2[0],
                  cmp_v_pos[0], cmp_v_w1[0], cmp_v_w2[0], w_out[0], norm_ffn[0], peer_w_query[0],
                  peer_sub_keys_1[0], peer_sub_keys_2[0], peer_expert_down[0], peer_expert_up[0], norm_f)
```

```python
import functools

import numpy as np
import jax
import jax.numpy as jnp
from jax import lax
from jax.experimental import pallas as pl
from jax.experimental.pallas import tpu as pltpu

F32 = jnp.float32
BF16 = jnp.bfloat16
I32 = jnp.int32

HEAD_DIM = 64
ROPE_THETA = 10000.0
RMS_EPS = 1e-6
NEG_BIG = -1e30

SWA_HEADS = 8
SWA_KV_HEADS = 2
SWA_WINDOW = 128

NSA_HEADS = 8
NSA_KV_HEADS = 2
CMP_LEN = 32
CMP_STRIDE = 16
CMP_HIDDEN = 256
SEL_BLOCK = 64
SEL_TOPN = 16
NSA_WINDOW = 512
NSA_N_GATES = 3

PEER_HEADS = 8
N_KEYS = 128
PEER_QDIM = 256
PEER_TOPK = 16
PEER_SLOTS = 3

LANES = 128
SUBLANES = 8
VMEM_LIMIT = 56 * 1024 * 1024

_IN_SPLITS = (("qa", 512), ("ka", 128), ("va", 128), ("qb", 512), ("kc", 128), ("vc", 128),
              ("ks", 128), ("vs", 128), ("kw", 128), ("vw", 128), ("gl", 24))
IN_WIDTH = sum(w for _, w in _IN_SPLITS)
IN_PAD = -(-IN_WIDTH // LANES) * LANES
_IN_OFF = {}
_o = 0
for _n, _w in _IN_SPLITS:
    _IN_OFF[_n] = _o
    _o += _w
_ROPED = ("qa", "ka", "qb", "kc", "ks", "kw")


def _gelu(x):
    return 0.5 * x * (1.0 + lax.erf(x * 0.7071067811865476))


def _params(*sem):
    return pltpu.CompilerParams(dimension_semantics=sem, vmem_limit_bytes=VMEM_LIMIT)


def _inproj_kernel(x_ref, pos_ref, g_ref, w_ref, invf_ref,
                   qa_ref, ka_ref, va_ref, qb_ref, kc_ref, vc_ref, ks_ref, vs_ref, kw_ref, vw_ref, gt_ref):
    x = x_ref[...]
    h = x * lax.rsqrt(jnp.mean(x * x, axis=-1, keepdims=True) + RMS_EPS) * g_ref[...]
    proj = jnp.dot(h.astype(BF16), w_ref[...], preferred_element_type=F32)
    ang = pos_ref[...].astype(F32) * invf_ref[...]
    lane = lax.broadcasted_iota(I32, ang.shape, 1)
    lo = (lane & (HEAD_DIM - 1)) < (HEAD_DIM // 2)
    cos = jnp.cos(ang)
    sin = jnp.sin(ang)
    sin_s = jnp.where(lo, -sin, sin)
    outs = {"qa": qa_ref, "ka": ka_ref, "va": va_ref, "qb": qb_ref, "kc": kc_ref, "vc": vc_ref,
            "ks": ks_ref, "vs": vs_ref, "kw": kw_ref, "vw": vw_ref}
    for name, width in _IN_SPLITS[:-1]:
        off = _IN_OFF[name]
        ref = outs[name]
        for i in range(width // LANES):
            v = proj[:, off + LANES * i: off + LANES * (i + 1)]
            if name in _ROPED:
                rot = jnp.where(lo, pltpu.roll(v, LANES - HEAD_DIM // 2, 1), pltpu.roll(v, HEAD_DIM // 2, 1))
                v = v * cos + rot * sin_s
            ref[:, LANES * i: LANES * (i + 1)] = v.astype(ref.dtype)
    gl = proj[:, _IN_OFF["gl"]: _IN_OFF["gl"] + LANES]
    gt_ref[...] = jax.nn.sigmoid(gl)


def _inproj(x2, pos2, g, w_pad, invf, tm):
    n, d = x2.shape
    row = lambda w: pl.BlockSpec((tm, w), lambda i: (i, 0))
    full = lambda a: pl.BlockSpec(a.shape, lambda i: (0,) * a.ndim)
    out_dt = {"qa": BF16, "ka": BF16, "va": BF16, "qb": BF16, "kc": F32, "vc": F32,
              "ks": BF16, "vs": BF16, "kw": BF16, "vw": BF16}
    names = [nm for nm, _ in _IN_SPLITS[:-1]]
    widths = dict(_IN_SPLITS)
    out_shape = [jax.ShapeDtypeStruct((n, widths[nm]), out_dt[nm]) for nm in names]
    out_shape.append(jax.ShapeDtypeStruct((n, LANES), F32))
    out_specs = [row(widths[nm]) for nm in names] + [row(LANES)]
    res = pl.pallas_call(
        _inproj_kernel,
        grid=(n // tm,),
        in_specs=[row(d), row(1), full(g), full(w_pad), full(invf)],
        out_specs=out_specs,
        out_shape=out_shape,
        compiler_params=_params("parallel"),
        name="inproj",
    )(x2, pos2, g, w_pad, invf)
    out = dict(zip(names, res[:-1]))
    out["gates"] = res[-1]
    return out


def _compress_kernel(kc_ref, vc_ref, kpos_ref, kw1_ref, kw2_ref, vpos_ref, vw1_ref, vw2_ref, ko_ref, vo_ref):
    ncp = ko_ref.shape[0]
    for src, pos_ref, w1_ref, w2_ref, out_ref in ((kc_ref, kpos_ref, kw1_ref, kw2_ref, ko_ref),
                                                  (vc_ref, vpos_ref, vw1_ref, vw2_ref, vo_ref)):
        bias = jnp.dot(pos_ref[...].astype(BF16), w1_ref[...], preferred_element_type=F32)
        acc_a = [jnp.zeros((ncp, CMP_HIDDEN), F32) for _ in range(2)]
        acc_b = [jnp.zeros((ncp, CMP_HIDDEN), F32) for _ in range(2)]
        for j in range(CMP_STRIDE):
            pj = src[pl.ds(j, ncp, stride=CMP_STRIDE), :].astype(BF16)
            wa = w1_ref[j * HEAD_DIM:(j + 1) * HEAD_DIM, :]
            wb = w1_ref[(CMP_STRIDE + j) * HEAD_DIM:(CMP_STRIDE + j + 1) * HEAD_DIM, :]
            for hh in range(2):
                ph = pj[:, hh * HEAD_DIM:(hh + 1) * HEAD_DIM]
                acc_a[hh] = acc_a[hh] + jnp.dot(ph, wa, preferred_element_type=F32)
                acc_b[hh] = acc_b[hh] + jnp.dot(ph, wb, preferred_element_type=F32)
        for hh in range(2):
            h1 = acc_a[hh] + pltpu.roll(acc_b[hh], ncp - 1, 0) + bias
            out = jnp.dot(_gelu(h1).astype(BF16), w2_ref[...], preferred_element_type=F32)
            out_ref[:, hh * HEAD_DIM:(hh + 1) * HEAD_DIM] = out


def _compress(kc3, vc3, kpos, kw1, kw2, vpos, vw1, vw2):
    b, t, w = kc3.shape
    ncp = t // CMP_STRIDE
    per_b = pl.BlockSpec((None, t, w), lambda i: (i, 0, 0))
    full = lambda a: pl.BlockSpec(a.shape, lambda i: (0,) * a.ndim)
    out_spec = pl.BlockSpec((None, ncp, w), lambda i: (i, 0, 0))
    return pl.pallas_call(
        _compress_kernel,
        grid=(b,),
        in_specs=[per_b, per_b, full(kpos), full(kw1), full(kw2), full(vpos), full(vw1), full(vw2)],
        out_specs=[out_spec, out_spec],
        out_shape=[jax.ShapeDtypeStruct((b, ncp, w), F32)] * 2,
        compiler_params=_params("parallel"),
        name="compress",
    )(kc3, vc3, kpos, kw1, kw2, vpos, vw1, vw2)


def _cmpsel_kernel(q_ref, kc_ref, vc_ref, ovt_ref, o_ref, sel_ref, *, tq, n_cmp):
    i = pl.program_id(1)
    t0 = i * tq
    g = NSA_HEADS // NSA_KV_HEADS
    ncp = kc_ref.shape[0]
    ns = ovt_ref.shape[0]
    rows = g * tq
    tpos = t0 + (lax.broadcasted_iota(I32, (rows, ncp), 0) & (tq - 1))
    cidx = lax.broadcasted_iota(I32, (rows, ncp), 1)
    vis = (cidx * CMP_STRIDE + (CMP_LEN - 1) <= tpos) & (cidx < n_cmp)
    jidx = lax.broadcasted_iota(I32, (ns, tq), 0)
    tcol = t0 + lax.broadcasted_iota(I32, (ns, tq), 1)
    cur = tcol // SEL_BLOCK
    forced = (jidx == 0) | (jidx == cur) | (jidx == cur - 1)
    valid = jidx * SEL_BLOCK <= tcol
    for hk in range(NSA_KV_HEADS):
        kk = kc_ref[:, hk * HEAD_DIM:(hk + 1) * HEAD_DIM].astype(BF16)
        vv = vc_ref[:, hk * HEAD_DIM:(hk + 1) * HEAD_DIM].astype(BF16)
        q4 = jnp.concatenate(
            [q_ref[:, (hk * g + gi) * HEAD_DIM:(hk * g + gi + 1) * HEAD_DIM] for gi in range(g)], axis=0)
        s = lax.dot_general(q4, kk, (((1,), (1,)), ((), ())), preferred_element_type=F32) * (HEAD_DIM ** -0.5)
        s = jnp.where(vis, s, NEG_BIG)
        m = jnp.max(s, axis=-1, keepdims=True)
        e = jnp.exp(s - m)
        p = jnp.where(vis, e / jnp.sum(e, axis=-1, keepdims=True), 0.0)
        o = jnp.dot(p.astype(BF16), vv, preferred_element_type=F32)
        for gi in range(g):
            h = hk * g + gi
            o_ref[:, h * HEAD_DIM:(h + 1) * HEAD_DIM] = o[gi * tq:(gi + 1) * tq]
        psum = p[0:tq]
        for gi in range(1, g):
            psum = psum + p[gi * tq:(gi + 1) * tq]
        imp_t = lax.dot_general(ovt_ref[...], psum, (((1,), (1,)), ((), ())),
                                precision=lax.Precision.HIGHEST, preferred_element_type=F32)
        score = jnp.where(forced, jnp.inf, jnp.where(valid, imp_t, -jnp.inf))
        cnt = jnp.zeros((ns, tq), F32)
        for r in range(ns):
            row = score[r:r + 1, :]
            beats = (row > score) | ((row >= score) & (jidx > r))
            cnt = cnt + jnp.where(beats, 1.0, 0.0)
        sel_ref[hk] = jnp.where((cnt < SEL_TOPN) & (score > -jnp.inf), 1.0, 0.0)


def _cmpsel(qb3, kcmp, vcmp, ovt, tq, n_cmp):
    b, t, w = qb3.shape
    ncp = kcmp.shape[1]
    ns = ovt.shape[0]
    kern = functools.partial(_cmpsel_kernel, tq=tq, n_cmp=n_cmp)
    return pl.pallas_call(
        kern,
        grid=(b, t // tq),
        in_specs=[pl.BlockSpec((None, tq, w), lambda bi, i: (bi, i, 0)),
                  pl.BlockSpec((None, ncp, LANES), lambda bi, i: (bi, 0, 0)),
                  pl.BlockSpec((None, ncp, LANES), lambda bi, i: (bi, 0, 0)),
                  pl.BlockSpec(ovt.shape, lambda bi, i: (0, 0))],
        out_specs=[pl.BlockSpec((None, tq, w), lambda bi, i: (bi, i, 0)),
                   pl.BlockSpec((None, NSA_KV_HEADS, ns, tq), lambda bi, i: (bi, 0, 0, i))],
        out_shape=[jax.ShapeDtypeStruct((b, t, w), F32), jax.ShapeDtypeStruct((b, NSA_KV_HEADS, ns, t), F32)],
        compiler_params=_params("parallel", "parallel"),
        name="cmpsel",
    )(qb3, kcmp, vcmp, ovt)


def _selattn_kernel(q_ref, k_ref, vt_ref, sel_ref, o_ref, m_sc, l_sc, acc_sc, *, tq, tk):
    i = pl.program_id(1)
    t0 = i * tq
    g = NSA_HEADS // NSA_KV_HEADS
    rows = g * tq
    assert tk % tq == 0, "one key tile must cover the whole diagonal of a query tile"
    n_below = (i * tq) // tk
    nb = tk // SEL_BLOCK
    qpos = t0 + (lax.broadcasted_iota(I32, (tk, rows), 1) & (tq - 1))
    krow = lax.broadcasted_iota(I32, (tk, rows), 0)
    for hk in range(NSA_KV_HEADS):
        qt = jnp.concatenate(
            [jnp.transpose(q_ref[:, (hk * g + gi) * HEAD_DIM:(hk * g + gi + 1) * HEAD_DIM].astype(F32))
             for gi in range(g)], axis=1)
        qt = (qt * (HEAD_DIM ** -0.5)).astype(BF16)
        m_sc[...] = jnp.full(m_sc.shape, NEG_BIG, F32)
        l_sc[...] = jnp.zeros(l_sc.shape, F32)
        acc_sc[...] = jnp.zeros(acc_sc.shape, F32)

        def tile(kt, diagonal):
            ks = pl.multiple_of(kt * tk, tk)
            kk = k_ref[pl.ds(ks, tk), hk * HEAD_DIM:(hk + 1) * HEAD_DIM]
            s = jnp.dot(kk, qt, preferred_element_type=F32)
            bias = jnp.concatenate(
                [jnp.broadcast_to(
                    jnp.concatenate([(1.0 - sel_ref[hk, pl.ds(kt * nb + jj, 1), :]) * NEG_BIG] * g, axis=1),
                    (SEL_BLOCK, rows)) for jj in range(nb)], axis=0)
            s = s + bias
            if diagonal:
                s = jnp.where(ks + krow <= qpos, s, NEG_BIG)
            m_old = m_sc[...]
            m_new = jnp.maximum(m_old, jnp.max(s, axis=0, keepdims=True))
            alpha = jnp.exp(m_old - m_new)
            p = jnp.exp(s - m_new)
            l_sc[...] = alpha * l_sc[...] + jnp.sum(p, axis=0, keepdims=True)
            vt = vt_ref[kt, hk * HEAD_DIM:(hk + 1) * HEAD_DIM, :]
            acc_sc[...] = alpha * acc_sc[...] + jnp.dot(vt, p.astype(BF16), preferred_element_type=F32)
            m_sc[...] = m_new

        def below_diagonal(kt, carry):
            tile(kt, False)
            return carry

        lax.fori_loop(0, n_below, below_diagonal, 0)
        tile(n_below, True)
        o_t = acc_sc[...] / l_sc[...]
        for gi in range(g):
            h = hk * g + gi
            o_ref[:, h * HEAD_DIM:(h + 1) * HEAD_DIM] = jnp.transpose(o_t[:, gi * tq:(gi + 1) * tq])


def _selattn(qb3, ks3, vst4, sel4, tq, tk):
    b, t, w = qb3.shape
    g = NSA_HEADS // NSA_KV_HEADS
    ns = sel4.shape[2]
    kern = functools.partial(_selattn_kernel, tq=tq, tk=tk)
    return pl.pallas_call(
        kern,
        grid=(b, t // tq),
        in_specs=[pl.BlockSpec((None, tq, w), lambda bi, i: (bi, i, 0)),
                  pl.BlockSpec((None, t, LANES), lambda bi, i: (bi, 0, 0)),
                  pl.BlockSpec((None, t // tk, LANES, tk), lambda bi, i: (bi, 0, 0, 0)),
                  pl.BlockSpec((None, NSA_KV_HEADS, ns, tq), lambda bi, i: (bi, 0, 0, i))],
        out_specs=pl.BlockSpec((None, tq, w), lambda bi, i: (bi, i, 0)),
        out_shape=jax.ShapeDtypeStruct((b, t, w), F32),
        scratch_shapes=[pltpu.VMEM((1, g * tq), F32), pltpu.VMEM((1, g * tq), F32),
                        pltpu.VMEM((HEAD_DIM, g * tq), F32)],
        compiler_params=_params("parallel", "arbitrary"),
        name="selattn",
    )(qb3, ks3, vst4, sel4)


def _band_kernel(*refs, window, tq, span, n_heads, n_kv, has_sink):
    if has_sink:
        sink_ref, q_ref, k_ref, vt_ref, o_ref = refs
    else:
        q_ref, k_ref, vt_ref, o_ref = refs
    i = pl.program_id(1)
    t0 = i * tq
    start = pl.multiple_of(jnp.maximum(t0 + tq - span, 0), tq)
    tile0 = start // LANES
    g = n_heads // n_kv
    rows = g * tq
    qpos = t0 + (lax.broadcasted_iota(I32, (span, rows), 1) & (tq - 1))
    kpos = start + lax.broadcasted_iota(I32, (span, rows), 0)
    diff = qpos - kpos
    mask = (diff >= 0) & (diff < window)
    for hk in range(n_kv):
        kk = k_ref[pl.ds(start, span), hk * HEAD_DIM:(hk + 1) * HEAD_DIM]
        qt = jnp.concatenate(
            [jnp.transpose(q_ref[:, (hk * g + gi) * HEAD_DIM:(hk * g + gi + 1) * HEAD_DIM].astype(F32))
             for gi in range(g)], axis=1)
        qt = (qt * (HEAD_DIM ** -0.5)).astype(BF16)
        s = jnp.where(mask, jnp.dot(kk, qt, preferred_element_type=F32), NEG_BIG)
        m = jnp.max(s, axis=0, keepdims=True)
        if has_sink:
            sink = jnp.concatenate([jnp.full((1, tq), sink_ref[hk * g + gi], F32) for gi in range(g)], axis=1)
            m = jnp.maximum(m, sink)
        p = jnp.exp(s - m)
        l = jnp.sum(p, axis=0, keepdims=True)
        if has_sink:
            l = l + jnp.exp(sink - m)
        vt = jnp.concatenate([vt_ref[tile0 + j, hk * HEAD_DIM:(hk + 1) * HEAD_DIM, :]
                              for j in range(span // LANES)], axis=1)
        o_t = jnp.dot(vt, p.astype(BF16), preferred_element_type=F32) / l
        for gi in range(g):
            h = hk * g + gi
            o_ref[:, h * HEAD_DIM:(h + 1) * HEAD_DIM] = jnp.transpose(o_t[:, gi * tq:(gi + 1) * tq])


def _band(q3, k3, v3, sinks, window, tq, n_heads, n_kv):
    b, t, w = q3.shape
    nprev = -(-(window - 1) // tq)
    span = min((nprev + 1) * tq, t)
    has_sink = sinks is not None
    vt4 = jnp.swapaxes(v3.reshape(b, t // LANES, LANES, LANES), 2, 3)
    kern = functools.partial(_band_kernel, window=window, tq=tq, span=span, n_heads=n_heads, n_kv=n_kv,
                             has_sink=has_sink)
    in_specs = [pl.BlockSpec((None, tq, w), lambda bi, i: (bi, i, 0)),
                pl.BlockSpec((None, t, LANES), lambda bi, i: (bi, 0, 0)),
                pl.BlockSpec((None, t // LANES, LANES, LANES), lambda bi, i: (bi, 0, 0, 0))]
    args = [q3, k3, vt4]
    if has_sink:
        in_specs = [pl.BlockSpec(memory_space=pltpu.SMEM)] + in_specs
        args = [sinks] + args
    return pl.pallas_call(
        kern,
        grid=(b, t // tq),
        in_specs=in_specs,
        out_specs=pl.BlockSpec((None, tq, w), lambda bi, i: (bi, i, 0)),
        out_shape=jax.ShapeDtypeStruct((b, t, w), F32),
        compiler_params=_params("parallel", "parallel"),
        name="band_sink" if has_sink else "band_win",
    )(*args)


def _first_max_rows(val, payload=None):
    rows, n = val.shape
    sub = lax.broadcasted_iota(I32, (SUBLANES, n), 0).astype(F32)
    parts = [[val[r:r + SUBLANES] for r in range(0, rows, SUBLANES)],
             [sub + float(r) for r in range(0, rows, SUBLANES)]]
    if payload is not None:
        parts.append([payload[r:r + SUBLANES] for r in range(0, rows, SUBLANES)])
    while len(parts[0]) > 1:
        nxt = [[] for _ in parts]
        for j in range(0, len(parts[0]) - 1, 2):
            take_right = parts[0][j + 1] > parts[0][j]
            for dst, src in zip(nxt, parts):
                dst.append(jnp.where(take_right, src[j + 1], src[j]))
        if len(parts[0]) % 2:
            for dst, src in zip(nxt, parts):
                dst.append(src[-1])
        parts = nxt
    v8, i8 = parts[0][0], parts[1][0]
    m = jnp.max(v8, axis=0, keepdims=True)
    first = jnp.min(jnp.where(v8 == m, i8, float(rows)), axis=0, keepdims=True)
    if payload is None:
        return m, first
    return m, first, jnp.max(jnp.where(i8 == first, parts[2][0], -1.0), axis=0, keepdims=True)


def _topk_rows(s, k, ridx):
    vals, idxs = [], []
    for _ in range(k):
        m, first = _first_max_rows(s)
        vals.append(m)
        idxs.append(first)
        s = jnp.where(ridx == first, -jnp.inf, s)
    return jnp.concatenate(vals, axis=0), jnp.concatenate(idxs, axis=0)


def _peer_candidates(t1, i1, t2, i2):
    vals = [t1[0:1] + t2]
    eids = [i1[0:1] * float(N_KEYS) + i2]
    for a in range(1, 8):
        vals.append(t1[a:a + 1] + t2[0:8])
        eids.append(i1[a:a + 1] * float(N_KEYS) + i2[0:8])
    vals.append(t1[8:16] + t2[0:1])
    eids.append(i1[8:16] * float(N_KEYS) + i2[0:1])
    return jnp.concatenate(vals, axis=0), jnp.concatenate(eids, axis=0)


def _route_kernel(x_ref, oa_ref, oc_ref, os_ref, ow_ref, gt_ref, gexp_ref, wo_ref, gf_ref, wq_ref,
                  k1_ref, k2_ref, x1_ref, eidx_ref, gate_ref):
    tm = x_ref.shape[0]
    gt = gt_ref[...]
    g_hi = gt.astype(BF16)
    g_lo = (gt - g_hi.astype(F32)).astype(BF16)
    ob = jnp.zeros(oc_ref.shape, F32)
    for j, br in enumerate((oc_ref, os_ref, ow_ref)):
        ex = gexp_ref[j]
        gj = (jnp.dot(g_hi, ex, preferred_element_type=F32) + jnp.dot(g_lo, ex, preferred_element_type=F32))
        ob = ob + gj * br[...]
    half = oa_ref.shape[1]
    mixed = (jnp.dot(oa_ref[...].astype(BF16), wo_ref[0:half, :], preferred_element_type=F32)
             + jnp.dot(ob.astype(BF16), wo_ref[half:, :], preferred_element_type=F32))
    x1 = x_ref[...] + mixed
    x1_ref[...] = x1.reshape(tm, SUBLANES, LANES)
    h2 = x1 * lax.rsqrt(jnp.mean(x1 * x1, axis=-1, keepdims=True) + RMS_EPS) * gf_ref[...]
    q = jnp.dot(h2.astype(BF16), wq_ref[...], preferred_element_type=F32)
    hq = PEER_QDIM // 2
    ridx = lax.broadcasted_iota(I32, (N_KEYS, tm), 0).astype(F32)
    e_rows, g_rows = [], []
    for h in range(PEER_HEADS):
        q1 = q[:, h * PEER_QDIM: h * PEER_QDIM + hq].astype(BF16)
        q2 = q[:, h * PEER_QDIM + hq:(h + 1) * PEER_QDIM].astype(BF16)
        s1 = lax.dot_general(k1_ref[...], q1, (((1,), (1,)), ((), ())), preferred_element_type=F32)
        s2 = lax.dot_general(k2_ref[...], q2, (((1,), (1,)), ((), ())), preferred_element_type=F32)
        t1, i1 = _topk_rows(s1, PEER_TOPK, ridx)
        t2, i2 = _topk_rows(s2, PEER_TOPK, ridx)
        cand, cand_e = _peer_candidates(t1, i1, t2, i2)
        cidx = lax.broadcasted_iota(I32, cand.shape, 0).astype(F32)
        sc, ee = [], []
        for _ in range(PEER_TOPK):
            m, first, e_first = _first_max_rows(cand, cand_e)
            sc.append(m)
            ee.append(e_first)
            cand = jnp.where(cidx == first, -jnp.inf, cand)
        sc = jnp.concatenate(sc, axis=0)
        ex = jnp.exp(sc - sc[0:1])
        g_rows.append(ex / jnp.sum(ex, axis=0, keepdims=True))
        e_rows.append(jnp.concatenate(ee, axis=0))
    e_t = jnp.concatenate(e_rows, axis=0)
    g_t = jnp.concatenate(g_rows, axis=0)
    eidx_ref[...] = jnp.transpose(e_t).astype(I32)
    gate_ref[...] = jnp.transpose(g_t)


def _route(x2, oa, oc, osel, ow, gates, gexp, wo, gf, wq, k1, k2, tm):
    n, d = x2.shape
    row = lambda w: pl.BlockSpec((tm, w), lambda i: (i, 0))
    full = lambda a: pl.BlockSpec(a.shape, lambda i: (0,) * a.ndim)
    nk = PEER_HEADS * PEER_TOPK
    return pl.pallas_call(
        _route_kernel,
        grid=(n // tm,),
        in_specs=[row(d), row(oa.shape[1]), row(oc.shape[1]), row(osel.shape[1]), row(ow.shape[1]), row(LANES),
                  full(gexp), full(wo), full(gf), full(wq), full(k1), full(k2)],
        out_specs=[pl.BlockSpec((tm, SUBLANES, LANES), lambda i: (i, 0, 0)), row(nk), row(nk)],
        out_shape=[jax.ShapeDtypeStruct((n, SUBLANES, LANES), F32), jax.ShapeDtypeStruct((n, nk), I32),
                   jax.ShapeDtypeStruct((n, nk), F32)],
        compiler_params=_params("parallel"),
        name="route",
    )(x2, oa, oc, osel, ow, gates, gexp, wo, gf, wq, k1, k2)


def _fold8(p):
    sub = lax.broadcasted_iota(I32, (SUBLANES, LANES), 0)
    lo4 = sub < 4
    q = [jnp.where(lo4, p[j], p[j + 4]) + pltpu.roll(jnp.where(lo4, p[j + 4], p[j]), 4, 0) for j in range(4)]
    m2 = (sub & 3) < 2
    r = [jnp.where(m2, q[j] + pltpu.roll(q[j], 6, 0), q[j + 2] + pltpu.roll(q[j + 2], 2, 0)) for j in range(2)]
    m1 = (sub & 1) == 0
    return jnp.where(m1, r[0] + pltpu.roll(r[0], 7, 0), r[1] + pltpu.roll(r[1], 1, 0))


def _peer_kernel(e0_ref, e1_ref, e2_ref, x_ref, gate_ref, gffn_ref, gfin_ref, tab_ref, o_ref,
                 buf, sem, a_sc, cb_sc, *, tb):
    step = pl.program_id(0)
    nsteps = pl.num_programs(0)
    slot = step % PEER_SLOTS
    fill = (step + PEER_SLOTS - 1) % PEER_SLOTS
    nk = PEER_HEADS * PEER_TOPK
    npair = tb * nk

    def issue_rows(eref, dst_slot, t, k0, k1):
        erow = eref.at[t]
        for k in range(k0, k1):
            copy = pltpu.make_async_copy(tab_ref.at[erow[k]], buf.at[dst_slot, t * nk + k], sem.at[dst_slot])
            copy.start(priority=k % 2)

    def wait_slot(s):
        pltpu.make_async_copy(tab_ref.at[pl.ds(0, npair)], buf.at[s], sem.at[s]).wait()

    @pl.when(step == 0)
    def _():
        def first(t, carry):
            issue_rows(e0_ref, 0, t, 0, nk)
            return carry
        lax.fori_loop(0, tb, first, 0)
        def second(t, carry):
            issue_rows(e1_ref, 1, t, 0, nk)
            return carry
        lax.fori_loop(0, tb, second, 0)

    eye = jnp.where(lax.broadcasted_iota(I32, (nk, nk), 0) == lax.broadcasted_iota(I32, (nk, nk), 1),
                    1.0, 0.0).astype(BF16)
    gr = gate_ref[...]
    g0 = gr.astype(BF16)
    r1 = gr - g0.astype(F32)
    g1 = r1.astype(BF16)
    g2 = (r1 - g1.astype(F32)).astype(BF16)
    dn = (((1,), (1,)), ((), ()))
    gate_t = (lax.dot_general(eye, g0, dn, preferred_element_type=F32)
              + lax.dot_general(eye, g1, dn, preferred_element_type=F32)
              + lax.dot_general(eye, g2, dn, preferred_element_type=F32))

    wait_slot(slot)

    lane_tb = lax.broadcasted_iota(I32, (nk, tb), 1)
    inv_d = 1.0 / (SUBLANES * LANES)
    a_sc[...] = jnp.zeros(a_sc.shape, F32)

    ngroup = nk // SUBLANES
    rows_p1 = 6
    rows_p2 = SUBLANES - rows_p1

    def dots(t, carry):
        x8 = x_ref[t]
        ms = jnp.sum(jnp.sum(x8 * x8, axis=1, keepdims=True), axis=0, keepdims=True) * inv_d
        h8 = x8 * lax.rsqrt(ms + RMS_EPS) * gffn_ref[...]
        base = t * nk
        folded = []
        for gi in range(nk // SUBLANES):
            blk = buf[slot, pl.ds(base + gi * SUBLANES, SUBLANES), 0:SUBLANES, :]
            folded.append(_fold8([blk[j] * h8 for j in range(SUBLANES)]))
            issue_rows(e2_ref, fill, t, gi * rows_p1, (gi + 1) * rows_p1)
        a = jnp.sum(jnp.concatenate(folded, axis=0), axis=1, keepdims=True)
        a_sc[...] = jnp.where(lane_tb == t, a, a_sc[...])
        return carry

    lax.fori_loop(0, tb, dots, 0, unroll=2)

    a_sc[...] = gate_t * _gelu(a_sc[...])

    def combine(t, carry):
        base = t * nk
        w_col = jnp.sum(jnp.where(lane_tb == t, a_sc[...], 0.0), axis=1, keepdims=True)
        cb_sc[t] = jnp.broadcast_to(w_col, (nk, LANES))
        accs = [jnp.zeros((SUBLANES, LANES), F32) for _ in range(4)]
        for gi in range(ngroup):
            for k in range(gi * SUBLANES, (gi + 1) * SUBLANES):
                ck = jnp.broadcast_to(cb_sc[t, k:k + 1, :], (SUBLANES, LANES))
                accs[k % 4] = accs[k % 4] + ck * buf[slot, base + k, SUBLANES:2 * SUBLANES, :]
            issue_rows(e2_ref, fill, t, ngroup * rows_p1 + gi * rows_p2, ngroup * rows_p1 + (gi + 1) * rows_p2)
        y8 = x_ref[t] + ((accs[0] + accs[1]) + (accs[2] + accs[3]))
        ms2 = jnp.sum(jnp.sum(y8 * y8, axis=1, keepdims=True), axis=0, keepdims=True) * inv_d
        o_ref[t] = y8 * lax.rsqrt(ms2 + RMS_EPS) * gfin_ref[...]
        return carry

    lax.fori_loop(0, tb, combine, 0, unroll=4)

    @pl.when(step == nsteps - 1)
    def _():
        for ahead in range(1, PEER_SLOTS):
            wait_slot((step + ahead) % PEER_SLOTS)


def _peer(eidx, x1s, gate, gffn8, gfin8, table, tb):
    n = x1s.shape[0]
    nk = PEER_HEADS * PEER_TOPK
    nsteps = n // tb
    kern = functools.partial(_peer_kernel, tb=tb)
    return pl.pallas_call(
        kern,
        grid=(nsteps,),
        in_specs=[pl.BlockSpec((tb, nk), lambda i: (i, 0), memory_space=pltpu.SMEM),
                  pl.BlockSpec((tb, nk), lambda i: ((i + 1) % nsteps, 0), memory_space=pltpu.SMEM),
                  pl.BlockSpec((tb, nk), lambda i: ((i + PEER_SLOTS - 1) % nsteps, 0), memory_space=pltpu.SMEM),
                  pl.BlockSpec((tb, SUBLANES, LANES), lambda i: (i, 0, 0)),
                  pl.BlockSpec((tb, nk), lambda i: (i, 0)),
                  pl.BlockSpec((SUBLANES, LANES), lambda i: (0, 0)),
                  pl.BlockSpec((SUBLANES, LANES), lambda i: (0, 0)),
                  pl.BlockSpec(memory_space=pl.ANY)],
        out_specs=pl.BlockSpec((tb, SUBLANES, LANES), lambda i: (i, 0, 0)),
        out_shape=jax.ShapeDtypeStruct((n, SUBLANES, LANES), F32),
        scratch_shapes=[pltpu.VMEM((PEER_SLOTS, tb * nk, 2 * SUBLANES, LANES), F32),
                        pltpu.SemaphoreType.DMA((PEER_SLOTS,)),
                        pltpu.VMEM((nk, tb), F32),
                        pltpu.VMEM((tb, nk, LANES), F32)],
        compiler_params=_params("arbitrary"),
        name="peer",
    )(eidx, eidx, eidx, x1s, gate, gffn8, gfin8, table)


def _overlap_t(t):
    nc = (t - CMP_LEN) // CMP_STRIDE + 1
    ns = t // SEL_BLOCK
    ncp = t // CMP_STRIDE
    cstart = np.arange(nc) * CMP_STRIDE
    bstart = np.arange(ns) * SEL_BLOCK
    lo = np.maximum(cstart[:, None], bstart[None, :])
    hi = np.minimum(cstart[:, None] + CMP_LEN, bstart[None, :] + SEL_BLOCK)
    ov = np.clip(hi - lo, 0, None).astype(np.float32) / CMP_LEN
    out = np.zeros((ns, ncp), np.float32)
    out[:, :nc] = ov.T
    return out, nc


def _gate_expand():
    ex = np.zeros((NSA_N_GATES, LANES, NSA_HEADS * HEAD_DIM), np.float32)
    for j in range(NSA_N_GATES):
        for h in range(NSA_HEADS):
            ex[j, h * NSA_N_GATES + j, h * HEAD_DIM:(h + 1) * HEAD_DIM] = 1.0
    return ex


def _layer(x, positions, norm_attn, w_in, attn_sinks, cmp_k_pos, cmp_k_w1, cmp_k_w2, cmp_v_pos, cmp_v_w1,
           cmp_v_w2, w_out, norm_ffn, w_query, sub_keys_1, sub_keys_2, expert_down, expert_up, norm_out,
           tm_in=512, tq_cmp=256, tq_sel=256, tk_sel=512, tq_swa=128, tq_win=256, tm_route=256, tb_peer=16):
    b, t, d = x.shape
    n = b * t
    x2 = x.reshape(n, d)
    half = HEAD_DIM // 2
    inv_freq = ROPE_THETA ** (-jnp.arange(half, dtype=F32) / half)
    invf = jnp.tile(inv_freq, LANES // half).reshape(1, LANES)
    w_pad = jnp.pad(w_in, ((0, 0), (0, IN_PAD - IN_WIDTH))).astype(BF16)
    pr = _inproj(x2, positions.reshape(n, 1), norm_attn.reshape(1, d), w_pad, invf, min(tm_in, n))
    r3 = lambda a: a.reshape(b, t, a.shape[-1])

    o_a = _band(r3(pr["qa"]), r3(pr["ka"]), r3(pr["va"]), attn_sinks.astype(F32), SWA_WINDOW, tq_swa,
                SWA_HEADS, SWA_KV_HEADS)
    kcmp, vcmp = _compress(r3(pr["kc"]), r3(pr["vc"]),
                           cmp_k_pos.reshape(1, -1), cmp_k_w1.astype(BF16), cmp_k_w2.astype(BF16),
                           cmp_v_pos.reshape(1, -1), cmp_v_w1.astype(BF16), cmp_v_w2.astype(BF16))
    ovt, n_cmp = _overlap_t(t)
    qb3 = r3(pr["qb"])
    o_cmp, sel = _cmpsel(qb3, kcmp, vcmp, jnp.asarray(ovt), tq_cmp, n_cmp)
    tk_sel = min(tk_sel, t)
    vst4 = jnp.swapaxes(r3(pr["vs"]).reshape(b, t // tk_sel, tk_sel, LANES), 2, 3)
    o_sel = _selattn(qb3, r3(pr["ks"]), vst4, sel, tq_sel, tk_sel)
    o_win = _band(qb3, r3(pr["kw"]), r3(pr["vw"]), None, NSA_WINDOW, tq_win, NSA_HEADS, NSA_KV_HEADS)

    x1, eidx, gate = _route(x2, o_a.reshape(n, -1), o_cmp.reshape(n, -1), o_sel.reshape(n, -1),
                            o_win.reshape(n, -1), pr["gates"], jnp.asarray(_gate_expand(), BF16),
                            w_out.astype(BF16), norm_ffn.reshape(1, d), w_query.astype(BF16),
                            sub_keys_1.astype(BF16), sub_keys_2.astype(BF16), min(tm_route, n))
    ne = expert_down.shape[0]
    table = jnp.stack([expert_down.reshape(ne, SUBLANES, LANES), expert_up.reshape(ne, SUBLANES, LANES)],
                      axis=1).reshape(ne, 2 * SUBLANES, LANES)
    y = _peer(eidx, x1, gate, norm_ffn.reshape(SUBLANES, LANES),
              norm_out.reshape(SUBLANES, LANES), table, tb_peer)
    return y.reshape(b, t, d)


def kernel(x, positions, norm_attn, w_in, attn_sinks, cmp_k_pos, cmp_k_w1, cmp_k_w2, cmp_v_pos, cmp_v_w1,
           cmp_v_w2, w_out, norm_ffn, peer_w_query, peer_sub_keys_1, peer_sub_keys_2, peer_expert_down,
           peer_expert_up, norm_f):
    assert norm_attn.shape[0] == 1, "single-layer block"
    return _layer(x, positions, norm_attn[0], w_in[0], attn_sinks[0], cmp_k_pos[0], cmp_k_w1[0], cmp_k_w2[0],
                  cmp_v_pos[0], cmp_v_w1[0], cmp_v_w2[0], w_out[0], norm_ffn[0], peer_w_query[0],
                  peer_sub_keys_1[0], peer_sub_keys_2[0], peer_expert_down[0], peer_expert_up[0], norm_f)
```

```python
import functools

import numpy as np
import jax
import jax.numpy as jnp
from jax import lax
from jax.experimental import pallas as pl
from jax.experimental.pallas import tpu as pltpu

F32 = jnp.float32
BF16 = jnp.bfloat16
I32 = jnp.int32

HEAD_DIM = 64
ROPE_THETA = 10000.0
RMS_EPS = 1e-6
NEG_BIG = -1e30

SWA_HEADS = 8
SWA_KV_HEADS = 2
SWA_WINDOW = 128

NSA_HEADS = 8
NSA_KV_HEADS = 2
CMP_LEN = 32
CMP_STRIDE = 16
CMP_HIDDEN = 256
SEL_BLOCK = 64
SEL_TOPN = 16
NSA_WINDOW = 512
NSA_N_GATES = 3

PEER_HEADS = 8
N_KEYS = 128
PEER_QDIM = 256
PEER_TOPK = 16
PEER_SLOTS = 3

LANES = 128
SUBLANES = 8
VMEM_LIMIT = 56 * 1024 * 1024

_IN_SPLITS = (("qa", 512), ("ka", 128), ("va", 128), ("qb", 512), ("kc", 128), ("vc", 128),
              ("ks", 128), ("vs", 128), ("kw", 128), ("vw", 128), ("gl", 24))
IN_WIDTH = sum(w for _, w in _IN_SPLITS)
IN_PAD = -(-IN_WIDTH // LANES) * LANES
_IN_OFF = {}
_o = 0
for _n, _w in _IN_SPLITS:
    _IN_OFF[_n] = _o
    _o += _w
_ROPED = ("qa", "ka", "qb", "kc", "ks", "kw")


def _gelu(x):
    return 0.5 * x * (1.0 + lax.erf(x * 0.7071067811865476))


def _params(*sem):
    return pltpu.CompilerParams(dimension_semantics=sem, vmem_limit_bytes=VMEM_LIMIT)


def _inproj_kernel(x_ref, pos_ref, g_ref, w_ref, invf_ref,
                   qa_ref, ka_ref, va_ref, qb_ref, kc_ref, vc_ref, ks_ref, vs_ref, kw_ref, vw_ref, gt_ref):
    x = x_ref[...]
    h = x * lax.rsqrt(jnp.mean(x * x, axis=-1, keepdims=True) + RMS_EPS) * g_ref[...]
    proj = jnp.dot(h.astype(BF16), w_ref[...], preferred_element_type=F32)
    ang = pos_ref[...].astype(F32) * invf_ref[...]
    lane = lax.broadcasted_iota(I32, ang.shape, 1)
    lo = (lane & (HEAD_DIM - 1)) < (HEAD_DIM // 2)
    cos = jnp.cos(ang)
    sin = jnp.sin(ang)
    sin_s = jnp.where(lo, -sin, sin)
    outs = {"qa": qa_ref, "ka": ka_ref, "va": va_ref, "qb": qb_ref, "kc": kc_ref, "vc": vc_ref,
            "ks": ks_ref, "vs": vs_ref, "kw": kw_ref, "vw": vw_ref}
    for name, width in _IN_SPLITS[:-1]:
        off = _IN_OFF[name]
        ref = outs[name]
        for i in range(width // LANES):
            v = proj[:, off + LANES * i: off + LANES * (i + 1)]
            if name in _ROPED:
                rot = jnp.where(lo, pltpu.roll(v, LANES - HEAD_DIM // 2, 1), pltpu.roll(v, HEAD_DIM // 2, 1))
                v = v * cos + rot * sin_s
            ref[:, LANES * i: LANES * (i + 1)] = v.astype(ref.dtype)
    gl = proj[:, _IN_OFF["gl"]: _IN_OFF["gl"] + LANES]
    gt_ref[...] = jax.nn.sigmoid(gl)


def _inproj(x2, pos2, g, w_pad, invf, tm):
    n, d = x2.shape
    row = lambda w: pl.BlockSpec((tm, w), lambda i: (i, 0))
    full = lambda a: pl.BlockSpec(a.shape, lambda i: (0,) * a.ndim)
    out_dt = {"qa": BF16, "ka": BF16, "va": BF16, "qb": BF16, "kc": F32, "vc": F32,
              "ks": BF16, "vs": BF16, "kw": BF16, "vw": BF16}
    names = [nm for nm, _ in _IN_SPLITS[:-1]]
    widths = dict(_IN_SPLITS)
    out_shape = [jax.ShapeDtypeStruct((n, widths[nm]), out_dt[nm]) for nm in names]
    out_shape.append(jax.ShapeDtypeStruct((n, LANES), F32))
    out_specs = [row(widths[nm]) for nm in names] + [row(LANES)]
    res = pl.pallas_call(
        _inproj_kernel,
        grid=(n // tm,),
        in_specs=[row(d), row(1), full(g), full(w_pad), full(invf)],
        out_specs=out_specs,
        out_shape=out_shape,
        compiler_params=_params("parallel"),
        name="inproj",
    )(x2, pos2, g, w_pad, invf)
    out = dict(zip(names, res[:-1]))
    out["gates"] = res[-1]
    return out


def _compress_kernel(kc_ref, vc_ref, kpos_ref, kw1_ref, kw2_ref, vpos_ref, vw1_ref, vw2_ref, ko_ref, vo_ref):
    ncp = ko_ref.shape[0]
    for src, pos_ref, w1_ref, w2_ref, out_ref in ((kc_ref, kpos_ref, kw1_ref, kw2_ref, ko_ref),
                                                  (vc_ref, vpos_ref, vw1_ref, vw2_ref, vo_ref)):
        bias = jnp.dot(pos_ref[...].astype(BF16), w1_ref[...], preferred_element_type=F32)
        acc_a = [jnp.zeros((ncp, CMP_HIDDEN), F32) for _ in range(2)]
        acc_b = [jnp.zeros((ncp, CMP_HIDDEN), F32) for _ in range(2)]
        for j in range(CMP_STRIDE):
            pj = src[pl.ds(j, ncp, stride=CMP_STRIDE), :].astype(BF16)
            wa = w1_ref[j * HEAD_DIM:(j + 1) * HEAD_DIM, :]
            wb = w1_ref[(CMP_STRIDE + j) * HEAD_DIM:(CMP_STRIDE + j + 1) * HEAD_DIM, :]
            for hh in range(2):
                ph = pj[:, hh * HEAD_DIM:(hh + 1) * HEAD_DIM]
                acc_a[hh] = acc_a[hh] + jnp.dot(ph, wa, preferred_element_type=F32)
                acc_b[hh] = acc_b[hh] + jnp.dot(ph, wb, preferred_element_type=F32)
        for hh in range(2):
            h1 = acc_a[hh] + pltpu.roll(acc_b[hh], ncp - 1, 0) + bias
            out = jnp.dot(_gelu(h1).astype(BF16), w2_ref[...], preferred_element_type=F32)
            out_ref[:, hh * HEAD_DIM:(hh + 1) * HEAD_DIM] = out


def _compress(kc3, vc3, kpos, kw1, kw2, vpos, vw1, vw2):
    b, t, w = kc3.shape
    ncp = t // CMP_STRIDE
    per_b = pl.BlockSpec((None, t, w), lambda i: (i, 0, 0))
    full = lambda a: pl.BlockSpec(a.shape, lambda i: (0,) * a.ndim)
    out_spec = pl.BlockSpec((None, ncp, w), lambda i: (i, 0, 0))
    return pl.pallas_call(
        _compress_kernel,
        grid=(b,),
        in_specs=[per_b, per_b, full(kpos), full(kw1), full(kw2), full(vpos), full(vw1), full(vw2)],
        out_specs=[out_spec, out_spec],
        out_shape=[jax.ShapeDtypeStruct((b, ncp, w), F32)] * 2,
        compiler_params=_params("parallel"),
        name="compress",
    )(kc3, vc3, kpos, kw1, kw2, vpos, vw1, vw2)


def _cmpsel_kernel(q_ref, kc_ref, vc_ref, ovt_ref, o_ref, sel_ref, *, tq, n_cmp):
    i = pl.program_id(1)
    t0 = i * tq
    g = NSA_HEADS // NSA_KV_HEADS
    ncp = kc_ref.shape[0]
    ns = ovt_ref.shape[0]
    rows = g * tq
    tpos = t0 + (lax.broadcasted_iota(I32, (rows, ncp), 0) & (tq - 1))
    cidx = lax.broadcasted_iota(I32, (rows, ncp), 1)
    vis = (cidx * CMP_STRIDE + (CMP_LEN - 1) <= tpos) & (cidx < n_cmp)
    jidx = lax.broadcasted_iota(I32, (ns, tq), 0)
    tcol = t0 + lax.broadcasted_iota(I32, (ns, tq), 1)
    cur = tcol // SEL_BLOCK
    forced = (jidx == 0) | (jidx == cur) | (jidx == cur - 1)
    valid = jidx * SEL_BLOCK <= tcol
    for hk in range(NSA_KV_HEADS):
        kk = kc_ref[:, hk * HEAD_DIM:(hk + 1) * HEAD_DIM].astype(BF16)
        vv = vc_ref[:, hk * HEAD_DIM:(hk + 1) * HEAD_DIM].astype(BF16)
        q4 = jnp.concatenate(
            [q_ref[:, (hk * g + gi) * HEAD_DIM:(hk * g + gi + 1) * HEAD_DIM] for gi in range(g)], axis=0)
        s = lax.dot_general(q4, kk, (((1,), (1,)), ((), ())), preferred_element_type=F32) * (HEAD_DIM ** -0.5)
        s = jnp.where(vis, s, NEG_BIG)
        m = jnp.max(s, axis=-1, keepdims=True)
        e = jnp.exp(s - m)
        p = jnp.where(vis, e / jnp.sum(e, axis=-1, keepdims=True), 0.0)
        o = jnp.dot(p.astype(BF16), vv, preferred_element_type=F32)
        for gi in range(g):
            h = hk * g + gi
            o_ref[:, h * HEAD_DIM:(h + 1) * HEAD_DIM] = o[gi * tq:(gi + 1) * tq]
        psum = p[0:tq]
        for gi in range(1, g):
            psum = psum + p[gi * tq:(gi + 1) * tq]
        imp_t = lax.dot_general(ovt_ref[...], psum, (((1,), (1,)), ((), ())),
                                precision=lax.Precision.HIGHEST, preferred_element_type=F32)
        score = jnp.where(forced, jnp.inf, jnp.where(valid, imp_t, -jnp.inf))
        cnt = jnp.zeros((ns, tq), F32)
        for r in range(ns):
            row = score[r:r + 1, :]
            beats = (row > score) | ((row >= score) & (jidx > r))
            cnt = cnt + jnp.where(beats, 1.0, 0.0)
        sel_ref[hk] = jnp.where((cnt < SEL_TOPN) & (score > -jnp.inf), 1.0, 0.0)


def _cmpsel(qb3, kcmp, vcmp, ovt, tq, n_cmp):
    b, t, w = qb3.shape
    ncp = kcmp.shape[1]
    ns = ovt.shape[0]
    kern = functools.partial(_cmpsel_kernel, tq=tq, n_cmp=n_cmp)
    return pl.pallas_call(
        kern,
        grid=(b, t // tq),
        in_specs=[pl.BlockSpec((None, tq, w), lambda bi, i: (bi, i, 0)),
                  pl.BlockSpec((None, ncp, LANES), lambda bi, i: (bi, 0, 0)),
                  pl.BlockSpec((None, ncp, LANES), lambda bi, i: (bi, 0, 0)),
                  pl.BlockSpec(ovt.shape, lambda bi, i: (0, 0))],
        out_specs=[pl.BlockSpec((None, tq, w), lambda bi, i: (bi, i, 0)),
                   pl.BlockSpec((None, NSA_KV_HEADS, ns, tq), lambda bi, i: (bi, 0, 0, i))],
        out_shape=[jax.ShapeDtypeStruct((b, t, w), F32), jax.ShapeDtypeStruct((b, NSA_KV_HEADS, ns, t), F32)],
        compiler_params=_params("parallel", "parallel"),
        name="cmpsel",
    )(qb3, kcmp, vcmp, ovt)


def _selattn_kernel(q_ref, k_ref, vt_ref, sel_ref, o_ref, m_sc, l_sc, acc_sc, *, tq, tk):
    i = pl.program_id(1)
    t0 = i * tq
    g = NSA_HEADS // NSA_KV_HEADS
    rows = g * tq
    assert tk % tq == 0, "one key tile must cover the whole diagonal of a query tile"
    n_below = (i * tq) // tk
    nb = tk // SEL_BLOCK
    qpos = t0 + (lax.broadcasted_iota(I32, (tk, rows), 1) & (tq - 1))
    krow = lax.broadcasted_iota(I32, (tk, rows), 0)
    for hk in range(NSA_KV_HEADS):
        qt = jnp.concatenate(
            [jnp.transpose(q_ref[:, (hk * g + gi) * HEAD_DIM:(hk * g + gi + 1) * HEAD_DIM].astype(F32))
             for gi in range(g)], axis=1)
        qt = (qt * (HEAD_DIM ** -0.5)).astype(BF16)
        m_sc[...] = jnp.full(m_sc.shape, NEG_BIG, F32)
        l_sc[...] = jnp.zeros(l_sc.shape, F32)
        acc_sc[...] = jnp.zeros(acc_sc.shape, F32)

        def tile(kt, diagonal):
            ks = pl.multiple_of(kt * tk, tk)
            kk = k_ref[pl.ds(ks, tk), hk * HEAD_DIM:(hk + 1) * HEAD_DIM]
            s = jnp.dot(kk, qt, preferred_element_type=F32)
            bias = jnp.concatenate(
                [jnp.broadcast_to(
                    jnp.concatenate([(1.0 - sel_ref[hk, pl.ds(kt * nb + jj, 1), :]) * NEG_BIG] * g, axis=1),
                    (SEL_BLOCK, rows)) for jj in range(nb)], axis=0)
            s = s + bias
            if diagonal:
                s = jnp.where(ks + krow <= qpos, s, NEG_BIG)
            m_old = m_sc[...]
            m_new = jnp.maximum(m_old, jnp.max(s, axis=0, keepdims=True))
            alpha = jnp.exp(m_old - m_new)
            p = jnp.exp(s - m_new)
            l_sc[...] = alpha * l_sc[...] + jnp.sum(p, axis=0, keepdims=True)
            vt = vt_ref[kt, hk * HEAD_DIM:(hk + 1) * HEAD_DIM, :]
            acc_sc[...] = alpha * acc_sc[...] + jnp.dot(vt, p.astype(BF16), preferred_element_type=F32)
            m_sc[...] = m_new

        def below_diagonal(kt, carry):
            tile(kt, False)
            return carry

        lax.fori_loop(0, n_below, below_diagonal, 0)
        tile(n_below, True)
        o_t = acc_sc[...] / l_sc[...]
        for gi in range(g):
            h = hk * g + gi
            o_ref[:, h * HEAD_DIM:(h + 1) * HEAD_DIM] = jnp.transpose(o_t[:, gi * tq:(gi + 1) * tq])


def _selattn(qb3, ks3, vst4, sel4, tq, tk):
    b, t, w = qb3.shape
    g = NSA_HEADS // NSA_KV_HEADS
    ns = sel4.shape[2]
    kern = functools.partial(_selattn_kernel, tq=tq, tk=tk)
    return pl.pallas_call(
        kern,
        grid=(b, t // tq),
        in_specs=[pl.BlockSpec((None, tq, w), lambda bi, i: (bi, i, 0)),
                  pl.BlockSpec((None, t, LANES), lambda bi, i: (bi, 0, 0)),
                  pl.BlockSpec((None, t // tk, LANES, tk), lambda bi, i: (bi, 0, 0, 0)),
                  pl.BlockSpec((None, NSA_KV_HEADS, ns, tq), lambda bi, i: (bi, 0, 0, i))],
        out_specs=pl.BlockSpec((None, tq, w), lambda bi, i: (bi, i, 0)),
        out_shape=jax.ShapeDtypeStruct((b, t, w), F32),
        scratch_shapes=[pltpu.VMEM((1, g * tq), F32), pltpu.VMEM((1, g * tq), F32),
                        pltpu.VMEM((HEAD_DIM, g * tq), F32)],
        compiler_params=_params("parallel", "arbitrary"),
        name="selattn",
    )(qb3, ks3, vst4, sel4)


def _band_kernel(*refs, window, tq, span, n_heads, n_kv, has_sink):
    if has_sink:
        sink_ref, q_ref, k_ref, vt_ref, o_ref = refs
    else:
        q_ref, k_ref, vt_ref, o_ref = refs
    i = pl.program_id(1)
    t0 = i * tq
    start = pl.multiple_of(jnp.maximum(t0 + tq - span, 0), tq)
    tile0 = start // LANES
    g = n_heads // n_kv
    rows = g * tq
    qpos = t0 + (lax.broadcasted_iota(I32, (span, rows), 1) & (tq - 1))
    kpos = start + lax.broadcasted_iota(I32, (span, rows), 0)
    diff = qpos - kpos
    mask = (diff >= 0) & (diff < window)
    for hk in range(n_kv):
        kk = k_ref[pl.ds(start, span), hk * HEAD_DIM:(hk + 1) * HEAD_DIM]
        qt = jnp.concatenate(
            [jnp.transpose(q_ref[:, (hk * g + gi) * HEAD_DIM:(hk * g + gi + 1) * HEAD_DIM].astype(F32))
             for gi in range(g)], axis=1)
        qt = (qt * (HEAD_DIM ** -0.5)).astype(BF16)
        s = jnp.where(mask, jnp.dot(kk, qt, preferred_element_type=F32), NEG_BIG)
        m = jnp.max(s, axis=0, keepdims=True)
        if has_sink:
            sink = jnp.concatenate([jnp.full((1, tq), sink_ref[hk * g + gi], F32) for gi in range(g)], axis=1)
            m = jnp.maximum(m, sink)
        p = jnp.exp(s - m)
        l = jnp.sum(p, axis=0, keepdims=True)
        if has_sink:
            l = l + jnp.exp(sink - m)
        vt = jnp.concatenate([vt_ref[tile0 + j, hk * HEAD_DIM:(hk + 1) * HEAD_DIM, :]
                              for j in range(span // LANES)], axis=1)
        o_t = jnp.dot(vt, p.astype(BF16), preferred_element_type=F32) / l
        for gi in range(g):
            h = hk * g + gi
            o_ref[:, h * HEAD_DIM:(h + 1) * HEAD_DIM] = jnp.transpose(o_t[:, gi * tq:(gi + 1) * tq])


def _band(q3, k3, v3, sinks, window, tq, n_heads, n_kv):
    b, t, w = q3.shape
    nprev = -(-(window - 1) // tq)
    span = min((nprev + 1) * tq, t)
    has_sink = sinks is not None
    vt4 = jnp.swapaxes(v3.reshape(b, t // LANES, LANES, LANES), 2, 3)
    kern = functools.partial(_band_kernel, window=window, tq=tq, span=span, n_heads=n_heads, n_kv=n_kv,
                             has_sink=has_sink)
    in_specs = [pl.BlockSpec((None, tq, w), lambda bi, i: (bi, i, 0)),
                pl.BlockSpec((None, t, LANES), lambda bi, i: (bi, 0, 0)),
                pl.BlockSpec((None, t // LANES, LANES, LANES), lambda bi, i: (bi, 0, 0, 0))]
    args = [q3, k3, vt4]
    if has_sink:
        in_specs = [pl.BlockSpec(memory_space=pltpu.SMEM)] + in_specs
        args = [sinks] + args
    return pl.pallas_call(
        kern,
        grid=(b, t // tq),
        in_specs=in_specs,
        out_specs=pl.BlockSpec((None, tq, w), lambda bi, i: (bi, i, 0)),
        out_shape=jax.ShapeDtypeStruct((b, t, w), F32),
        compiler_params=_params("parallel", "parallel"),
        name="band_sink" if has_sink else "band_win",
    )(*args)


def _first_max_rows(val, payload=None):
    rows, n = val.shape
    sub = lax.broadcasted_iota(I32, (SUBLANES, n), 0).astype(F32)
    parts = [[val[r:r + SUBLANES] for r in range(0, rows, SUBLANES)],
             [sub + float(r) for r in range(0, rows, SUBLANES)]]
    if payload is not None:
        parts.append([payload[r:r + SUBLANES] for r in range(0, rows, SUBLANES)])
    while len(parts[0]) > 1:
        nxt = [[] for _ in parts]
        for j in range(0, len(parts[0]) - 1, 2):
            take_right = parts[0][j + 1] > parts[0][j]
            for dst, src in zip(nxt, parts):
                dst.append(jnp.where(take_right, src[j + 1], src[j]))
        if len(parts[0]) % 2:
            for dst, src in zip(nxt, parts):
                dst.append(src[-1])
        parts = nxt
    v8, i8 = parts[0][0], parts[1][0]
    m = jnp.max(v8, axis=0, keepdims=True)
    first = jnp.min(jnp.where(v8 == m, i8, float(rows)), axis=0, keepdims=True)
    if payload is None:
        return m, first
    return m, first, jnp.max(jnp.where(i8 == first, parts[2][0], -1.0), axis=0, keepdims=True)


def _topk_rows(s, k, ridx):
    vals, idxs = [], []
    for _ in range(k):
        m, first = _first_max_rows(s)
        vals.append(m)
        idxs.append(first)
        s = jnp.where(ridx == first, -jnp.inf, s)
    return jnp.concatenate(vals, axis=0), jnp.concatenate(idxs, axis=0)


def _peer_candidates(t1, i1, t2, i2):
    vals = [t1[0:1] + t2]
    eids = [i1[0:1] * float(N_KEYS) + i2]
    for a in range(1, 8):
        vals.append(t1[a:a + 1] + t2[0:8])
        eids.append(i1[a:a + 1] * float(N_KEYS) + i2[0:8])
    vals.append(t1[8:16] + t2[0:1])
    eids.append(i1[8:16] * float(N_KEYS) + i2[0:1])
    return jnp.concatenate(vals, axis=0), jnp.concatenate(eids, axis=0)


def _route_kernel(x_ref, oa_ref, oc_ref, os_ref, ow_ref, gt_ref, gexp_ref, wo_ref, gf_ref, wq_ref,
                  k1_ref, k2_ref, x1_ref, eidx_ref, gate_ref):
    tm = x_ref.shape[0]
    gt = gt_ref[...]
    g_hi = gt.astype(BF16)
    g_lo = (gt - g_hi.astype(F32)).astype(BF16)
    ob = jnp.zeros(oc_ref.shape, F32)
    for j, br in enumerate((oc_ref, os_ref, ow_ref)):
        ex = gexp_ref[j]
        gj = (jnp.dot(g_hi, ex, preferred_element_type=F32) + jnp.dot(g_lo, ex, preferred_element_type=F32))
        ob = ob + gj * br[...]
    half = oa_ref.shape[1]
    mixed = (jnp.dot(oa_ref[...].astype(BF16), wo_ref[0:half, :], preferred_element_type=F32)
             + jnp.dot(ob.astype(BF16), wo_ref[half:, :], preferred_element_type=F32))
    x1 = x_ref[...] + mixed
    x1_ref[...] = x1.reshape(tm, SUBLANES, LANES)
    h2 = x1 * lax.rsqrt(jnp.mean(x1 * x1, axis=-1, keepdims=True) + RMS_EPS) * gf_ref[...]
    q = jnp.dot(h2.astype(BF16), wq_ref[...], preferred_element_type=F32)
    hq = PEER_QDIM // 2
    ridx = lax.broadcasted_iota(I32, (N_KEYS, tm), 0).astype(F32)
    e_rows, g_rows = [], []
    for h in range(PEER_HEADS):
        q1 = q[:, h * PEER_QDIM: h * PEER_QDIM + hq].astype(BF16)
        q2 = q[:, h * PEER_QDIM + hq:(h + 1) * PEER_QDIM].astype(BF16)
        s1 = lax.dot_general(k1_ref[...], q1, (((1,), (1,)), ((), ())), preferred_element_type=F32)
        s2 = lax.dot_general(k2_ref[...], q2, (((1,), (1,)), ((), ())), preferred_element_type=F32)
        t1, i1 = _topk_rows(s1, PEER_TOPK, ridx)
        t2, i2 = _topk_rows(s2, PEER_TOPK, ridx)
        cand, cand_e = _peer_candidates(t1, i1, t2, i2)
        cidx = lax.broadcasted_iota(I32, cand.shape, 0).astype(F32)
        sc, ee = [], []
        for _ in range(PEER_TOPK):
            m, first, e_first = _first_max_rows(cand, cand_e)
            sc.append(m)
            ee.append(e_first)
            cand = jnp.where(cidx == first, -jnp.inf, cand)
        sc = jnp.concatenate(sc, axis=0)
        ex = jnp.exp(sc - sc[0:1])
        g_rows.append(ex / jnp.sum(ex, axis=0, keepdims=True))
        e_rows.append(jnp.concatenate(ee, axis=0))
    e_t = jnp.concatenate(e_rows, axis=0)
    g_t = jnp.concatenate(g_rows, axis=0)
    eidx_ref[...] = jnp.transpose(e_t).astype(I32)
    gate_ref[...] = jnp.transpose(g_t)


def _route(x2, oa, oc, osel, ow, gates, gexp, wo, gf, wq, k1, k2, tm):
    n, d = x2.shape
    row = lambda w: pl.BlockSpec((tm, w), lambda i: (i, 0))
    full = lambda a: pl.BlockSpec(a.shape, lambda i: (0,) * a.ndim)
    nk = PEER_HEADS * PEER_TOPK
    return pl.pallas_call(
        _route_kernel,
        grid=(n // tm,),
        in_specs=[row(d), row(oa.shape[1]), row(oc.shape[1]), row(osel.shape[1]), row(ow.shape[1]), row(LANES),
                  full(gexp), full(wo), full(gf), full(wq), full(k1), full(k2)],
        out_specs=[pl.BlockSpec((tm, SUBLANES, LANES), lambda i: (i, 0, 0)), row(nk), row(nk)],
        out_shape=[jax.ShapeDtypeStruct((n, SUBLANES, LANES), F32), jax.ShapeDtypeStruct((n, nk), I32),
                   jax.ShapeDtypeStruct((n, nk), F32)],
        compiler_params=_params("parallel"),
        name="route",
    )(x2, oa, oc, osel, ow, gates, gexp, wo, gf, wq, k1, k2)


def _fold8(p):
    sub = lax.broadcasted_iota(I32, (SUBLANES, LANES), 0)
    lo4 = sub < 4
    q = [jnp.where(lo4, p[j], p[j + 4]) + pltpu.roll(jnp.where(lo4, p[j + 4], p[j]), 4, 0) for j in range(4)]
    m2 = (sub & 3) < 2
    r = [jnp.where(m2, q[j] + pltpu.roll(q[j], 6, 0), q[j + 2] + pltpu.roll(q[j + 2], 2, 0)) for j in range(2)]
    m1 = (sub & 1) == 0
    return jnp.where(m1, r[0] + pltpu.roll(r[0], 7, 0), r[1] + pltpu.roll(r[1], 1, 0))


def _peer_kernel(e0_ref, e1_ref, e2_ref, x_ref, gate_ref, gffn_ref, gfin_ref, tab_ref, o_ref,
                 buf, sem, a_sc, cb_sc, y_sc, *, tb):
    step = pl.program_id(0)
    nsteps = pl.num_programs(0)
    slot = step % PEER_SLOTS
    fill = (step + PEER_SLOTS - 1) % PEER_SLOTS
    nk = PEER_HEADS * PEER_TOPK
    npair = tb * nk

    def issue_rows(eref, dst_slot, t, k0, k1):
        erow = eref.at[t]
        for k in range(k0, k1):
            copy = pltpu.make_async_copy(tab_ref.at[erow[k]], buf.at[dst_slot, t * nk + k], sem.at[dst_slot])
            copy.start(priority=k % 2)

    def wait_slot(s):
        pltpu.make_async_copy(tab_ref.at[pl.ds(0, npair)], buf.at[s], sem.at[s]).wait()

    @pl.when(step == 0)
    def _():
        def first(t, carry):
            issue_rows(e0_ref, 0, t, 0, nk)
            return carry
        lax.fori_loop(0, tb, first, 0)
        def second(t, carry):
            issue_rows(e1_ref, 1, t, 0, nk)
            return carry
        lax.fori_loop(0, tb, second, 0)

    eye = jnp.where(lax.broadcasted_iota(I32, (nk, nk), 0) == lax.broadcasted_iota(I32, (nk, nk), 1),
                    1.0, 0.0).astype(BF16)
    gr = gate_ref[...]
    g0 = gr.astype(BF16)
    r1 = gr - g0.astype(F32)
    g1 = r1.astype(BF16)
    g2 = (r1 - g1.astype(F32)).astype(BF16)
    dn = (((1,), (1,)), ((), ()))
    gate_t = (lax.dot_general(eye, g0, dn, preferred_element_type=F32)
              + lax.dot_general(eye, g1, dn, preferred_element_type=F32)
              + lax.dot_general(eye, g2, dn, preferred_element_type=F32))

    wait_slot(slot)

    lane_tb = lax.broadcasted_iota(I32, (nk, tb), 1)
    inv_d = 1.0 / (SUBLANES * LANES)
    a_sc[...] = jnp.zeros(a_sc.shape, F32)

    ngroup = nk // SUBLANES
    rows_p1 = 6
    rows_p2 = SUBLANES - rows_p1

    def dots(t, carry):
        x8 = x_ref[t]
        ms = jnp.sum(jnp.sum(x8 * x8, axis=1, keepdims=True), axis=0, keepdims=True) * inv_d
        h8 = x8 * lax.rsqrt(ms + RMS_EPS) * gffn_ref[...]
        base = t * nk
        folded = []
        for gi in range(nk // SUBLANES):
            blk = buf[slot, pl.ds(base + gi * SUBLANES, SUBLANES), 0:SUBLANES, :]
            folded.append(_fold8([blk[j] * h8 for j in range(SUBLANES)]))
            issue_rows(e2_ref, fill, t, gi * rows_p1, (gi + 1) * rows_p1)
        a = jnp.sum(jnp.concatenate(folded, axis=0), axis=1, keepdims=True)
        a_sc[...] = jnp.where(lane_tb == t, a, a_sc[...])
        return carry

    lax.fori_loop(0, tb, dots, 0, unroll=2)

    a_sc[...] = gate_t * _gelu(a_sc[...])

    def combine(t, carry):
        base = t * nk
        w_col = jnp.sum(jnp.where(lane_tb == t, a_sc[...], 0.0), axis=1, keepdims=True)
        cb_sc[t] = jnp.broadcast_to(w_col, (nk, LANES))
        accs = [jnp.zeros((SUBLANES, LANES), F32) for _ in range(4)]
        for gi in range(ngroup):
            for k in range(gi * SUBLANES, (gi + 1) * SUBLANES):
                ck = jnp.broadcast_to(cb_sc[t, k:k + 1, :], (SUBLANES, LANES))
                accs[k % 4] = accs[k % 4] + ck * buf[slot, base + k, SUBLANES:2 * SUBLANES, :]
            issue_rows(e2_ref, fill, t, ngroup * rows_p1 + gi * rows_p2, ngroup * rows_p1 + (gi + 1) * rows_p2)
        y8 = x_ref[t] + ((accs[0] + accs[1]) + (accs[2] + accs[3]))
        ms2 = jnp.sum(jnp.sum(y8 * y8, axis=1, keepdims=True), axis=0, keepdims=True) * inv_d
        y_sc[t] = y8 * lax.rsqrt(ms2 + RMS_EPS) * gfin_ref[...]
        return carry

    lax.fori_loop(0, tb, combine, 0, unroll=4)
    o_ref[...] = y_sc[...].reshape(tb, SUBLANES * LANES)

    @pl.when(step == nsteps - 1)
    def _():
        for ahead in range(1, PEER_SLOTS):
            wait_slot((step + ahead) % PEER_SLOTS)


def _peer(eidx, x1s, gate, gffn8, gfin8, table, tb):
    n = x1s.shape[0]
    nk = PEER_HEADS * PEER_TOPK
    nsteps = n // tb
    kern = functools.partial(_peer_kernel, tb=tb)
    return pl.pallas_call(
        kern,
        grid=(nsteps,),
        in_specs=[pl.BlockSpec((tb, nk), lambda i: (i, 0), memory_space=pltpu.SMEM),
                  pl.BlockSpec((tb, nk), lambda i: ((i + 1) % nsteps, 0), memory_space=pltpu.SMEM),
                  pl.BlockSpec((tb, nk), lambda i: ((i + PEER_SLOTS - 1) % nsteps, 0), memory_space=pltpu.SMEM),
                  pl.BlockSpec((tb, SUBLANES, LANES), lambda i: (i, 0, 0)),
                  pl.BlockSpec((tb, nk), lambda i: (i, 0)),
                  pl.BlockSpec((SUBLANES, LANES), lambda i: (0, 0)),
                  pl.BlockSpec((SUBLANES, LANES), lambda i: (0, 0)),
                  pl.BlockSpec(memory_space=pl.ANY)],
        out_specs=pl.BlockSpec((tb, SUBLANES * LANES), lambda i: (i, 0)),
        out_shape=jax.ShapeDtypeStruct((n, SUBLANES * LANES), F32),
        scratch_shapes=[pltpu.VMEM((PEER_SLOTS, tb * nk, 2 * SUBLANES, LANES), F32),
                        pltpu.SemaphoreType.DMA((PEER_SLOTS,)),
                        pltpu.VMEM((nk, tb), F32),
                        pltpu.VMEM((tb, nk, LANES), F32),
                        pltpu.VMEM((tb, SUBLANES, LANES), F32)],
        compiler_params=_params("arbitrary"),
        name="peer",
    )(eidx, eidx, eidx, x1s, gate, gffn8, gfin8, table)


def _overlap_t(t):
    nc = (t - CMP_LEN) // CMP_STRIDE + 1
    ns = t // SEL_BLOCK
    ncp = t // CMP_STRIDE
    cstart = np.arange(nc) * CMP_STRIDE
    bstart = np.arange(ns) * SEL_BLOCK
    lo = np.maximum(cstart[:, None], bstart[None, :])
    hi = np.minimum(cstart[:, None] + CMP_LEN, bstart[None, :] + SEL_BLOCK)
    ov = np.clip(hi - lo, 0, None).astype(np.float32) / CMP_LEN
    out = np.zeros((ns, ncp), np.float32)
    out[:, :nc] = ov.T
    return out, nc


def _gate_expand():
    ex = np.zeros((NSA_N_GATES, LANES, NSA_HEADS * HEAD_DIM), np.float32)
    for j in range(NSA_N_GATES):
        for h in range(NSA_HEADS):
            ex[j, h * NSA_N_GATES + j, h * HEAD_DIM:(h + 1) * HEAD_DIM] = 1.0
    return ex


def _layer(x, positions, norm_attn, w_in, attn_sinks, cmp_k_pos, cmp_k_w1, cmp_k_w2, cmp_v_pos, cmp_v_w1,
           cmp_v_w2, w_out, norm_ffn, w_query, sub_keys_1, sub_keys_2, expert_down, expert_up, norm_out,
           tm_in=512, tq_cmp=256, tq_sel=256, tk_sel=512, tq_swa=128, tq_win=256, tm_route=256, tb_peer=16):
    b, t, d = x.shape
    n = b * t
    x2 = x.reshape(n, d)
    half = HEAD_DIM // 2
    inv_freq = ROPE_THETA ** (-jnp.arange(half, dtype=F32) / half)
    invf = jnp.tile(inv_freq, LANES // half).reshape(1, LANES)
    w_pad = jnp.pad(w_in, ((0, 0), (0, IN_PAD - IN_WIDTH))).astype(BF16)
    pr = _inproj(x2, positions.reshape(n, 1), norm_attn.reshape(1, d), w_pad, invf, min(tm_in, n))
    r3 = lambda a: a.reshape(b, t, a.shape[-1])

    o_a = _band(r3(pr["qa"]), r3(pr["ka"]), r3(pr["va"]), attn_sinks.astype(F32), SWA_WINDOW, tq_swa,
                SWA_HEADS, SWA_KV_HEADS)
    kcmp, vcmp = _compress(r3(pr["kc"]), r3(pr["vc"]),
                           cmp_k_pos.reshape(1, -1), cmp_k_w1.astype(BF16), cmp_k_w2.astype(BF16),
                           cmp_v_pos.reshape(1, -1), cmp_v_w1.astype(BF16), cmp_v_w2.astype(BF16))
    ovt, n_cmp = _overlap_t(t)
    qb3 = r3(pr["qb"])
    o_cmp, sel = _cmpsel(qb3, kcmp, vcmp, jnp.asarray(ovt), tq_cmp, n_cmp)
    tk_sel = min(tk_sel, t)
    vst4 = jnp.swapaxes(r3(pr["vs"]).reshape(b, t // tk_sel, tk_sel, LANES), 2, 3)
    o_sel = _selattn(qb3, r3(pr["ks"]), vst4, sel, tq_sel, tk_sel)
    o_win = _band(qb3, r3(pr["kw"]), r3(pr["vw"]), None, NSA_WINDOW, tq_win, NSA_HEADS, NSA_KV_HEADS)

    x1, eidx, gate = _route(x2, o_a.reshape(n, -1), o_cmp.reshape(n, -1), o_sel.reshape(n, -1),
                            o_win.reshape(n, -1), pr["gates"], jnp.asarray(_gate_expand(), BF16),
                            w_out.astype(BF16), norm_ffn.reshape(1, d), w_query.astype(BF16),
                            sub_keys_1.astype(BF16), sub_keys_2.astype(BF16), min(tm_route, n))
    ne = expert_down.shape[0]
    table = jnp.stack([expert_down.reshape(ne, SUBLANES, LANES), expert_up.reshape(ne, SUBLANES, LANES)],
                      axis=1).reshape(ne, 2 * SUBLANES, LANES)
    y = _peer(eidx, x1, gate, norm_ffn.reshape(SUBLANES, LANES),
              norm_out.reshape(SUBLANES, LANES), table, tb_peer)
    return y.reshape(b, t, d)


def kernel(x, positions, norm_attn, w_in, attn_sinks, cmp_k_pos, cmp_k_w1, cmp_k_w2, cmp_v_pos, cmp_v_w1,
           cmp_v_w2, w_out, norm_ffn, peer_w_query, peer_sub_keys_1, peer_sub_keys_2, peer_expert_down,
           peer_expert_up, norm_f):
    assert norm_attn.shape[0] == 1, "single-layer block"
    return _layer(x, positions, norm_attn[0], w_in[0], attn_sinks[0], cmp_k_pos[0], cmp_k_w1[0], cmp_k_w2[0],
                  cmp_v_pos[0], cmp_v_w1[0], cmp_v_w2[0], w_out[0], norm_ffn[0], peer_w_query[0],
                  peer_sub_keys_1[0], peer_sub_keys_2[0], peer_expert_down[0], peer_expert_up[0], norm_f)
```

```python
import functools

import numpy as np
import jax
import jax.numpy as jnp
from jax import lax
from jax.experimental import pallas as pl
from jax.experimental.pallas import tpu as pltpu

F32 = jnp.float32
BF16 = jnp.bfloat16
I32 = jnp.int32

HEAD_DIM = 64
ROPE_THETA = 10000.0
RMS_EPS = 1e-6
NEG_BIG = -1e30

SWA_HEADS = 8
SWA_KV_HEADS = 2
SWA_WINDOW = 128

NSA_HEADS = 8
NSA_KV_HEADS = 2
CMP_LEN = 32
CMP_STRIDE = 16
CMP_HIDDEN = 256
SEL_BLOCK = 64
SEL_TOPN = 16
NSA_WINDOW = 512
NSA_N_GATES = 3

PEER_HEADS = 8
N_KEYS = 128
PEER_QDIM = 256
PEER_TOPK = 16
PEER_SLOTS = 3

LANES = 128
SUBLANES = 8
VMEM_LIMIT = 56 * 1024 * 1024

_IN_SPLITS = (("qa", 512), ("ka", 128), ("va", 128), ("qb", 512), ("kc", 128), ("vc", 128),
              ("ks", 128), ("vs", 128), ("kw", 128), ("vw", 128), ("gl", 24))
IN_WIDTH = sum(w for _, w in _IN_SPLITS)
IN_PAD = -(-IN_WIDTH // LANES) * LANES
_IN_OFF = {}
_o = 0
for _n, _w in _IN_SPLITS:
    _IN_OFF[_n] = _o
    _o += _w
_ROPED = ("qa", "ka", "qb", "kc", "ks", "kw")


def _gelu(x):
    return 0.5 * x * (1.0 + lax.erf(x * 0.7071067811865476))


def _params(*sem):
    return pltpu.CompilerParams(dimension_semantics=sem, vmem_limit_bytes=VMEM_LIMIT)


def _inproj_kernel(x_ref, pos_ref, g_ref, w_ref, invf_ref,
                   qa_ref, ka_ref, va_ref, qb_ref, kc_ref, vc_ref, ks_ref, vs_ref, kw_ref, vw_ref, gt_ref):
    x = x_ref[...]
    h = x * lax.rsqrt(jnp.mean(x * x, axis=-1, keepdims=True) + RMS_EPS) * g_ref[...]
    proj = jnp.dot(h.astype(BF16), w_ref[...], preferred_element_type=F32)
    ang = pos_ref[...].astype(F32) * invf_ref[...]
    lane = lax.broadcasted_iota(I32, ang.shape, 1)
    lo = (lane & (HEAD_DIM - 1)) < (HEAD_DIM // 2)
    cos = jnp.cos(ang)
    sin = jnp.sin(ang)
    sin_s = jnp.where(lo, -sin, sin)
    outs = {"qa": qa_ref, "ka": ka_ref, "va": va_ref, "qb": qb_ref, "kc": kc_ref, "vc": vc_ref,
            "ks": ks_ref, "vs": vs_ref, "kw": kw_ref, "vw": vw_ref}
    for name, width in _IN_SPLITS[:-1]:
        off = _IN_OFF[name]
        ref = outs[name]
        for i in range(width // LANES):
            v = proj[:, off + LANES * i: off + LANES * (i + 1)]
            if name in _ROPED:
                rot = jnp.where(lo, pltpu.roll(v, LANES - HEAD_DIM // 2, 1), pltpu.roll(v, HEAD_DIM // 2, 1))
                v = v * cos + rot * sin_s
            ref[:, LANES * i: LANES * (i + 1)] = v.astype(ref.dtype)
    gl = proj[:, _IN_OFF["gl"]: _IN_OFF["gl"] + LANES]
    gt_ref[...] = jax.nn.sigmoid(gl)


def _inproj(x2, pos2, g, w_pad, invf, tm):
    n, d = x2.shape
    row = lambda w: pl.BlockSpec((tm, w), lambda i: (i, 0))
    full = lambda a: pl.BlockSpec(a.shape, lambda i: (0,) * a.ndim)
    out_dt = {"qa": BF16, "ka": BF16, "va": BF16, "qb": BF16, "kc": F32, "vc": F32,
              "ks": BF16, "vs": BF16, "kw": BF16, "vw": BF16}
    names = [nm for nm, _ in _IN_SPLITS[:-1]]
    widths = dict(_IN_SPLITS)
    out_shape = [jax.ShapeDtypeStruct((n, widths[nm]), out_dt[nm]) for nm in names]
    out_shape.append(jax.ShapeDtypeStruct((n, LANES), F32))
    out_specs = [row(widths[nm]) for nm in names] + [row(LANES)]
    res = pl.pallas_call(
        _inproj_kernel,
        grid=(n // tm,),
        in_specs=[row(d), row(1), full(g), full(w_pad), full(invf)],
        out_specs=out_specs,
        out_shape=out_shape,
        compiler_params=_params("parallel"),
        name="inproj",
    )(x2, pos2, g, w_pad, invf)
    out = dict(zip(names, res[:-1]))
    out["gates"] = res[-1]
    return out


def _compress_kernel(kc_ref, vc_ref, kpos_ref, kw1_ref, kw2_ref, vpos_ref, vw1_ref, vw2_ref, ko_ref, vo_ref):
    ncp = ko_ref.shape[0]
    for src, pos_ref, w1_ref, w2_ref, out_ref in ((kc_ref, kpos_ref, kw1_ref, kw2_ref, ko_ref),
                                                  (vc_ref, vpos_ref, vw1_ref, vw2_ref, vo_ref)):
        bias = jnp.dot(pos_ref[...].astype(BF16), w1_ref[...], preferred_element_type=F32)
        acc_a = [jnp.zeros((ncp, CMP_HIDDEN), F32) for _ in range(2)]
        acc_b = [jnp.zeros((ncp, CMP_HIDDEN), F32) for _ in range(2)]
        for j in range(CMP_STRIDE):
            pj = src[pl.ds(j, ncp, stride=CMP_STRIDE), :].astype(BF16)
            wa = w1_ref[j * HEAD_DIM:(j + 1) * HEAD_DIM, :]
            wb = w1_ref[(CMP_STRIDE + j) * HEAD_DIM:(CMP_STRIDE + j + 1) * HEAD_DIM, :]
            for hh in range(2):
                ph = pj[:, hh * HEAD_DIM:(hh + 1) * HEAD_DIM]
                acc_a[hh] = acc_a[hh] + jnp.dot(ph, wa, preferred_element_type=F32)
                acc_b[hh] = acc_b[hh] + jnp.dot(ph, wb, preferred_element_type=F32)
        for hh in range(2):
            h1 = acc_a[hh] + pltpu.roll(acc_b[hh], ncp - 1, 0) + bias
            out = jnp.dot(_gelu(h1).astype(BF16), w2_ref[...], preferred_element_type=F32)
            out_ref[:, hh * HEAD_DIM:(hh + 1) * HEAD_DIM] = out


def _compress(kc3, vc3, kpos, kw1, kw2, vpos, vw1, vw2):
    b, t, w = kc3.shape
    ncp = t // CMP_STRIDE
    per_b = pl.BlockSpec((None, t, w), lambda i: (i, 0, 0))
    full = lambda a: pl.BlockSpec(a.shape, lambda i: (0,) * a.ndim)
    out_spec = pl.BlockSpec((None, ncp, w), lambda i: (i, 0, 0))
    return pl.pallas_call(
        _compress_kernel,
        grid=(b,),
        in_specs=[per_b, per_b, full(kpos), full(kw1), full(kw2), full(vpos), full(vw1), full(vw2)],
        out_specs=[out_spec, out_spec],
        out_shape=[jax.ShapeDtypeStruct((b, ncp, w), F32)] * 2,
        compiler_params=_params("parallel"),
        name="compress",
    )(kc3, vc3, kpos, kw1, kw2, vpos, vw1, vw2)


def _cmpsel_kernel(q_ref, kc_ref, vc_ref, ovt_ref, o_ref, sel_ref, *, tq, n_cmp):
    i = pl.program_id(1)
    t0 = i * tq
    g = NSA_HEADS // NSA_KV_HEADS
    ncp = kc_ref.shape[0]
    ns = ovt_ref.shape[0]
    rows = g * tq
    tpos = t0 + (lax.broadcasted_iota(I32, (rows, ncp), 0) & (tq - 1))
    cidx = lax.broadcasted_iota(I32, (rows, ncp), 1)
    vis = (cidx * CMP_STRIDE + (CMP_LEN - 1) <= tpos) & (cidx < n_cmp)
    jidx = lax.broadcasted_iota(I32, (ns, tq), 0)
    tcol = t0 + lax.broadcasted_iota(I32, (ns, tq), 1)
    cur = tcol // SEL_BLOCK
    forced = (jidx == 0) | (jidx == cur) | (jidx == cur - 1)
    valid = jidx * SEL_BLOCK <= tcol
    for hk in range(NSA_KV_HEADS):
        kk = kc_ref[:, hk * HEAD_DIM:(hk + 1) * HEAD_DIM].astype(BF16)
        vv = vc_ref[:, hk * HEAD_DIM:(hk + 1) * HEAD_DIM].astype(BF16)
        q4 = jnp.concatenate(
            [q_ref[:, (hk * g + gi) * HEAD_DIM:(hk * g + gi + 1) * HEAD_DIM] for gi in range(g)], axis=0)
        s = lax.dot_general(q4, kk, (((1,), (1,)), ((), ())), preferred_element_type=F32) * (HEAD_DIM ** -0.5)
        s = jnp.where(vis, s, NEG_BIG)
        m = jnp.max(s, axis=-1, keepdims=True)
        e = jnp.exp(s - m)
        p = jnp.where(vis, e / jnp.sum(e, axis=-1, keepdims=True), 0.0)
        o = jnp.dot(p.astype(BF16), vv, preferred_element_type=F32)
        for gi in range(g):
            h = hk * g + gi
            o_ref[:, h * HEAD_DIM:(h + 1) * HEAD_DIM] = o[gi * tq:(gi + 1) * tq]
        psum = p[0:tq]
        for gi in range(1, g):
            psum = psum + p[gi * tq:(gi + 1) * tq]
        imp_t = lax.dot_general(ovt_ref[...], psum, (((1,), (1,)), ((), ())),
                                precision=lax.Precision.HIGHEST, preferred_element_type=F32)
        score = jnp.where(forced, jnp.inf, jnp.where(valid, imp_t, -jnp.inf))
        cnt = jnp.zeros((ns, tq), F32)
        for r in range(ns):
            row = score[r:r + 1, :]
            beats = (row > score) | ((row >= score) & (jidx > r))
            cnt = cnt + jnp.where(beats, 1.0, 0.0)
        sel_ref[hk] = jnp.where((cnt < SEL_TOPN) & (score > -jnp.inf), 1.0, 0.0)


def _cmpsel(qb3, kcmp, vcmp, ovt, tq, n_cmp):
    b, t, w = qb3.shape
    ncp = kcmp.shape[1]
    ns = ovt.shape[0]
    kern = functools.partial(_cmpsel_kernel, tq=tq, n_cmp=n_cmp)
    return pl.pallas_call(
        kern,
        grid=(b, t // tq),
        in_specs=[pl.BlockSpec((None, tq, w), lambda bi, i: (bi, i, 0)),
                  pl.BlockSpec((None, ncp, LANES), lambda bi, i: (bi, 0, 0)),
                  pl.BlockSpec((None, ncp, LANES), lambda bi, i: (bi, 0, 0)),
                  pl.BlockSpec(ovt.shape, lambda bi, i: (0, 0))],
        out_specs=[pl.BlockSpec((None, tq, w), lambda bi, i: (bi, i, 0)),
                   pl.BlockSpec((None, NSA_KV_HEADS, ns, tq), lambda bi, i: (bi, 0, 0, i))],
        out_shape=[jax.ShapeDtypeStruct((b, t, w), F32), jax.ShapeDtypeStruct((b, NSA_KV_HEADS, ns, t), F32)],
        compiler_params=_params("parallel", "parallel"),
        name="cmpsel",
    )(qb3, kcmp, vcmp, ovt)


def _selattn_kernel(q_ref, k_ref, vt_ref, sel_ref, o_ref, m_sc, l_sc, acc_sc, *, tq, tk):
    i = pl.program_id(1)
    t0 = i * tq
    g = NSA_HEADS // NSA_KV_HEADS
    rows = g * tq
    assert tk % tq == 0, "one key tile must cover the whole diagonal of a query tile"
    n_below = (i * tq) // tk
    nb = tk // SEL_BLOCK
    qpos = t0 + (lax.broadcasted_iota(I32, (tk, rows), 1) & (tq - 1))
    krow = lax.broadcasted_iota(I32, (tk, rows), 0)
    for hk in range(NSA_KV_HEADS):
        qt = jnp.concatenate(
            [jnp.transpose(q_ref[:, (hk * g + gi) * HEAD_DIM:(hk * g + gi + 1) * HEAD_DIM].astype(F32))
             for gi in range(g)], axis=1)
        qt = (qt * (HEAD_DIM ** -0.5)).astype(BF16)
        m_sc[...] = jnp.full(m_sc.shape, NEG_BIG, F32)
        l_sc[...] = jnp.zeros(l_sc.shape, F32)
        acc_sc[...] = jnp.zeros(acc_sc.shape, F32)

        def tile(kt, diagonal):
            ks = pl.multiple_of(kt * tk, tk)
            kk = k_ref[pl.ds(ks, tk), hk * HEAD_DIM:(hk + 1) * HEAD_DIM]
            s = jnp.dot(kk, qt, preferred_element_type=F32)
            bias = jnp.concatenate(
                [jnp.broadcast_to(
                    jnp.concatenate([(1.0 - sel_ref[hk, pl.ds(kt * nb + jj, 1), :]) * NEG_BIG] * g, axis=1),
                    (SEL_BLOCK, rows)) for jj in range(nb)], axis=0)
            s = s + bias
            if diagonal:
                s = jnp.where(ks + krow <= qpos, s, NEG_BIG)
            m_old = m_sc[...]
            m_new = jnp.maximum(m_old, jnp.max(s, axis=0, keepdims=True))
            alpha = jnp.exp(m_old - m_new)
            p = jnp.exp(s - m_new)
            l_sc[...] = alpha * l_sc[...] + jnp.sum(p, axis=0, keepdims=True)
            vt = vt_ref[kt, hk * HEAD_DIM:(hk + 1) * HEAD_DIM, :]
            acc_sc[...] = alpha * acc_sc[...] + jnp.dot(vt, p.astype(BF16), preferred_element_type=F32)
            m_sc[...] = m_new

        def below_diagonal(kt, carry):
            tile(kt, False)
            return carry

        lax.fori_loop(0, n_below, below_diagonal, 0)
        tile(n_below, True)
        o_t = acc_sc[...] / l_sc[...]
        for gi in range(g):
            h = hk * g + gi
            o_ref[:, h * HEAD_DIM:(h + 1) * HEAD_DIM] = jnp.transpose(o_t[:, gi * tq:(gi + 1) * tq])


def _selattn(qb3, ks3, vst4, sel4, tq, tk):
    b, t, w = qb3.shape
    g = NSA_HEADS // NSA_KV_HEADS
    ns = sel4.shape[2]
    kern = functools.partial(_selattn_kernel, tq=tq, tk=tk)
    return pl.pallas_call(
        kern,
        grid=(b, t // tq),
        in_specs=[pl.BlockSpec((None, tq, w), lambda bi, i: (bi, i, 0)),
                  pl.BlockSpec((None, t, LANES), lambda bi, i: (bi, 0, 0)),
                  pl.BlockSpec((None, t // tk, LANES, tk), lambda bi, i: (bi, 0, 0, 0)),
                  pl.BlockSpec((None, NSA_KV_HEADS, ns, tq), lambda bi, i: (bi, 0, 0, i))],
        out_specs=pl.BlockSpec((None, tq, w), lambda bi, i: (bi, i, 0)),
        out_shape=jax.ShapeDtypeStruct((b, t, w), F32),
        scratch_shapes=[pltpu.VMEM((1, g * tq), F32), pltpu.VMEM((1, g * tq), F32),
                        pltpu.VMEM((HEAD_DIM, g * tq), F32)],
        compiler_params=_params("parallel", "arbitrary"),
        name="selattn",
    )(qb3, ks3, vst4, sel4)


def _band_kernel(*refs, window, tq, span, n_heads, n_kv, has_sink):
    if has_sink:
        sink_ref, q_ref, k_ref, vt_ref, o_ref = refs
    else:
        q_ref, k_ref, vt_ref, o_ref = refs
    i = pl.program_id(1)
    t0 = i * tq
    start = pl.multiple_of(jnp.maximum(t0 + tq - span, 0), tq)
    tile0 = start // LANES
    g = n_heads // n_kv
    rows = g * tq
    qpos = t0 + (lax.broadcasted_iota(I32, (span, rows), 1) & (tq - 1))
    kpos = start + lax.broadcasted_iota(I32, (span, rows), 0)
    diff = qpos - kpos
    mask = (diff >= 0) & (diff < window)
    for hk in range(n_kv):
        kk = k_ref[pl.ds(start, span), hk * HEAD_DIM:(hk + 1) * HEAD_DIM]
        qt = jnp.concatenate(
            [jnp.transpose(q_ref[:, (hk * g + gi) * HEAD_DIM:(hk * g + gi + 1) * HEAD_DIM].astype(F32))
             for gi in range(g)], axis=1)
        qt = (qt * (HEAD_DIM ** -0.5)).astype(BF16)
        s = jnp.where(mask, jnp.dot(kk, qt, preferred_element_type=F32), NEG_BIG)
        m = jnp.max(s, axis=0, keepdims=True)
        if has_sink:
            sink = jnp.concatenate([jnp.full((1, tq), sink_ref[hk * g + gi], F32) for gi in range(g)], axis=1)
            m = jnp.maximum(m, sink)
        p = jnp.exp(s - m)
        l = jnp.sum(p, axis=0, keepdims=True)
        if has_sink:
            l = l + jnp.exp(sink - m)
        vt = jnp.concatenate([vt_ref[tile0 + j, hk * HEAD_DIM:(hk + 1) * HEAD_DIM, :]
                              for j in range(span // LANES)], axis=1)
        o_t = jnp.dot(vt, p.astype(BF16), preferred_element_type=F32) / l
        for gi in range(g):
            h = hk * g + gi
            o_ref[:, h * HEAD_DIM:(h + 1) * HEAD_DIM] = jnp.transpose(o_t[:, gi * tq:(gi + 1) * tq])


def _band(q3, k3, v3, sinks, window, tq, n_heads, n_kv):
    b, t, w = q3.shape
    nprev = -(-(window - 1) // tq)
    span = min((nprev + 1) * tq, t)
    has_sink = sinks is not None
    vt4 = jnp.swapaxes(v3.reshape(b, t // LANES, LANES, LANES), 2, 3)
    kern = functools.partial(_band_kernel, window=window, tq=tq, span=span, n_heads=n_heads, n_kv=n_kv,
                             has_sink=has_sink)
    in_specs = [pl.BlockSpec((None, tq, w), lambda bi, i: (bi, i, 0)),
                pl.BlockSpec((None, t, LANES), lambda bi, i: (bi, 0, 0)),
                pl.BlockSpec((None, t // LANES, LANES, LANES), lambda bi, i: (bi, 0, 0, 0))]
    args = [q3, k3, vt4]
    if has_sink:
        in_specs = [pl.BlockSpec(memory_space=pltpu.SMEM)] + in_specs
        args = [sinks] + args
    return pl.pallas_call(
        kern,
        grid=(b, t // tq),
        in_specs=in_specs,
        out_specs=pl.BlockSpec((None, tq, w), lambda bi, i: (bi, i, 0)),
        out_shape=jax.ShapeDtypeStruct((b, t, w), F32),
        compiler_params=_params("parallel", "parallel"),
        name="band_sink" if has_sink else "band_win",
    )(*args)


def _first_max_rows(val, payload=None):
    rows, n = val.shape
    sub = lax.broadcasted_iota(I32, (SUBLANES, n), 0).astype(F32)
    parts = [[val[r:r + SUBLANES] for r in range(0, rows, SUBLANES)],
             [sub + float(r) for r in range(0, rows, SUBLANES)]]
    if payload is not None:
        parts.append([payload[r:r + SUBLANES] for r in range(0, rows, SUBLANES)])
    while len(parts[0]) > 1:
        nxt = [[] for _ in parts]
        for j in range(0, len(parts[0]) - 1, 2):
            take_right = parts[0][j + 1] > parts[0][j]
            for dst, src in zip(nxt, parts):
                dst.append(jnp.where(take_right, src[j + 1], src[j]))
        if len(parts[0]) % 2:
            for dst, src in zip(nxt, parts):
                dst.append(src[-1])
        parts = nxt
    v8, i8 = parts[0][0], parts[1][0]
    m = jnp.max(v8, axis=0, keepdims=True)
    first = jnp.min(jnp.where(v8 == m, i8, float(rows)), axis=0, keepdims=True)
    if payload is None:
        return m, first
    return m, first, jnp.max(jnp.where(i8 == first, parts[2][0], -1.0), axis=0, keepdims=True)


def _topk_rows(s, k, ridx):
    vals, idxs = [], []
    for _ in range(k):
        m, first = _first_max_rows(s)
        vals.append(m)
        idxs.append(first)
        s = jnp.where(ridx == first, -jnp.inf, s)
    return jnp.concatenate(vals, axis=0), jnp.concatenate(idxs, axis=0)


def _peer_candidates(t1, i1, t2, i2):
    vals = [t1[0:1] + t2]
    eids = [i1[0:1] * float(N_KEYS) + i2]
    for a in range(1, 8):
        vals.append(t1[a:a + 1] + t2[0:8])
        eids.append(i1[a:a + 1] * float(N_KEYS) + i2[0:8])
    vals.append(t1[8:16] + t2[0:1])
    eids.append(i1[8:16] * float(N_KEYS) + i2[0:1])
    return jnp.concatenate(vals, axis=0), jnp.concatenate(eids, axis=0)


def _route_kernel(x_ref, oa_ref, oc_ref, os_ref, ow_ref, gt_ref, gexp_ref, wo_ref, gf_ref, wq_ref,
                  k1_ref, k2_ref, x1_ref, eidx_ref, gate_ref):
    tm = x_ref.shape[0]
    gt = gt_ref[...]
    g_hi = gt.astype(BF16)
    g_lo = (gt - g_hi.astype(F32)).astype(BF16)
    ob = jnp.zeros(oc_ref.shape, F32)
    for j, br in enumerate((oc_ref, os_ref, ow_ref)):
        ex = gexp_ref[j]
        gj = (jnp.dot(g_hi, ex, preferred_element_type=F32) + jnp.dot(g_lo, ex, preferred_element_type=F32))
        ob = ob + gj * br[...]
    half = oa_ref.shape[1]
    mixed = (jnp.dot(oa_ref[...].astype(BF16), wo_ref[0:half, :], preferred_element_type=F32)
             + jnp.dot(ob.astype(BF16), wo_ref[half:, :], preferred_element_type=F32))
    x1 = x_ref[...] + mixed
    x1_ref[...] = x1.reshape(tm, SUBLANES, LANES)
    h2 = x1 * lax.rsqrt(jnp.mean(x1 * x1, axis=-1, keepdims=True) + RMS_EPS) * gf_ref[...]
    q = jnp.dot(h2.astype(BF16), wq_ref[...], preferred_element_type=F32)
    hq = PEER_QDIM // 2
    ridx = lax.broadcasted_iota(I32, (N_KEYS, tm), 0).astype(F32)
    e_rows, g_rows = [], []
    for h in range(PEER_HEADS):
        q1 = q[:, h * PEER_QDIM: h * PEER_QDIM + hq].astype(BF16)
        q2 = q[:, h * PEER_QDIM + hq:(h + 1) * PEER_QDIM].astype(BF16)
        s1 = lax.dot_general(k1_ref[...], q1, (((1,), (1,)), ((), ())), preferred_element_type=F32)
        s2 = lax.dot_general(k2_ref[...], q2, (((1,), (1,)), ((), ())), preferred_element_type=F32)
        t1, i1 = _topk_rows(s1, PEER_TOPK, ridx)
        t2, i2 = _topk_rows(s2, PEER_TOPK, ridx)
        cand, cand_e = _peer_candidates(t1, i1, t2, i2)
        cidx = lax.broadcasted_iota(I32, cand.shape, 0).astype(F32)
        sc, ee = [], []
        for _ in range(PEER_TOPK):
            m, first, e_first = _first_max_rows(cand, cand_e)
            sc.append(m)
            ee.append(e_first)
            cand = jnp.where(cidx == first, -jnp.inf, cand)
        sc = jnp.concatenate(sc, axis=0)
        ex = jnp.exp(sc - sc[0:1])
        g_rows.append(ex / jnp.sum(ex, axis=0, keepdims=True))
        e_rows.append(jnp.concatenate(ee, axis=0))
    e_t = jnp.concatenate(e_rows, axis=0)
    g_t = jnp.concatenate(g_rows, axis=0)
    eidx_ref[...] = jnp.transpose(e_t).astype(I32)
    gate_ref[...] = jnp.transpose(g_t)


def _route(x2, oa, oc, osel, ow, gates, gexp, wo, gf, wq, k1, k2, tm):
    n, d = x2.shape
    row = lambda w: pl.BlockSpec((tm, w), lambda i: (i, 0))
    full = lambda a: pl.BlockSpec(a.shape, lambda i: (0,) * a.ndim)
    nk = PEER_HEADS * PEER_TOPK
    return pl.pallas_call(
        _route_kernel,
        grid=(n // tm,),
        in_specs=[row(d), row(oa.shape[1]), row(oc.shape[1]), row(osel.shape[1]), row(ow.shape[1]), row(LANES),
                  full(gexp), full(wo), full(gf), full(wq), full(k1), full(k2)],
        out_specs=[pl.BlockSpec((tm, SUBLANES, LANES), lambda i: (i, 0, 0)), row(nk), row(nk)],
        out_shape=[jax.ShapeDtypeStruct((n, SUBLANES, LANES), F32), jax.ShapeDtypeStruct((n, nk), I32),
                   jax.ShapeDtypeStruct((n, nk), F32)],
        compiler_params=_params("parallel"),
        name="route",
    )(x2, oa, oc, osel, ow, gates, gexp, wo, gf, wq, k1, k2)


def _fold8(p):
    sub = lax.broadcasted_iota(I32, (SUBLANES, LANES), 0)
    lo4 = sub < 4
    q = [jnp.where(lo4, p[j], p[j + 4]) + pltpu.roll(jnp.where(lo4, p[j + 4], p[j]), 4, 0) for j in range(4)]
    m2 = (sub & 3) < 2
    r = [jnp.where(m2, q[j] + pltpu.roll(q[j], 6, 0), q[j + 2] + pltpu.roll(q[j + 2], 2, 0)) for j in range(2)]
    m1 = (sub & 1) == 0
    return jnp.where(m1, r[0] + pltpu.roll(r[0], 7, 0), r[1] + pltpu.roll(r[1], 1, 0))


def _peer_kernel(e0_ref, e1_ref, e2_ref, x_ref, gate_ref, gffn_ref, gfin_ref, tab_ref, o_ref,
                 buf, sem, a_sc, cb_sc, *, tb):
    step = pl.program_id(0)
    nsteps = pl.num_programs(0)
    slot = step % PEER_SLOTS
    fill = (step + PEER_SLOTS - 1) % PEER_SLOTS
    nk = PEER_HEADS * PEER_TOPK
    npair = tb * nk

    def issue_rows(eref, dst_slot, t, k0, k1):
        erow = eref.at[t]
        for k in range(k0, k1):
            copy = pltpu.make_async_copy(tab_ref.at[erow[k]], buf.at[dst_slot, t * nk + k], sem.at[dst_slot])
            copy.start(priority=k % 2)

    def wait_slot(s):
        pltpu.make_async_copy(tab_ref.at[pl.ds(0, npair)], buf.at[s], sem.at[s]).wait()

    @pl.when(step == 0)
    def _():
        def first(t, carry):
            issue_rows(e0_ref, 0, t, 0, nk)
            return carry
        lax.fori_loop(0, tb, first, 0)
        def second(t, carry):
            issue_rows(e1_ref, 1, t, 0, nk)
            return carry
        lax.fori_loop(0, tb, second, 0)

    eye = jnp.where(lax.broadcasted_iota(I32, (nk, nk), 0) == lax.broadcasted_iota(I32, (nk, nk), 1),
                    1.0, 0.0).astype(BF16)
    gr = gate_ref[...]
    g0 = gr.astype(BF16)
    r1 = gr - g0.astype(F32)
    g1 = r1.astype(BF16)
    g2 = (r1 - g1.astype(F32)).astype(BF16)
    dn = (((1,), (1,)), ((), ()))
    gate_t = (lax.dot_general(eye, g0, dn, preferred_element_type=F32)
              + lax.dot_general(eye, g1, dn, preferred_element_type=F32)
              + lax.dot_general(eye, g2, dn, preferred_element_type=F32))

    wait_slot(slot)

    lane_tb = lax.broadcasted_iota(I32, (nk, tb), 1)
    inv_d = 1.0 / (SUBLANES * LANES)
    a_sc[...] = jnp.zeros(a_sc.shape, F32)

    ngroup = nk // SUBLANES
    rows_p1 = 6
    rows_p2 = SUBLANES - rows_p1

    def dots(t, carry):
        x8 = x_ref[t]
        ms = jnp.sum(jnp.sum(x8 * x8, axis=1, keepdims=True), axis=0, keepdims=True) * inv_d
        h8 = x8 * lax.rsqrt(ms + RMS_EPS) * gffn_ref[...]
        base = t * nk
        folded = []
        for gi in range(nk // SUBLANES):
            blk = buf[slot, pl.ds(base + gi * SUBLANES, SUBLANES), 0:SUBLANES, :]
            folded.append(_fold8([blk[j] * h8 for j in range(SUBLANES)]))
            issue_rows(e2_ref, fill, t, gi * rows_p1, (gi + 1) * rows_p1)
        a = jnp.sum(jnp.concatenate(folded, axis=0), axis=1, keepdims=True)
        a_sc[...] = jnp.where(lane_tb == t, a, a_sc[...])
        return carry

    lax.fori_loop(0, tb, dots, 0, unroll=2)

    a_sc[...] = gate_t * _gelu(a_sc[...])

    def combine(t, carry):
        base = t * nk
        w_col = jnp.sum(jnp.where(lane_tb == t, a_sc[...], 0.0), axis=1, keepdims=True)
        cb_sc[t] = jnp.broadcast_to(w_col, (nk, LANES))
        accs = [jnp.zeros((SUBLANES, LANES), F32) for _ in range(4)]
        for gi in range(ngroup):
            for k in range(gi * SUBLANES, (gi + 1) * SUBLANES):
                ck = jnp.broadcast_to(cb_sc[t, k:k + 1, :], (SUBLANES, LANES))
                accs[k % 4] = accs[k % 4] + ck * buf[slot, base + k, SUBLANES:2 * SUBLANES, :]
            issue_rows(e2_ref, fill, t, ngroup * rows_p1 + gi * rows_p2, ngroup * rows_p1 + (gi + 1) * rows_p2)
        y8 = x_ref[t] + ((accs[0] + accs[1]) + (accs[2] + accs[3]))
        ms2 = jnp.sum(jnp.sum(y8 * y8, axis=1, keepdims=True), axis=0, keepdims=True) * inv_d
        o_ref[t] = y8 * lax.rsqrt(ms2 + RMS_EPS) * gfin_ref[...]
        return carry

    lax.fori_loop(0, tb, combine, 0, unroll=4)

    @pl.when(step == nsteps - 1)
    def _():
        for ahead in range(1, PEER_SLOTS):
            wait_slot((step + ahead) % PEER_SLOTS)


def _peer(eidx, x1s, gate, gffn8, gfin8, table, tb):
    n = x1s.shape[0]
    nk = PEER_HEADS * PEER_TOPK
    nsteps = n // tb
    kern = functools.partial(_peer_kernel, tb=tb)
    return pl.pallas_call(
        kern,
        grid=(nsteps,),
        in_specs=[pl.BlockSpec((tb, nk), lambda i: (i, 0), memory_space=pltpu.SMEM),
                  pl.BlockSpec((tb, nk), lambda i: ((i + 1) % nsteps, 0), memory_space=pltpu.SMEM),
                  pl.BlockSpec((tb, nk), lambda i: ((i + PEER_SLOTS - 1) % nsteps, 0), memory_space=pltpu.SMEM),
                  pl.BlockSpec((tb, SUBLANES, LANES), lambda i: (i, 0, 0)),
                  pl.BlockSpec((tb, nk), lambda i: (i, 0)),
                  pl.BlockSpec((SUBLANES, LANES), lambda i: (0, 0)),
                  pl.BlockSpec((SUBLANES, LANES), lambda i: (0, 0)),
                  pl.BlockSpec(memory_space=pl.ANY)],
        out_specs=pl.BlockSpec((tb, SUBLANES, LANES), lambda i: (i, 0, 0)),
        out_shape=jax.ShapeDtypeStruct((n, SUBLANES, LANES), F32),
        scratch_shapes=[pltpu.VMEM((PEER_SLOTS, tb * nk, 2 * SUBLANES, LANES), F32),
                        pltpu.SemaphoreType.DMA((PEER_SLOTS,)),
                        pltpu.VMEM((nk, tb), F32),
                        pltpu.VMEM((tb, nk, LANES), F32)],
        compiler_params=_params("arbitrary"),
        name="peer",
    )(eidx, eidx, eidx, x1s, gate, gffn8, gfin8, table)


def _overlap_t(t):
    nc = (t - CMP_LEN) // CMP_STRIDE + 1
    ns = t // SEL_BLOCK
    ncp = t // CMP_STRIDE
    cstart = np.arange(nc) * CMP_STRIDE
    bstart = np.arange(ns) * SEL_BLOCK
    lo = np.maximum(cstart[:, None], bstart[None, :])
    hi = np.minimum(cstart[:, None] + CMP_LEN, bstart[None, :] + SEL_BLOCK)
    ov = np.clip(hi - lo, 0, None).astype(np.float32) / CMP_LEN
    out = np.zeros((ns, ncp), np.float32)
    out[:, :nc] = ov.T
    return out, nc


def _gate_expand():
    ex = np.zeros((NSA_N_GATES, LANES, NSA_HEADS * HEAD_DIM), np.float32)
    for j in range(NSA_N_GATES):
        for h in range(NSA_HEADS):
            ex[j, h * NSA_N_GATES + j, h * HEAD_DIM:(h + 1) * HEAD_DIM] = 1.0
    return ex


def _layer(x, positions, norm_attn, w_in, attn_sinks, cmp_k_pos, cmp_k_w1, cmp_k_w2, cmp_v_pos, cmp_v_w1,
           cmp_v_w2, w_out, norm_ffn, w_query, sub_keys_1, sub_keys_2, expert_down, expert_up, norm_out,
           tm_in=512, tq_cmp=256, tq_sel=512, tk_sel=512, tq_swa=128, tq_win=256, tm_route=256, tb_peer=16):
    b, t, d = x.shape
    n = b * t
    x2 = x.reshape(n, d)
    half = HEAD_DIM // 2
    inv_freq = ROPE_THETA ** (-jnp.arange(half, dtype=F32) / half)
    invf = jnp.tile(inv_freq, LANES // half).reshape(1, LANES)
    w_pad = jnp.pad(w_in, ((0, 0), (0, IN_PAD - IN_WIDTH))).astype(BF16)
    pr = _inproj(x2, positions.reshape(n, 1), norm_attn.reshape(1, d), w_pad, invf, min(tm_in, n))
    r3 = lambda a: a.reshape(b, t, a.shape[-1])

    o_a = _band(r3(pr["qa"]), r3(pr["ka"]), r3(pr["va"]), attn_sinks.astype(F32), SWA_WINDOW, tq_swa,
                SWA_HEADS, SWA_KV_HEADS)
    kcmp, vcmp = _compress(r3(pr["kc"]), r3(pr["vc"]),
                           cmp_k_pos.reshape(1, -1), cmp_k_w1.astype(BF16), cmp_k_w2.astype(BF16),
                           cmp_v_pos.reshape(1, -1), cmp_v_w1.astype(BF16), cmp_v_w2.astype(BF16))
    ovt, n_cmp = _overlap_t(t)
    qb3 = r3(pr["qb"])
    o_cmp, sel = _cmpsel(qb3, kcmp, vcmp, jnp.asarray(ovt), tq_cmp, n_cmp)
    tk_sel = min(tk_sel, t)
    vst4 = jnp.swapaxes(r3(pr["vs"]).reshape(b, t // tk_sel, tk_sel, LANES), 2, 3)
    o_sel = _selattn(qb3, r3(pr["ks"]), vst4, sel, tq_sel, tk_sel)
    o_win = _band(qb3, r3(pr["kw"]), r3(pr["vw"]), None, NSA_WINDOW, tq_win, NSA_HEADS, NSA_KV_HEADS)

    x1, eidx, gate = _route(x2, o_a.reshape(n, -1), o_cmp.reshape(n, -1), o_sel.reshape(n, -1),
                            o_win.reshape(n, -1), pr["gates"], jnp.asarray(_gate_expand(), BF16),
                            w_out.astype(BF16), norm_ffn.reshape(1, d), w_query.astype(BF16),
                            sub_keys_1.astype(BF16), sub_keys_2.astype(BF16), min(tm_route, n))
    ne = expert_down.shape[0]
    table = jnp.stack([expert_down.reshape(ne, SUBLANES, LANES), expert_up.reshape(ne, SUBLANES, LANES)],
                      axis=1).reshape(ne, 2 * SUBLANES, LANES)
    y = _peer(eidx, x1, gate, norm_ffn.reshape(SUBLANES, LANES),
              norm_out.reshape(SUBLANES, LANES), table, tb_peer)
    return y.reshape(b, t, d)


def kernel(x, positions, norm_attn, w_in, attn_sinks, cmp_k_pos, cmp_k_w1, cmp_k_w2, cmp_v_pos, cmp_v_w1,
           cmp_v_w2, w_out, norm_ffn, peer_w_query, peer_sub_keys_1, peer_sub_keys_2, peer_expert_down,
           peer_expert_up, norm_f):
    assert norm_attn.shape[0] == 1, "single-layer block"
    return _layer(x, positions, norm_attn[0], w_in[0], attn_sinks[0], cmp_k_pos[0], cmp_k_w1[0], cmp_k_w2[0],
                  cmp_v_pos[0], cmp_v_w1[0], cmp_v_w2[0], w_out[0], norm_ffn[0], peer_w_query[0],
                  peer_sub_keys_1[0], peer_sub_keys_2[0], peer_expert_down[0], peer_expert_up[0], norm_f)
```

```python
import functools

import numpy as np
import jax
import jax.numpy as jnp
from jax import lax
from jax.experimental import pallas as pl
from jax.experimental.pallas import tpu as pltpu

F32 = jnp.float32
BF16 = jnp.bfloat16
I32 = jnp.int32

HEAD_DIM = 64
ROPE_THETA = 10000.0
RMS_EPS = 1e-6
NEG_BIG = -1e30

SWA_HEADS = 8
SWA_KV_HEADS = 2
SWA_WINDOW = 128

NSA_HEADS = 8
NSA_KV_HEADS = 2
CMP_LEN = 32
CMP_STRIDE = 16
CMP_HIDDEN = 256
SEL_BLOCK = 64
SEL_TOPN = 16
NSA_WINDOW = 512
NSA_N_GATES = 3

PEER_HEADS = 8
N_KEYS = 128
PEER_QDIM = 256
PEER_TOPK = 16
PEER_SLOTS = 3

LANES = 128
SUBLANES = 8
VMEM_LIMIT = 56 * 1024 * 1024

_IN_SPLITS = (("qa", 512), ("ka", 128), ("va", 128), ("qb", 512), ("kc", 128), ("vc", 128),
              ("ks", 128), ("vs", 128), ("kw", 128), ("vw", 128), ("gl", 24))
IN_WIDTH = sum(w for _, w in _IN_SPLITS)
IN_PAD = -(-IN_WIDTH // LANES) * LANES
_IN_OFF = {}
_o = 0
for _n, _w in _IN_SPLITS:
    _IN_OFF[_n] = _o
    _o += _w
_ROPED = ("qa", "ka", "qb", "kc", "ks", "kw")


def _gelu(x):
    return 0.5 * x * (1.0 + lax.erf(x * 0.7071067811865476))


def _params(*sem):
    return pltpu.CompilerParams(dimension_semantics=sem, vmem_limit_bytes=VMEM_LIMIT)


def _inproj_kernel(x_ref, pos_ref, g_ref, w_ref, invf_ref,
                   qa_ref, ka_ref, va_ref, qb_ref, kc_ref, vc_ref, ks_ref, vs_ref, kw_ref, vw_ref, gt_ref):
    x = x_ref[...]
    h = x * lax.rsqrt(jnp.mean(x * x, axis=-1, keepdims=True) + RMS_EPS) * g_ref[...]
    proj = jnp.dot(h.astype(BF16), w_ref[...], preferred_element_type=F32)
    ang = pos_ref[...].astype(F32) * invf_ref[...]
    lane = lax.broadcasted_iota(I32, ang.shape, 1)
    lo = (lane & (HEAD_DIM - 1)) < (HEAD_DIM // 2)
    cos = jnp.cos(ang)
    sin = jnp.sin(ang)
    sin_s = jnp.where(lo, -sin, sin)
    outs = {"qa": qa_ref, "ka": ka_ref, "va": va_ref, "qb": qb_ref, "kc": kc_ref, "vc": vc_ref,
            "ks": ks_ref, "vs": vs_ref, "kw": kw_ref, "vw": vw_ref}
    for name, width in _IN_SPLITS[:-1]:
        off = _IN_OFF[name]
        ref = outs[name]
        for i in range(width // LANES):
            v = proj[:, off + LANES * i: off + LANES * (i + 1)]
            if name in _ROPED:
                rot = jnp.where(lo, pltpu.roll(v, LANES - HEAD_DIM // 2, 1), pltpu.roll(v, HEAD_DIM // 2, 1))
                v = v * cos + rot * sin_s
            ref[:, LANES * i: LANES * (i + 1)] = v.astype(ref.dtype)
    gl = proj[:, _IN_OFF["gl"]: _IN_OFF["gl"] + LANES]
    gt_ref[...] = jax.nn.sigmoid(gl)


def _inproj(x2, pos2, g, w_pad, invf, tm):
    n, d = x2.shape
    row = lambda w: pl.BlockSpec((tm, w), lambda i: (i, 0))
    full = lambda a: pl.BlockSpec(a.shape, lambda i: (0,) * a.ndim)
    out_dt = {"qa": BF16, "ka": BF16, "va": BF16, "qb": BF16, "kc": F32, "vc": F32,
              "ks": BF16, "vs": BF16, "kw": BF16, "vw": BF16}
    names = [nm for nm, _ in _IN_SPLITS[:-1]]
    widths = dict(_IN_SPLITS)
    out_shape = [jax.ShapeDtypeStruct((n, widths[nm]), out_dt[nm]) for nm in names]
    out_shape.append(jax.ShapeDtypeStruct((n, LANES), F32))
    out_specs = [row(widths[nm]) for nm in names] + [row(LANES)]
    res = pl.pallas_call(
        _inproj_kernel,
        grid=(n // tm,),
        in_specs=[row(d), row(1), full(g), full(w_pad), full(invf)],
        out_specs=out_specs,
        out_shape=out_shape,
        compiler_params=_params("parallel"),
        name="inproj",
    )(x2, pos2, g, w_pad, invf)
    out = dict(zip(names, res[:-1]))
    out["gates"] = res[-1]
    return out


def _compress_kernel(kc_ref, vc_ref, kpos_ref, kw1_ref, kw2_ref, vpos_ref, vw1_ref, vw2_ref, ko_ref, vo_ref):
    ncp = ko_ref.shape[0]
    for src, pos_ref, w1_ref, w2_ref, out_ref in ((kc_ref, kpos_ref, kw1_ref, kw2_ref, ko_ref),
                                                  (vc_ref, vpos_ref, vw1_ref, vw2_ref, vo_ref)):
        bias = jnp.dot(pos_ref[...].astype(BF16), w1_ref[...], preferred_element_type=F32)
        acc_a = [jnp.zeros((ncp, CMP_HIDDEN), F32) for _ in range(2)]
        acc_b = [jnp.zeros((ncp, CMP_HIDDEN), F32) for _ in range(2)]
        for j in range(CMP_STRIDE):
            pj = src[pl.ds(j, ncp, stride=CMP_STRIDE), :].astype(BF16)
            wa = w1_ref[j * HEAD_DIM:(j + 1) * HEAD_DIM, :]
            wb = w1_ref[(CMP_STRIDE + j) * HEAD_DIM:(CMP_STRIDE + j + 1) * HEAD_DIM, :]
            for hh in range(2):
                ph = pj[:, hh * HEAD_DIM:(hh + 1) * HEAD_DIM]
                acc_a[hh] = acc_a[hh] + jnp.dot(ph, wa, preferred_element_type=F32)
                acc_b[hh] = acc_b[hh] + jnp.dot(ph, wb, preferred_element_type=F32)
        for hh in range(2):
            h1 = acc_a[hh] + pltpu.roll(acc_b[hh], ncp - 1, 0) + bias
            out = jnp.dot(_gelu(h1).astype(BF16), w2_ref[...], preferred_element_type=F32)
            out_ref[:, hh * HEAD_DIM:(hh + 1) * HEAD_DIM] = out


def _compress(kc3, vc3, kpos, kw1, kw2, vpos, vw1, vw2):
    b, t, w = kc3.shape
    ncp = t // CMP_STRIDE
    per_b = pl.BlockSpec((None, t, w), lambda i: (i, 0, 0))
    full = lambda a: pl.BlockSpec(a.shape, lambda i: (0,) * a.ndim)
    out_spec = pl.BlockSpec((None, ncp, w), lambda i: (i, 0, 0))
    return pl.pallas_call(
        _compress_kernel,
        grid=(b,),
        in_specs=[per_b, per_b, full(kpos), full(kw1), full(kw2), full(vpos), full(vw1), full(vw2)],
        out_specs=[out_spec, out_spec],
        out_shape=[jax.ShapeDtypeStruct((b, ncp, w), F32)] * 2,
        compiler_params=_params("parallel"),
        name="compress",
    )(kc3, vc3, kpos, kw1, kw2, vpos, vw1, vw2)


def _cmpsel_kernel(q_ref, kc_ref, vc_ref, ovt_ref, o_ref, sel_ref, *, tq, n_cmp):
    i = pl.program_id(1)
    t0 = i * tq
    g = NSA_HEADS // NSA_KV_HEADS
    ncp = kc_ref.shape[0]
    ns = ovt_ref.shape[0]
    rows = g * tq
    tpos = t0 + (lax.broadcasted_iota(I32, (rows, ncp), 0) & (tq - 1))
    cidx = lax.broadcasted_iota(I32, (rows, ncp), 1)
    vis = (cidx * CMP_STRIDE + (CMP_LEN - 1) <= tpos) & (cidx < n_cmp)
    jidx = lax.broadcasted_iota(I32, (ns, tq), 0)
    tcol = t0 + lax.broadcasted_iota(I32, (ns, tq), 1)
    cur = tcol // SEL_BLOCK
    forced = (jidx == 0) | (jidx == cur) | (jidx == cur - 1)
    valid = jidx * SEL_BLOCK <= tcol
    for hk in range(NSA_KV_HEADS):
        kk = kc_ref[:, hk * HEAD_DIM:(hk + 1) * HEAD_DIM].astype(BF16)
        vv = vc_ref[:, hk * HEAD_DIM:(hk + 1) * HEAD_DIM].astype(BF16)
        q4 = jnp.concatenate(
            [q_ref[:, (hk * g + gi) * HEAD_DIM:(hk * g + gi + 1) * HEAD_DIM] for gi in range(g)], axis=0)
        s = lax.dot_general(q4, kk, (((1,), (1,)), ((), ())), preferred_element_type=F32) * (HEAD_DIM ** -0.5)
        s = jnp.where(vis, s, NEG_BIG)
        m = jnp.max(s, axis=-1, keepdims=True)
        e = jnp.exp(s - m)
        p = jnp.where(vis, e / jnp.sum(e, axis=-1, keepdims=True), 0.0)
        o = jnp.dot(p.astype(BF16), vv, preferred_element_type=F32)
        for gi in range(g):
            h = hk * g + gi
            o_ref[:, h * HEAD_DIM:(h + 1) * HEAD_DIM] = o[gi * tq:(gi + 1) * tq]
        psum = p[0:tq]
        for gi in range(1, g):
            psum = psum + p[gi * tq:(gi + 1) * tq]
        imp_t = lax.dot_general(ovt_ref[...], psum, (((1,), (1,)), ((), ())),
                                precision=lax.Precision.HIGHEST, preferred_element_type=F32)
        score = jnp.where(forced, jnp.inf, jnp.where(valid, imp_t, -jnp.inf))
        cnt = jnp.zeros((ns, tq), F32)
        for r in range(ns):
            row = score[r:r + 1, :]
            beats = (row > score) | ((row >= score) & (jidx > r))
            cnt = cnt + jnp.where(beats, 1.0, 0.0)
        sel_ref[hk] = jnp.where((cnt < SEL_TOPN) & (score > -jnp.inf), 1.0, 0.0)


def _cmpsel(qb3, kcmp, vcmp, ovt, tq, n_cmp):
    b, t, w = qb3.shape
    ncp = kcmp.shape[1]
    ns = ovt.shape[0]
    kern = functools.partial(_cmpsel_kernel, tq=tq, n_cmp=n_cmp)
    return pl.pallas_call(
        kern,
        grid=(b, t // tq),
        in_specs=[pl.BlockSpec((None, tq, w), lambda bi, i: (bi, i, 0)),
                  pl.BlockSpec((None, ncp, LANES), lambda bi, i: (bi, 0, 0)),
                  pl.BlockSpec((None, ncp, LANES), lambda bi, i: (bi, 0, 0)),
                  pl.BlockSpec(ovt.shape, lambda bi, i: (0, 0))],
        out_specs=[pl.BlockSpec((None, tq, w), lambda bi, i: (bi, i, 0)),
                   pl.BlockSpec((None, NSA_KV_HEADS, ns, tq), lambda bi, i: (bi, 0, 0, i))],
        out_shape=[jax.ShapeDtypeStruct((b, t, w), F32), jax.ShapeDtypeStruct((b, NSA_KV_HEADS, ns, t), F32)],
        compiler_params=_params("parallel", "parallel"),
        name="cmpsel",
    )(qb3, kcmp, vcmp, ovt)


def _selattn_kernel(q_ref, k_ref, vt_ref, sel_ref, o_ref, m_sc, l_sc, acc_sc, *, tq, tk):
    i = pl.program_id(1)
    t0 = i * tq
    g = NSA_HEADS // NSA_KV_HEADS
    rows = g * tq
    assert tk % tq == 0, "one key tile must cover the whole diagonal of a query tile"
    n_below = (i * tq) // tk
    nb = tk // SEL_BLOCK
    qpos = t0 + (lax.broadcasted_iota(I32, (tk, rows), 1) & (tq - 1))
    krow = lax.broadcasted_iota(I32, (tk, rows), 0)
    for hk in range(NSA_KV_HEADS):
        qt = jnp.concatenate(
            [jnp.transpose(q_ref[:, (hk * g + gi) * HEAD_DIM:(hk * g + gi + 1) * HEAD_DIM].astype(F32))
             for gi in range(g)], axis=1)
        qt = (qt * (HEAD_DIM ** -0.5)).astype(BF16)
        m_sc[...] = jnp.full(m_sc.shape, NEG_BIG, F32)
        l_sc[...] = jnp.zeros(l_sc.shape, F32)
        acc_sc[...] = jnp.zeros(acc_sc.shape, F32)

        def tile(kt, diagonal):
            ks = pl.multiple_of(kt * tk, tk)
            kk = k_ref[pl.ds(ks, tk), hk * HEAD_DIM:(hk + 1) * HEAD_DIM]
            s = jnp.dot(kk, qt, preferred_element_type=F32)
            bias = jnp.concatenate(
                [jnp.broadcast_to(
                    jnp.concatenate([(1.0 - sel_ref[hk, pl.ds(kt * nb + jj, 1), :]) * NEG_BIG] * g, axis=1),
                    (SEL_BLOCK, rows)) for jj in range(nb)], axis=0)
            s = s + bias
            if diagonal:
                s = jnp.where(ks + krow <= qpos, s, NEG_BIG)
            m_old = m_sc[...]
            m_new = jnp.maximum(m_old, jnp.max(s, axis=0, keepdims=True))
            alpha = jnp.exp(m_old - m_new)
            p = jnp.exp(s - m_new)
            l_sc[...] = alpha * l_sc[...] + jnp.sum(p, axis=0, keepdims=True)
            vt = vt_ref[kt, hk * HEAD_DIM:(hk + 1) * HEAD_DIM, :]
            acc_sc[...] = alpha * acc_sc[...] + jnp.dot(vt, p.astype(BF16), preferred_element_type=F32)
            m_sc[...] = m_new

        def below_diagonal(kt, carry):
            tile(kt, False)
            return carry

        lax.fori_loop(0, n_below, below_diagonal, 0)
        tile(n_below, True)
        o_t = acc_sc[...] / l_sc[...]
        for gi in range(g):
            h = hk * g + gi
            o_ref[:, h * HEAD_DIM:(h + 1) * HEAD_DIM] = jnp.transpose(o_t[:, gi * tq:(gi + 1) * tq])


def _selattn(qb3, ks3, vst4, sel4, tq, tk):
    b, t, w = qb3.shape
    g = NSA_HEADS // NSA_KV_HEADS
    ns = sel4.shape[2]
    kern = functools.partial(_selattn_kernel, tq=tq, tk=tk)
    return pl.pallas_call(
        kern,
        grid=(b, t // tq),
        in_specs=[pl.BlockSpec((None, tq, w), lambda bi, i: (bi, i, 0)),
                  pl.BlockSpec((None, t, LANES), lambda bi, i: (bi, 0, 0)),
                  pl.BlockSpec((None, t // tk, LANES, tk), lambda bi, i: (bi, 0, 0, 0)),
                  pl.BlockSpec((None, NSA_KV_HEADS, ns, tq), lambda bi, i: (bi, 0, 0, i))],
        out_specs=pl.BlockSpec((None, tq, w), lambda bi, i: (bi, i, 0)),
        out_shape=jax.ShapeDtypeStruct((b, t, w), F32),
        scratch_shapes=[pltpu.VMEM((1, g * tq), F32), pltpu.VMEM((1, g * tq), F32),
                        pltpu.VMEM((HEAD_DIM, g * tq), F32)],
        compiler_params=_params("parallel", "arbitrary"),
        name="selattn",
    )(qb3, ks3, vst4, sel4)


def _band_kernel(*refs, window, tq, span, n_heads, n_kv, has_sink):
    if has_sink:
        sink_ref, q_ref, k_ref, vt_ref, o_ref = refs
    else:
        q_ref, k_ref, vt_ref, o_ref = refs
    i = pl.program_id(1)
    t0 = i * tq
    start = pl.multiple_of(jnp.maximum(t0 + tq - span, 0), tq)
    tile0 = start // LANES
    g = n_heads // n_kv
    rows = g * tq
    qpos = t0 + (lax.broadcasted_iota(I32, (span, rows), 1) & (tq - 1))
    kpos = start + lax.broadcasted_iota(I32, (span, rows), 0)
    diff = qpos - kpos
    mask = (diff >= 0) & (diff < window)
    for hk in range(n_kv):
        kk = k_ref[pl.ds(start, span), hk * HEAD_DIM:(hk + 1) * HEAD_DIM]
        qt = jnp.concatenate(
            [jnp.transpose(q_ref[:, (hk * g + gi) * HEAD_DIM:(hk * g + gi + 1) * HEAD_DIM].astype(F32))
             for gi in range(g)], axis=1)
        qt = (qt * (HEAD_DIM ** -0.5)).astype(BF16)
        s = jnp.where(mask, jnp.dot(kk, qt, preferred_element_type=F32), NEG_BIG)
        m = jnp.max(s, axis=0, keepdims=True)
        if has_sink:
            sink = jnp.concatenate([jnp.full((1, tq), sink_ref[hk * g + gi], F32) for gi in range(g)], axis=1)
            m = jnp.maximum(m, sink)
        p = jnp.exp(s - m)
        l = jnp.sum(p, axis=0, keepdims=True)
        if has_sink:
            l = l + jnp.exp(sink - m)
        vt = jnp.concatenate([vt_ref[tile0 + j, hk * HEAD_DIM:(hk + 1) * HEAD_DIM, :]
                              for j in range(span // LANES)], axis=1)
        o_t = jnp.dot(vt, p.astype(BF16), preferred_element_type=F32) / l
        for gi in range(g):
            h = hk * g + gi
            o_ref[:, h * HEAD_DIM:(h + 1) * HEAD_DIM] = jnp.transpose(o_t[:, gi * tq:(gi + 1) * tq])


def _band(q3, k3, v3, sinks, window, tq, n_heads, n_kv):
    b, t, w = q3.shape
    nprev = -(-(window - 1) // tq)
    span = min((nprev + 1) * tq, t)
    has_sink = sinks is not None
    vt4 = jnp.swapaxes(v3.reshape(b, t // LANES, LANES, LANES), 2, 3)
    kern = functools.partial(_band_kernel, window=window, tq=tq, span=span, n_heads=n_heads, n_kv=n_kv,
                             has_sink=has_sink)
    in_specs = [pl.BlockSpec((None, tq, w), lambda bi, i: (bi, i, 0)),
                pl.BlockSpec((None, t, LANES), lambda bi, i: (bi, 0, 0)),
                pl.BlockSpec((None, t // LANES, LANES, LANES), lambda bi, i: (bi, 0, 0, 0))]
    args = [q3, k3, vt4]
    if has_sink:
        in_specs = [pl.BlockSpec(memory_space=pltpu.SMEM)] + in_specs
        args = [sinks] + args
    return pl.pallas_call(
        kern,
        grid=(b, t // tq),
        in_specs=in_specs,
        out_specs=pl.BlockSpec((None, tq, w), lambda bi, i: (bi, i, 0)),
        out_shape=jax.ShapeDtypeStruct((b, t, w), F32),
        compiler_params=_params("parallel", "parallel"),
        name="band_sink" if has_sink else "band_win",
    )(*args)


def _first_max_rows(val, payload=None):
    rows, n = val.shape
    sub = lax.broadcasted_iota(I32, (SUBLANES, n), 0).astype(F32)
    parts = [[val[r:r + SUBLANES] for r in range(0, rows, SUBLANES)],
             [sub + float(r) for r in range(0, rows, SUBLANES)]]
    if payload is not None:
        parts.append([payload[r:r + SUBLANES] for r in range(0, rows, SUBLANES)])
    while len(parts[0]) > 1:
        nxt = [[] for _ in parts]
        for j in range(0, len(parts[0]) - 1, 2):
            take_right = parts[0][j + 1] > parts[0][j]
            for dst, src in zip(nxt, parts):
                dst.append(jnp.where(take_right, src[j + 1], src[j]))
        if len(parts[0]) % 2:
            for dst, src in zip(nxt, parts):
                dst.append(src[-1])
        parts = nxt
    v8, i8 = parts[0][0], parts[1][0]
    m = jnp.max(v8, axis=0, keepdims=True)
    first = jnp.min(jnp.where(v8 == m, i8, float(rows)), axis=0, keepdims=True)
    if payload is None:
        return m, first
    return m, first, jnp.max(jnp.where(i8 == first, parts[2][0], -1.0), axis=0, keepdims=True)


def _topk_rows(s, k, ridx):
    vals, idxs = [], []
    for _ in range(k):
        m, first = _first_max_rows(s)
        vals.append(m)
        idxs.append(first)
        s = jnp.where(ridx == first, -jnp.inf, s)
    return jnp.concatenate(vals, axis=0), jnp.concatenate(idxs, axis=0)


def _peer_candidates(t1, i1, t2, i2):
    vals = [t1[0:1] + t2]
    eids = [i1[0:1] * float(N_KEYS) + i2]
    for a in range(1, 8):
        vals.append(t1[a:a + 1] + t2[0:8])
        eids.append(i1[a:a + 1] * float(N_KEYS) + i2[0:8])
    vals.append(t1[8:16] + t2[0:1])
    eids.append(i1[8:16] * float(N_KEYS) + i2[0:1])
    return jnp.concatenate(vals, axis=0), jnp.concatenate(eids, axis=0)


def _route_kernel(x_ref, oa_ref, oc_ref, os_ref, ow_ref, gt_ref, gexp_ref, wo_ref, gf_ref, wq_ref,
                  k1_ref, k2_ref, x1_ref, eidx_ref, gate_ref):
    tm = x_ref.shape[0]
    gt = gt_ref[...]
    g_hi = gt.astype(BF16)
    g_lo = (gt - g_hi.astype(F32)).astype(BF16)
    ob = jnp.zeros(oc_ref.shape, F32)
    for j, br in enumerate((oc_ref, os_ref, ow_ref)):
        ex = gexp_ref[j]
        gj = (jnp.dot(g_hi, ex, preferred_element_type=F32) + jnp.dot(g_lo, ex, preferred_element_type=F32))
        ob = ob + gj * br[...]
    half = oa_ref.shape[1]
    mixed = (jnp.dot(oa_ref[...].astype(BF16), wo_ref[0:half, :], preferred_element_type=F32)
             + jnp.dot(ob.astype(BF16), wo_ref[half:, :], preferred_element_type=F32))
    x1 = x_ref[...] + mixed
    x1_ref[...] = x1.reshape(tm, SUBLANES, LANES)
    h2 = x1 * lax.rsqrt(jnp.mean(x1 * x1, axis=-1, keepdims=True) + RMS_EPS) * gf_ref[...]
    q = jnp.dot(h2.astype(BF16), wq_ref[...], preferred_element_type=F32)
    hq = PEER_QDIM // 2
    ridx = lax.broadcasted_iota(I32, (N_KEYS, tm), 0).astype(F32)
    e_rows, g_rows = [], []
    for h in range(PEER_HEADS):
        q1 = q[:, h * PEER_QDIM: h * PEER_QDIM + hq].astype(BF16)
        q2 = q[:, h * PEER_QDIM + hq:(h + 1) * PEER_QDIM].astype(BF16)
        s1 = lax.dot_general(k1_ref[...], q1, (((1,), (1,)), ((), ())), preferred_element_type=F32)
        s2 = lax.dot_general(k2_ref[...], q2, (((1,), (1,)), ((), ())), preferred_element_type=F32)
        t1, i1 = _topk_rows(s1, PEER_TOPK, ridx)
        t2, i2 = _topk_rows(s2, PEER_TOPK, ridx)
        cand, cand_e = _peer_candidates(t1, i1, t2, i2)
        cidx = lax.broadcasted_iota(I32, cand.shape, 0).astype(F32)
        sc, ee = [], []
        for _ in range(PEER_TOPK):
            m, first, e_first = _first_max_rows(cand, cand_e)
            sc.append(m)
            ee.append(e_first)
            cand = jnp.where(cidx == first, -jnp.inf, cand)
        sc = jnp.concatenate(sc, axis=0)
        ex = jnp.exp(sc - sc[0:1])
        g_rows.append(ex / jnp.sum(ex, axis=0, keepdims=True))
        e_rows.append(jnp.concatenate(ee, axis=0))
    e_t = jnp.concatenate(e_rows, axis=0)
    g_t = jnp.concatenate(g_rows, axis=0)
    eidx_ref[...] = jnp.transpose(e_t).astype(I32)
    gate_ref[...] = jnp.transpose(g_t)


def _route(x2, oa, oc, osel, ow, gates, gexp, wo, gf, wq, k1, k2, tm):
    n, d = x2.shape
    row = lambda w: pl.BlockSpec((tm, w), lambda i: (i, 0))
    full = lambda a: pl.BlockSpec(a.shape, lambda i: (0,) * a.ndim)
    nk = PEER_HEADS * PEER_TOPK
    return pl.pallas_call(
        _route_kernel,
        grid=(n // tm,),
        in_specs=[row(d), row(oa.shape[1]), row(oc.shape[1]), row(osel.shape[1]), row(ow.shape[1]), row(LANES),
                  full(gexp), full(wo), full(gf), full(wq), full(k1), full(k2)],
        out_specs=[pl.BlockSpec((tm, SUBLANES, LANES), lambda i: (i, 0, 0)), row(nk), row(nk)],
        out_shape=[jax.ShapeDtypeStruct((n, SUBLANES, LANES), F32), jax.ShapeDtypeStruct((n, nk), I32),
                   jax.ShapeDtypeStruct((n, nk), F32)],
        compiler_params=_params("parallel"),
        name="route",
    )(x2, oa, oc, osel, ow, gates, gexp, wo, gf, wq, k1, k2)


def _fold8(p):
    sub = lax.broadcasted_iota(I32, (SUBLANES, LANES), 0)
    lo4 = sub < 4
    q = [jnp.where(lo4, p[j], p[j + 4]) + pltpu.roll(jnp.where(lo4, p[j + 4], p[j]), 4, 0) for j in range(4)]
    m2 = (sub & 3) < 2
    r = [jnp.where(m2, q[j] + pltpu.roll(q[j], 6, 0), q[j + 2] + pltpu.roll(q[j + 2], 2, 0)) for j in range(2)]
    m1 = (sub & 1) == 0
    return jnp.where(m1, r[0] + pltpu.roll(r[0], 7, 0), r[1] + pltpu.roll(r[1], 1, 0))


def _peer_kernel(e0_ref, e1_ref, e2_ref, x_ref, gate_ref, gffn_ref, gfin_ref, tab_ref, o_ref,
                 buf, sem, a_sc, cb_sc, *, tb):
    step = pl.program_id(0)
    nsteps = pl.num_programs(0)
    slot = step % PEER_SLOTS
    fill = (step + PEER_SLOTS - 1) % PEER_SLOTS
    nk = PEER_HEADS * PEER_TOPK
    npair = tb * nk

    def issue_rows(eref, dst_slot, t, k0, k1):
        erow = eref.at[t]
        for k in range(k0, k1):
            copy = pltpu.make_async_copy(tab_ref.at[erow[k]], buf.at[dst_slot, t * nk + k], sem.at[dst_slot])
            copy.start(priority=k % 2)

    def wait_slot(s):
        pltpu.make_async_copy(tab_ref.at[pl.ds(0, npair)], buf.at[s], sem.at[s]).wait()

    @pl.when(step == 0)
    def _():
        def first(t, carry):
            issue_rows(e0_ref, 0, t, 0, nk)
            return carry
        lax.fori_loop(0, tb, first, 0)
        def second(t, carry):
            issue_rows(e1_ref, 1, t, 0, nk)
            return carry
        lax.fori_loop(0, tb, second, 0)

    eye = jnp.where(lax.broadcasted_iota(I32, (nk, nk), 0) == lax.broadcasted_iota(I32, (nk, nk), 1),
                    1.0, 0.0).astype(BF16)
    gr = gate_ref[...]
    g0 = gr.astype(BF16)
    r1 = gr - g0.astype(F32)
    g1 = r1.astype(BF16)
    g2 = (r1 - g1.astype(F32)).astype(BF16)
    dn = (((1,), (1,)), ((), ()))
    gate_t = (lax.dot_general(eye, g0, dn, preferred_element_type=F32)
              + lax.dot_general(eye, g1, dn, preferred_element_type=F32)
              + lax.dot_general(eye, g2, dn, preferred_element_type=F32))

    wait_slot(slot)

    lane_tb = lax.broadcasted_iota(I32, (nk, tb), 1)
    inv_d = 1.0 / (SUBLANES * LANES)
    a_sc[...] = jnp.zeros(a_sc.shape, F32)

    ngroup = nk // SUBLANES
    rows_p1 = 6
    rows_p2 = SUBLANES - rows_p1

    def dots(t, carry):
        x8 = x_ref[t]
        ms = jnp.sum(jnp.sum(x8 * x8, axis=1, keepdims=True), axis=0, keepdims=True) * inv_d
        h8 = x8 * lax.rsqrt(ms + RMS_EPS) * gffn_ref[...]
        base = t * nk
        folded = []
        for gi in range(nk // SUBLANES):
            blk = buf[slot, pl.ds(base + gi * SUBLANES, SUBLANES), 0:SUBLANES, :]
            folded.append(_fold8([blk[j] * h8 for j in range(SUBLANES)]))
            issue_rows(e2_ref, fill, t, gi * rows_p1, (gi + 1) * rows_p1)
        a = jnp.sum(jnp.concatenate(folded, axis=0), axis=1, keepdims=True)
        a_sc[...] = jnp.where(lane_tb == t, a, a_sc[...])
        return carry

    lax.fori_loop(0, tb, dots, 0, unroll=2)

    a_sc[...] = gate_t * _gelu(a_sc[...])

    def combine(t, carry):
        base = t * nk
        w_col = jnp.sum(jnp.where(lane_tb == t, a_sc[...], 0.0), axis=1, keepdims=True)
        cb_sc[t] = jnp.broadcast_to(w_col, (nk, LANES))
        accs = [jnp.zeros((SUBLANES, LANES), F32) for _ in range(4)]
        for gi in range(ngroup):
            for k in range(gi * SUBLANES, (gi + 1) * SUBLANES):
                ck = jnp.broadcast_to(cb_sc[t, k:k + 1, :], (SUBLANES, LANES))
                accs[k % 4] = accs[k % 4] + ck * buf[slot, base + k, SUBLANES:2 * SUBLANES, :]
            issue_rows(e2_ref, fill, t, ngroup * rows_p1 + gi * rows_p2, ngroup * rows_p1 + (gi + 1) * rows_p2)
        y8 = x_ref[t] + ((accs[0] + accs[1]) + (accs[2] + accs[3]))
        ms2 = jnp.sum(jnp.sum(y8 * y8, axis=1, keepdims=True), axis=0, keepdims=True) * inv_d
        o_ref[t] = y8 * lax.rsqrt(ms2 + RMS_EPS) * gfin_ref[...]
        return carry

    lax.fori_loop(0, tb, combine, 0, unroll=8)

    @pl.when(step == nsteps - 1)
    def _():
        for ahead in range(1, PEER_SLOTS):
            wait_slot((step + ahead) % PEER_SLOTS)


def _peer(eidx, x1s, gate, gffn8, gfin8, table, tb):
    n = x1s.shape[0]
    nk = PEER_HEADS * PEER_TOPK
    nsteps = n // tb
    kern = functools.partial(_peer_kernel, tb=tb)
    return pl.pallas_call(
        kern,
        grid=(nsteps,),
        in_specs=[pl.BlockSpec((tb, nk), lambda i: (i, 0), memory_space=pltpu.SMEM),
                  pl.BlockSpec((tb, nk), lambda i: ((i + 1) % nsteps, 0), memory_space=pltpu.SMEM),
                  pl.BlockSpec((tb, nk), lambda i: ((i + PEER_SLOTS - 1) % nsteps, 0), memory_space=pltpu.SMEM),
                  pl.BlockSpec((tb, SUBLANES, LANES), lambda i: (i, 0, 0)),
                  pl.BlockSpec((tb, nk), lambda i: (i, 0)),
                  pl.BlockSpec((SUBLANES, LANES), lambda i: (0, 0)),
                  pl.BlockSpec((SUBLANES, LANES), lambda i: (0, 0)),
                  pl.BlockSpec(memory_space=pl.ANY)],
        out_specs=pl.BlockSpec((tb, SUBLANES, LANES), lambda i: (i, 0, 0)),
        out_shape=jax.ShapeDtypeStruct((n, SUBLANES, LANES), F32),
        scratch_shapes=[pltpu.VMEM((PEER_SLOTS, tb * nk, 2 * SUBLANES, LANES), F32),
                        pltpu.SemaphoreType.DMA((PEER_SLOTS,)),
                        pltpu.VMEM((nk, tb), F32),
                        pltpu.VMEM((tb, nk, LANES), F32)],
        compiler_params=_params("arbitrary"),
        name="peer",
    )(eidx, eidx, eidx, x1s, gate, gffn8, gfin8, table)


def _overlap_t(t):
    nc = (t - CMP_LEN) // CMP_STRIDE + 1
    ns = t // SEL_BLOCK
    ncp = t // CMP_STRIDE
    cstart = np.arange(nc) * CMP_STRIDE
    bstart = np.arange(ns) * SEL_BLOCK
    lo = np.maximum(cstart[:, None], bstart[None, :])
    hi = np.minimum(cstart[:, None] + CMP_LEN, bstart[None, :] + SEL_BLOCK)
    ov = np.clip(hi - lo, 0, None).astype(np.float32) / CMP_LEN
    out = np.zeros((ns, ncp), np.float32)
    out[:, :nc] = ov.T
    return out, nc


def _gate_expand():
    ex = np.zeros((NSA_N_GATES, LANES, NSA_HEADS * HEAD_DIM), np.float32)
    for j in range(NSA_N_GATES):
        for h in range(NSA_HEADS):
            ex[j, h * NSA_N_GATES + j, h * HEAD_DIM:(h + 1) * HEAD_DIM] = 1.0
    return ex


def _layer(x, positions, norm_attn, w_in, attn_sinks, cmp_k_pos, cmp_k_w1, cmp_k_w2, cmp_v_pos, cmp_v_w1,
           cmp_v_w2, w_out, norm_ffn, w_query, sub_keys_1, sub_keys_2, expert_down, expert_up, norm_out,
           tm_in=512, tq_cmp=512, tq_sel=512, tk_sel=512, tq_swa=128, tq_win=256, tm_route=256, tb_peer=16):
    b, t, d = x.shape
    n = b * t
    x2 = x.reshape(n, d)
    half = HEAD_DIM // 2
    inv_freq = ROPE_THETA ** (-jnp.arange(half, dtype=F32) / half)
    invf = jnp.tile(inv_freq, LANES // half).reshape(1, LANES)
    w_pad = jnp.pad(w_in, ((0, 0), (0, IN_PAD - IN_WIDTH))).astype(BF16)
    pr = _inproj(x2, positions.reshape(n, 1), norm_attn.reshape(1, d), w_pad, invf, min(tm_in, n))
    r3 = lambda a: a.reshape(b, t, a.shape[-1])

    o_a = _band(r3(pr["qa"]), r3(pr["ka"]), r3(pr["va"]), attn_sinks.astype(F32), SWA_WINDOW, tq_swa,
                SWA_HEADS, SWA_KV_HEADS)
    kcmp, vcmp = _compress(r3(pr["kc"]), r3(pr["vc"]),
                           cmp_k_pos.reshape(1, -1), cmp_k_w1.astype(BF16), cmp_k_w2.astype(BF16),
                           cmp_v_pos.reshape(1, -1), cmp_v_w1.astype(BF16), cmp_v_w2.astype(BF16))
    ovt, n_cmp = _overlap_t(t)
    qb3 = r3(pr["qb"])
    o_cmp, sel = _cmpsel(qb3, kcmp, vcmp, jnp.asarray(ovt), tq_cmp, n_cmp)
    tk_sel = min(tk_sel, t)
    vst4 = jnp.swapaxes(r3(pr["vs"]).reshape(b, t // tk_sel, tk_sel, LANES), 2, 3)
    o_sel = _selattn(qb3, r3(pr["ks"]), vst4, sel, tq_sel, tk_sel)
    o_win = _band(qb3, r3(pr["kw"]), r3(pr["vw"]), None, NSA_WINDOW, tq_win, NSA_HEADS, NSA_KV_HEADS)

    x1, eidx, gate = _route(x2, o_a.reshape(n, -1), o_cmp.reshape(n, -1), o_sel.reshape(n, -1),
                            o_win.reshape(n, -1), pr["gates"], jnp.asarray(_gate_expand(), BF16),
                            w_out.astype(BF16), norm_ffn.reshape(1, d), w_query.astype(BF16),
                            sub_keys_1.astype(BF16), sub_keys_2.astype(BF16), min(tm_route, n))
    ne = expert_down.shape[0]
    table = jnp.stack([expert_down.reshape(ne, SUBLANES, LANES), expert_up.reshape(ne, SUBLANES, LANES)],
                      axis=1).reshape(ne, 2 * SUBLANES, LANES)
    y = _peer(eidx, x1, gate, norm_ffn.reshape(SUBLANES, LANES),
              norm_out.reshape(SUBLANES, LANES), table, tb_peer)
    return y.reshape(b, t, d)


def kernel(x, positions, norm_attn, w_in, attn_sinks, cmp_k_pos, cmp_k_w1, cmp_k_w2, cmp_v_pos, cmp_v_w1,
           cmp_v_w2, w_out, norm_ffn, peer_w_query, peer_sub_keys_1, peer_sub_keys_2, peer_expert_down,
           peer_expert_up, norm_f):
    assert norm_attn.shape[0] == 1, "single-layer block"
    return _layer(x, positions, norm_attn[0], w_in[0], attn_sinks[0], cmp_k_pos[0], cmp_k_w1[0], cmp_k_w2[0],
                  cmp_v_pos[0], cmp_v_w1[0], cmp_v_w2[0], w_out[0], norm_ffn[0], peer_w_query[0],
                  peer_sub_keys_1[0], peer_sub_keys_2[0], peer_expert_down[0], peer_expert_up[0], norm_f)
```

```python
import functools

import numpy as np
import jax
import jax.numpy as jnp
from jax import lax
from jax.experimental import pallas as pl
from jax.experimental.pallas import tpu as pltpu

F32 = jnp.float32
BF16 = jnp.bfloat16
I32 = jnp.int32

HEAD_DIM = 64
ROPE_THETA = 10000.0
RMS_EPS = 1e-6
NEG_BIG = -1e30

SWA_HEADS = 8
SWA_KV_HEADS = 2
SWA_WINDOW = 128

NSA_HEADS = 8
NSA_KV_HEADS = 2
CMP_LEN = 32
CMP_STRIDE = 16
CMP_HIDDEN = 256
SEL_BLOCK = 64
SEL_TOPN = 16
NSA_WINDOW = 512
NSA_N_GATES = 3

PEER_HEADS = 8
N_KEYS = 128
PEER_QDIM = 256
PEER_TOPK = 16
PEER_SLOTS = 3

LANES = 128
SUBLANES = 8
VMEM_LIMIT = 56 * 1024 * 1024

_IN_SPLITS = (("qa", 512), ("ka", 128), ("va", 128), ("qb", 512), ("kc", 128), ("vc", 128),
              ("ks", 128), ("vs", 128), ("kw", 128), ("vw", 128), ("gl", 24))
IN_WIDTH = sum(w for _, w in _IN_SPLITS)
IN_PAD = -(-IN_WIDTH // LANES) * LANES
_IN_OFF = {}
_o = 0
for _n, _w in _IN_SPLITS:
    _IN_OFF[_n] = _o
    _o += _w
_ROPED = ("qa", "ka", "qb", "kc", "ks", "kw")


def _gelu(x):
    return 0.5 * x * (1.0 + lax.erf(x * 0.7071067811865476))


def _params(*sem):
    return pltpu.CompilerParams(dimension_semantics=sem, vmem_limit_bytes=VMEM_LIMIT)


def _inproj_kernel(x_ref, pos_ref, g_ref, w_ref, invf_ref,
                   qa_ref, ka_ref, va_ref, qb_ref, kc_ref, vc_ref, ks_ref, vs_ref, kw_ref, vw_ref, gt_ref):
    x = x_ref[...]
    h = x * lax.rsqrt(jnp.mean(x * x, axis=-1, keepdims=True) + RMS_EPS) * g_ref[...]
    proj = jnp.dot(h.astype(BF16), w_ref[...], preferred_element_type=F32)
    ang = pos_ref[...].astype(F32) * invf_ref[...]
    lane = lax.broadcasted_iota(I32, ang.shape, 1)
    lo = (lane & (HEAD_DIM - 1)) < (HEAD_DIM // 2)
    cos = jnp.cos(ang)
    sin = jnp.sin(ang)
    sin_s = jnp.where(lo, -sin, sin)
    outs = {"qa": qa_ref, "ka": ka_ref, "va": va_ref, "qb": qb_ref, "kc": kc_ref, "vc": vc_ref,
            "ks": ks_ref, "vs": vs_ref, "kw": kw_ref, "vw": vw_ref}
    for name, width in _IN_SPLITS[:-1]:
        off = _IN_OFF[name]
        ref = outs[name]
        for i in range(width // LANES):
            v = proj[:, off + LANES * i: off + LANES * (i + 1)]
            if name in _ROPED:
                rot = jnp.where(lo, pltpu.roll(v, LANES - HEAD_DIM // 2, 1), pltpu.roll(v, HEAD_DIM // 2, 1))
                v = v * cos + rot * sin_s
            ref[:, LANES * i: LANES * (i + 1)] = v.astype(ref.dtype)
    gl = proj[:, _IN_OFF["gl"]: _IN_OFF["gl"] + LANES]
    gt_ref[...] = jax.nn.sigmoid(gl)


def _inproj(x2, pos2, g, w_pad, invf, tm):
    n, d = x2.shape
    row = lambda w: pl.BlockSpec((tm, w), lambda i: (i, 0))
    full = lambda a: pl.BlockSpec(a.shape, lambda i: (0,) * a.ndim)
    out_dt = {"qa": BF16, "ka": BF16, "va": BF16, "qb": BF16, "kc": F32, "vc": F32,
              "ks": BF16, "vs": BF16, "kw": BF16, "vw": BF16}
    names = [nm for nm, _ in _IN_SPLITS[:-1]]
    widths = dict(_IN_SPLITS)
    out_shape = [jax.ShapeDtypeStruct((n, widths[nm]), out_dt[nm]) for nm in names]
    out_shape.append(jax.ShapeDtypeStruct((n, LANES), F32))
    out_specs = [row(widths[nm]) for nm in names] + [row(LANES)]
    res = pl.pallas_call(
        _inproj_kernel,
        grid=(n // tm,),
        in_specs=[row(d), row(1), full(g), full(w_pad), full(invf)],
        out_specs=out_specs,
        out_shape=out_shape,
        compiler_params=_params("parallel"),
        name="inproj",
    )(x2, pos2, g, w_pad, invf)
    out = dict(zip(names, res[:-1]))
    out["gates"] = res[-1]
    return out


def _compress_kernel(kc_ref, vc_ref, kpos_ref, kw1_ref, kw2_ref, vpos_ref, vw1_ref, vw2_ref, ko_ref, vo_ref):
    ncp = ko_ref.shape[0]
    for src, pos_ref, w1_ref, w2_ref, out_ref in ((kc_ref, kpos_ref, kw1_ref, kw2_ref, ko_ref),
                                                  (vc_ref, vpos_ref, vw1_ref, vw2_ref, vo_ref)):
        bias = jnp.dot(pos_ref[...].astype(BF16), w1_ref[...], preferred_element_type=F32)
        acc_a = [jnp.zeros((ncp, CMP_HIDDEN), F32) for _ in range(2)]
        acc_b = [jnp.zeros((ncp, CMP_HIDDEN), F32) for _ in range(2)]
        for j in range(CMP_STRIDE):
            pj = src[pl.ds(j, ncp, stride=CMP_STRIDE), :].astype(BF16)
            wa = w1_ref[j * HEAD_DIM:(j + 1) * HEAD_DIM, :]
            wb = w1_ref[(CMP_STRIDE + j) * HEAD_DIM:(CMP_STRIDE + j + 1) * HEAD_DIM, :]
            for hh in range(2):
                ph = pj[:, hh * HEAD_DIM:(hh + 1) * HEAD_DIM]
                acc_a[hh] = acc_a[hh] + jnp.dot(ph, wa, preferred_element_type=F32)
                acc_b[hh] = acc_b[hh] + jnp.dot(ph, wb, preferred_element_type=F32)
        for hh in range(2):
            h1 = acc_a[hh] + pltpu.roll(acc_b[hh], ncp - 1, 0) + bias
            out = jnp.dot(_gelu(h1).astype(BF16), w2_ref[...], preferred_element_type=F32)
            out_ref[:, hh * HEAD_DIM:(hh + 1) * HEAD_DIM] = out


def _compress(kc3, vc3, kpos, kw1, kw2, vpos, vw1, vw2):
    b, t, w = kc3.shape
    ncp = t // CMP_STRIDE
    per_b = pl.BlockSpec((None, t, w), lambda i: (i, 0, 0))
    full = lambda a: pl.BlockSpec(a.shape, lambda i: (0,) * a.ndim)
    out_spec = pl.BlockSpec((None, ncp, w), lambda i: (i, 0, 0))
    return pl.pallas_call(
        _compress_kernel,
        grid=(b,),
        in_specs=[per_b, per_b, full(kpos), full(kw1), full(kw2), full(vpos), full(vw1), full(vw2)],
        out_specs=[out_spec, out_spec],
        out_shape=[jax.ShapeDtypeStruct((b, ncp, w), F32)] * 2,
        compiler_params=_params("parallel"),
        name="compress",
    )(kc3, vc3, kpos, kw1, kw2, vpos, vw1, vw2)


def _cmpsel_kernel(q_ref, kc_ref, vc_ref, ovt_ref, o_ref, sel_ref, *, tq, n_cmp):
    i = pl.program_id(1)
    t0 = i * tq
    g = NSA_HEADS // NSA_KV_HEADS
    ncp = kc_ref.shape[0]
    ns = ovt_ref.shape[0]
    rows = g * tq
    tpos = t0 + (lax.broadcasted_iota(I32, (rows, ncp), 0) & (tq - 1))
    cidx = lax.broadcasted_iota(I32, (rows, ncp), 1)
    vis = (cidx * CMP_STRIDE + (CMP_LEN - 1) <= tpos) & (cidx < n_cmp)
    jidx = lax.broadcasted_iota(I32, (ns, tq), 0)
    tcol = t0 + lax.broadcasted_iota(I32, (ns, tq), 1)
    cur = tcol // SEL_BLOCK
    forced = (jidx == 0) | (jidx == cur) | (jidx == cur - 1)
    valid = jidx * SEL_BLOCK <= tcol
    for hk in range(NSA_KV_HEADS):
        kk = kc_ref[:, hk * HEAD_DIM:(hk + 1) * HEAD_DIM].astype(BF16)
        vv = vc_ref[:, hk * HEAD_DIM:(hk + 1) * HEAD_DIM].astype(BF16)
        q4 = jnp.concatenate(
            [q_ref[:, (hk * g + gi) * HEAD_DIM:(hk * g + gi + 1) * HEAD_DIM] for gi in range(g)], axis=0)
        s = lax.dot_general(q4, kk, (((1,), (1,)), ((), ())), preferred_element_type=F32) * (HEAD_DIM ** -0.5)
        s = jnp.where(vis, s, NEG_BIG)
        m = jnp.max(s, axis=-1, keepdims=True)
        e = jnp.exp(s - m)
        p = jnp.where(vis, e / jnp.sum(e, axis=-1, keepdims=True), 0.0)
        o = jnp.dot(p.astype(BF16), vv, preferred_element_type=F32)
        for gi in range(g):
            h = hk * g + gi
            o_ref[:, h * HEAD_DIM:(h + 1) * HEAD_DIM] = o[gi * tq:(gi + 1) * tq]
        psum = p[0:tq]
        for gi in range(1, g):
            psum = psum + p[gi * tq:(gi + 1) * tq]
        imp_t = lax.dot_general(ovt_ref[...], psum, (((1,), (1,)), ((), ())),
                                precision=lax.Precision.HIGHEST, preferred_element_type=F32)
        score = jnp.where(forced, jnp.inf, jnp.where(valid, imp_t, -jnp.inf))
        cnt = jnp.zeros((ns, tq), F32)
        for r in range(ns):
            row = score[r:r + 1, :]
            beats = (row > score) | ((row >= score) & (jidx > r))
            cnt = cnt + jnp.where(beats, 1.0, 0.0)
        sel_ref[hk] = jnp.where((cnt < SEL_TOPN) & (score > -jnp.inf), 1.0, 0.0)


def _selattn_kernel(q_ref, k_ref, vt_ref, sel_ref, o_ref, m_sc, l_sc, acc_sc, *, tq, tk):
    i = pl.program_id(1)
    t0 = i * tq
    g = NSA_HEADS // NSA_KV_HEADS
    rows = g * tq
    assert tk % tq == 0, "one key tile must cover the whole diagonal of a query tile"
    n_below = (i * tq) // tk
    nb = tk // SEL_BLOCK
    qpos = t0 + (lax.broadcasted_iota(I32, (tk, rows), 1) & (tq - 1))
    krow = lax.broadcasted_iota(I32, (tk, rows), 0)
    for hk in range(NSA_KV_HEADS):
        qt = jnp.concatenate(
            [jnp.transpose(q_ref[:, (hk * g + gi) * HEAD_DIM:(hk * g + gi + 1) * HEAD_DIM].astype(F32))
             for gi in range(g)], axis=1)
        qt = (qt * (HEAD_DIM ** -0.5)).astype(BF16)
        m_sc[...] = jnp.full(m_sc.shape, NEG_BIG, F32)
        l_sc[...] = jnp.zeros(l_sc.shape, F32)
        acc_sc[...] = jnp.zeros(acc_sc.shape, F32)

        def tile(kt, diagonal):
            ks = pl.multiple_of(kt * tk, tk)
            kk = k_ref[pl.ds(ks, tk), hk * HEAD_DIM:(hk + 1) * HEAD_DIM]
            s = jnp.dot(kk, qt, preferred_element_type=F32)
            bias = jnp.concatenate(
                [jnp.broadcast_to(
                    jnp.concatenate([(1.0 - sel_ref[hk, pl.ds(kt * nb + jj, 1), :]) * NEG_BIG] * g, axis=1),
                    (SEL_BLOCK, rows)) for jj in range(nb)], axis=0)
            s = s + bias
            if diagonal:
                s = jnp.where(ks + krow <= qpos, s, NEG_BIG)
            m_old = m_sc[...]
            m_new = jnp.maximum(m_old, jnp.max(s, axis=0, keepdims=True))
            alpha = jnp.exp(m_old - m_new)
            p = jnp.exp(s - m_new)
            l_sc[...] = alpha * l_sc[...] + jnp.sum(p, axis=0, keepdims=True)
            vt = vt_ref[kt, hk * HEAD_DIM:(hk + 1) * HEAD_DIM, :]
            acc_sc[...] = alpha * acc_sc[...] + jnp.dot(vt, p.astype(BF16), preferred_element_type=F32)
            m_sc[...] = m_new

        def below_diagonal(kt, carry):
            tile(kt, False)
            return carry

        lax.fori_loop(0, n_below, below_diagonal, 0)
        tile(n_below, True)
        o_t = acc_sc[...] / l_sc[...]
        for gi in range(g):
            h = hk * g + gi
            o_ref[:, h * HEAD_DIM:(h + 1) * HEAD_DIM] = jnp.transpose(o_t[:, gi * tq:(gi + 1) * tq])


def _nsa_kernel(q_ref, kc_ref, vc_ref, ovt_ref, k_ref, vt_ref, ocmp_ref, osel_ref, sel_sc, m_sc, l_sc, acc_sc,
                *, tq, tk, n_cmp):
    _cmpsel_kernel(q_ref, kc_ref, vc_ref, ovt_ref, ocmp_ref, sel_sc, tq=tq, n_cmp=n_cmp)
    _selattn_kernel(q_ref, k_ref, vt_ref, sel_sc, osel_ref, m_sc, l_sc, acc_sc, tq=tq, tk=tk)


def _nsa(qb3, kcmp, vcmp, ovt, ks3, vst4, tq, tk, n_cmp):
    b, t, w = qb3.shape
    g = NSA_HEADS // NSA_KV_HEADS
    ncp = kcmp.shape[1]
    ns = ovt.shape[0]
    kern = functools.partial(_nsa_kernel, tq=tq, tk=tk, n_cmp=n_cmp)
    per_q = pl.BlockSpec((None, tq, w), lambda bi, i: (bi, i, 0))
    return pl.pallas_call(
        kern,
        grid=(b, t // tq),
        in_specs=[per_q,
                  pl.BlockSpec((None, ncp, LANES), lambda bi, i: (bi, 0, 0)),
                  pl.BlockSpec((None, ncp, LANES), lambda bi, i: (bi, 0, 0)),
                  pl.BlockSpec(ovt.shape, lambda bi, i: (0, 0)),
                  pl.BlockSpec((None, t, LANES), lambda bi, i: (bi, 0, 0)),
                  pl.BlockSpec((None, t // tk, LANES, tk), lambda bi, i: (bi, 0, 0, 0))],
        out_specs=[per_q, per_q],
        out_shape=[jax.ShapeDtypeStruct((b, t, w), F32)] * 2,
        scratch_shapes=[pltpu.VMEM((NSA_KV_HEADS, ns, tq), F32),
                        pltpu.VMEM((1, g * tq), F32), pltpu.VMEM((1, g * tq), F32),
                        pltpu.VMEM((HEAD_DIM, g * tq), F32)],
        compiler_params=_params("parallel", "arbitrary"),
        name="nsa_cmp_sel",
    )(qb3, kcmp, vcmp, ovt, ks3, vst4)


def _band_kernel(*refs, window, tq, span, n_heads, n_kv, has_sink):
    if has_sink:
        sink_ref, q_ref, k_ref, vt_ref, o_ref = refs
    else:
        q_ref, k_ref, vt_ref, o_ref = refs
    i = pl.program_id(1)
    t0 = i * tq
    start = pl.multiple_of(jnp.maximum(t0 + tq - span, 0), tq)
    tile0 = start // LANES
    g = n_heads // n_kv
    rows = g * tq
    qpos = t0 + (lax.broadcasted_iota(I32, (span, rows), 1) & (tq - 1))
    kpos = start + lax.broadcasted_iota(I32, (span, rows), 0)
    diff = qpos - kpos
    mask = (diff >= 0) & (diff < window)
    for hk in range(n_kv):
        kk = k_ref[pl.ds(start, span), hk * HEAD_DIM:(hk + 1) * HEAD_DIM]
        qt = jnp.concatenate(
            [jnp.transpose(q_ref[:, (hk * g + gi) * HEAD_DIM:(hk * g + gi + 1) * HEAD_DIM].astype(F32))
             for gi in range(g)], axis=1)
        qt = (qt * (HEAD_DIM ** -0.5)).astype(BF16)
        s = jnp.where(mask, jnp.dot(kk, qt, preferred_element_type=F32), NEG_BIG)
        m = jnp.max(s, axis=0, keepdims=True)
        if has_sink:
            sink = jnp.concatenate([jnp.full((1, tq), sink_ref[hk * g + gi], F32) for gi in range(g)], axis=1)
            m = jnp.maximum(m, sink)
        p = jnp.exp(s - m)
        l = jnp.sum(p, axis=0, keepdims=True)
        if has_sink:
            l = l + jnp.exp(sink - m)
        vt = jnp.concatenate([vt_ref[tile0 + j, hk * HEAD_DIM:(hk + 1) * HEAD_DIM, :]
                              for j in range(span // LANES)], axis=1)
        o_t = jnp.dot(vt, p.astype(BF16), preferred_element_type=F32) / l
        for gi in range(g):
            h = hk * g + gi
            o_ref[:, h * HEAD_DIM:(h + 1) * HEAD_DIM] = jnp.transpose(o_t[:, gi * tq:(gi + 1) * tq])


def _band(q3, k3, v3, sinks, window, tq, n_heads, n_kv):
    b, t, w = q3.shape
    nprev = -(-(window - 1) // tq)
    span = min((nprev + 1) * tq, t)
    has_sink = sinks is not None
    vt4 = jnp.swapaxes(v3.reshape(b, t // LANES, LANES, LANES), 2, 3)
    kern = functools.partial(_band_kernel, window=window, tq=tq, span=span, n_heads=n_heads, n_kv=n_kv,
                             has_sink=has_sink)
    in_specs = [pl.BlockSpec((None, tq, w), lambda bi, i: (bi, i, 0)),
                pl.BlockSpec((None, t, LANES), lambda bi, i: (bi, 0, 0)),
                pl.BlockSpec((None, t // LANES, LANES, LANES), lambda bi, i: (bi, 0, 0, 0))]
    args = [q3, k3, vt4]
    if has_sink:
        in_specs = [pl.BlockSpec(memory_space=pltpu.SMEM)] + in_specs
        args = [sinks] + args
    return pl.pallas_call(
        kern,
        grid=(b, t // tq),
        in_specs=in_specs,
        out_specs=pl.BlockSpec((None, tq, w), lambda bi, i: (bi, i, 0)),
        out_shape=jax.ShapeDtypeStruct((b, t, w), F32),
        compiler_params=_params("parallel", "parallel"),
        name="band_sink" if has_sink else "band_win",
    )(*args)


def _first_max_rows(val, payload=None):
    rows, n = val.shape
    sub = lax.broadcasted_iota(I32, (SUBLANES, n), 0).astype(F32)
    parts = [[val[r:r + SUBLANES] for r in range(0, rows, SUBLANES)],
             [sub + float(r) for r in range(0, rows, SUBLANES)]]
    if payload is not None:
        parts.append([payload[r:r + SUBLANES] for r in range(0, rows, SUBLANES)])
    while len(parts[0]) > 1:
        nxt = [[] for _ in parts]
        for j in range(0, len(parts[0]) - 1, 2):
            take_right = parts[0][j + 1] > parts[0][j]
            for dst, src in zip(nxt, parts):
                dst.append(jnp.where(take_right, src[j + 1], src[j]))
        if len(parts[0]) % 2:
            for dst, src in zip(nxt, parts):
                dst.append(src[-1])
        parts = nxt
    v8, i8 = parts[0][0], parts[1][0]
    m = jnp.max(v8, axis=0, keepdims=True)
    first = jnp.min(jnp.where(v8 == m, i8, float(rows)), axis=0, keepdims=True)
    if payload is None:
        return m, first
    return m, first, jnp.max(jnp.where(i8 == first, parts[2][0], -1.0), axis=0, keepdims=True)


def _topk_rows(s, k, ridx):
    vals, idxs = [], []
    for _ in range(k):
        m, first = _first_max_rows(s)
        vals.append(m)
        idxs.append(first)
        s = jnp.where(ridx == first, -jnp.inf, s)
    return jnp.concatenate(vals, axis=0), jnp.concatenate(idxs, axis=0)


def _peer_candidates(t1, i1, t2, i2):
    vals = [t1[0:1] + t2]
    eids = [i1[0:1] * float(N_KEYS) + i2]
    for a in range(1, 8):
        vals.append(t1[a:a + 1] + t2[0:8])
        eids.append(i1[a:a + 1] * float(N_KEYS) + i2[0:8])
    vals.append(t1[8:16] + t2[0:1])
    eids.append(i1[8:16] * float(N_KEYS) + i2[0:1])
    return jnp.concatenate(vals, axis=0), jnp.concatenate(eids, axis=0)


def _route_kernel(x_ref, oa_ref, oc_ref, os_ref, ow_ref, gt_ref, gexp_ref, wo_ref, gf_ref, wq_ref,
                  k1_ref, k2_ref, x1_ref, eidx_ref, gate_ref):
    tm = x_ref.shape[0]
    gt = gt_ref[...]
    g_hi = gt.astype(BF16)
    g_lo = (gt - g_hi.astype(F32)).astype(BF16)
    ob = jnp.zeros(oc_ref.shape, F32)
    for j, br in enumerate((oc_ref, os_ref, ow_ref)):
        ex = gexp_ref[j]
        gj = (jnp.dot(g_hi, ex, preferred_element_type=F32) + jnp.dot(g_lo, ex, preferred_element_type=F32))
        ob = ob + gj * br[...]
    half = oa_ref.shape[1]
    mixed = (jnp.dot(oa_ref[...].astype(BF16), wo_ref[0:half, :], preferred_element_type=F32)
             + jnp.dot(ob.astype(BF16), wo_ref[half:, :], preferred_element_type=F32))
    x1 = x_ref[...] + mixed
    x1_ref[...] = x1.reshape(tm, SUBLANES, LANES)
    h2 = x1 * lax.rsqrt(jnp.mean(x1 * x1, axis=-1, keepdims=True) + RMS_EPS) * gf_ref[...]
    q = jnp.dot(h2.astype(BF16), wq_ref[...], preferred_element_type=F32)
    hq = PEER_QDIM // 2
    ridx = lax.broadcasted_iota(I32, (N_KEYS, tm), 0).astype(F32)
    e_rows, g_rows = [], []
    for h in range(PEER_HEADS):
        q1 = q[:, h * PEER_QDIM: h * PEER_QDIM + hq].astype(BF16)
        q2 = q[:, h * PEER_QDIM + hq:(h + 1) * PEER_QDIM].astype(BF16)
        s1 = lax.dot_general(k1_ref[...], q1, (((1,), (1,)), ((), ())), preferred_element_type=F32)
        s2 = lax.dot_general(k2_ref[...], q2, (((1,), (1,)), ((), ())), preferred_element_type=F32)
        t1, i1 = _topk_rows(s1, PEER_TOPK, ridx)
        t2, i2 = _topk_rows(s2, PEER_TOPK, ridx)
        cand, cand_e = _peer_candidates(t1, i1, t2, i2)
        cidx = lax.broadcasted_iota(I32, cand.shape, 0).astype(F32)
        sc, ee = [], []
        for _ in range(PEER_TOPK):
            m, first, e_first = _first_max_rows(cand, cand_e)
            sc.append(m)
            ee.append(e_first)
            cand = jnp.where(cidx == first, -jnp.inf, cand)
        sc = jnp.concatenate(sc, axis=0)
        ex = jnp.exp(sc - sc[0:1])
        g_rows.append(ex / jnp.sum(ex, axis=0, keepdims=True))
        e_rows.append(jnp.concatenate(ee, axis=0))
    e_t = jnp.concatenate(e_rows, axis=0)
    g_t = jnp.concatenate(g_rows, axis=0)
    eidx_ref[...] = jnp.transpose(e_t).astype(I32)
    gate_ref[...] = jnp.transpose(g_t)


def _route(x2, oa, oc, osel, ow, gates, gexp, wo, gf, wq, k1, k2, tm):
    n, d = x2.shape
    row = lambda w: pl.BlockSpec((tm, w), lambda i: (i, 0))
    full = lambda a: pl.BlockSpec(a.shape, lambda i: (0,) * a.ndim)
    nk = PEER_HEADS * PEER_TOPK
    return pl.pallas_call(
        _route_kernel,
        grid=(n // tm,),
        in_specs=[row(d), row(oa.shape[1]), row(oc.shape[1]), row(osel.shape[1]), row(ow.shape[1]), row(LANES),
                  full(gexp), full(wo), full(gf), full(wq), full(k1), full(k2)],
        out_specs=[pl.BlockSpec((tm, SUBLANES, LANES), lambda i: (i, 0, 0)), row(nk), row(nk)],
        out_shape=[jax.ShapeDtypeStruct((n, SUBLANES, LANES), F32), jax.ShapeDtypeStruct((n, nk), I32),
                   jax.ShapeDtypeStruct((n, nk), F32)],
        compiler_params=_params("parallel"),
        name="route",
    )(x2, oa, oc, osel, ow, gates, gexp, wo, gf, wq, k1, k2)


def _fold8(p):
    sub = lax.broadcasted_iota(I32, (SUBLANES, LANES), 0)
    lo4 = sub < 4
    q = [jnp.where(lo4, p[j], p[j + 4]) + pltpu.roll(jnp.where(lo4, p[j + 4], p[j]), 4, 0) for j in range(4)]
    m2 = (sub & 3) < 2
    r = [jnp.where(m2, q[j] + pltpu.roll(q[j], 6, 0), q[j + 2] + pltpu.roll(q[j + 2], 2, 0)) for j in range(2)]
    m1 = (sub & 1) == 0
    return jnp.where(m1, r[0] + pltpu.roll(r[0], 7, 0), r[1] + pltpu.roll(r[1], 1, 0))


def _peer_kernel(e0_ref, e1_ref, e2_ref, x_ref, gate_ref, gffn_ref, gfin_ref, tab_ref, o_ref,
                 buf, sem, a_sc, cb_sc, *, tb):
    step = pl.program_id(0)
    nsteps = pl.num_programs(0)
    slot = step % PEER_SLOTS
    fill = (step + PEER_SLOTS - 1) % PEER_SLOTS
    nk = PEER_HEADS * PEER_TOPK
    npair = tb * nk

    def issue_rows(eref, dst_slot, t, k0, k1):
        erow = eref.at[t]
        for k in range(k0, k1):
            copy = pltpu.make_async_copy(tab_ref.at[erow[k]], buf.at[dst_slot, t * nk + k], sem.at[dst_slot])
            copy.start(priority=k % 2)

    def wait_slot(s):
        pltpu.make_async_copy(tab_ref.at[pl.ds(0, npair)], buf.at[s], sem.at[s]).wait()

    @pl.when(step == 0)
    def _():
        def first(t, carry):
            issue_rows(e0_ref, 0, t, 0, nk)
            return carry
        lax.fori_loop(0, tb, first, 0)
        def second(t, carry):
            issue_rows(e1_ref, 1, t, 0, nk)
            return carry
        lax.fori_loop(0, tb, second, 0)

    eye = jnp.where(lax.broadcasted_iota(I32, (nk, nk), 0) == lax.broadcasted_iota(I32, (nk, nk), 1),
                    1.0, 0.0).astype(BF16)
    gr = gate_ref[...]
    g0 = gr.astype(BF16)
    r1 = gr - g0.astype(F32)
    g1 = r1.astype(BF16)
    g2 = (r1 - g1.astype(F32)).astype(BF16)
    dn = (((1,), (1,)), ((), ()))
    gate_t = (lax.dot_general(eye, g0, dn, preferred_element_type=F32)
              + lax.dot_general(eye, g1, dn, preferred_element_type=F32)
              + lax.dot_general(eye, g2, dn, preferred_element_type=F32))

    wait_slot(slot)

    lane_tb = lax.broadcasted_iota(I32, (nk, tb), 1)
    inv_d = 1.0 / (SUBLANES * LANES)
    a_sc[...] = jnp.zeros(a_sc.shape, F32)

    ngroup = nk // SUBLANES
    rows_p1 = 6
    rows_p2 = SUBLANES - rows_p1

    def dots(t, carry):
        x8 = x_ref[t]
        ms = jnp.sum(jnp.sum(x8 * x8, axis=1, keepdims=True), axis=0, keepdims=True) * inv_d
        h8 = x8 * lax.rsqrt(ms + RMS_EPS) * gffn_ref[...]
        base = t * nk
        folded = []
        for gi in range(nk // SUBLANES):
            blk = buf[slot, pl.ds(base + gi * SUBLANES, SUBLANES), 0:SUBLANES, :]
            folded.append(_fold8([blk[j] * h8 for j in range(SUBLANES)]))
            issue_rows(e2_ref, fill, t, gi * rows_p1, (gi + 1) * rows_p1)
        a = jnp.sum(jnp.concatenate(folded, axis=0), axis=1, keepdims=True)
        a_sc[...] = jnp.where(lane_tb == t, a, a_sc[...])
        return carry

    lax.fori_loop(0, tb, dots, 0, unroll=2)

    a_sc[...] = gate_t * _gelu(a_sc[...])

    def combine(t, carry):
        base = t * nk
        w_col = jnp.sum(jnp.where(lane_tb == t, a_sc[...], 0.0), axis=1, keepdims=True)
        cb_sc[t] = jnp.broadcast_to(w_col, (nk, LANES))
        accs = [jnp.zeros((SUBLANES, LANES), F32) for _ in range(4)]
        for gi in range(ngroup):
            for k in range(gi * SUBLANES, (gi + 1) * SUBLANES):
                ck = jnp.broadcast_to(cb_sc[t, k:k + 1, :], (SUBLANES, LANES))
                accs[k % 4] = accs[k % 4] + ck * buf[slot, base + k, SUBLANES:2 * SUBLANES, :]
            issue_rows(e2_ref, fill, t, ngroup * rows_p1 + gi * rows_p2, ngroup * rows_p1 + (gi + 1) * rows_p2)
        y8 = x_ref[t] + ((accs[0] + accs[1]) + (accs[2] + accs[3]))
        ms2 = jnp.sum(jnp.sum(y8 * y8, axis=1, keepdims=True), axis=0, keepdims=True) * inv_d
        o_ref[t] = y8 * lax.rsqrt(ms2 + RMS_EPS) * gfin_ref[...]
        return carry

    lax.fori_loop(0, tb, combine, 0, unroll=8)

    @pl.when(step == nsteps - 1)
    def _():
        for ahead in range(1, PEER_SLOTS):
            wait_slot((step + ahead) % PEER_SLOTS)


def _peer(eidx, x1s, gate, gffn8, gfin8, table, tb):
    n = x1s.shape[0]
    nk = PEER_HEADS * PEER_TOPK
    nsteps = n // tb
    kern = functools.partial(_peer_kernel, tb=tb)
    return pl.pallas_call(
        kern,
        grid=(nsteps,),
        in_specs=[pl.BlockSpec((tb, nk), lambda i: (i, 0), memory_space=pltpu.SMEM),
                  pl.BlockSpec((tb, nk), lambda i: ((i + 1) % nsteps, 0), memory_space=pltpu.SMEM),
                  pl.BlockSpec((tb, nk), lambda i: ((i + PEER_SLOTS - 1) % nsteps, 0), memory_space=pltpu.SMEM),
                  pl.BlockSpec((tb, SUBLANES, LANES), lambda i: (i, 0, 0)),
                  pl.BlockSpec((tb, nk), lambda i: (i, 0)),
                  pl.BlockSpec((SUBLANES, LANES), lambda i: (0, 0)),
                  pl.BlockSpec((SUBLANES, LANES), lambda i: (0, 0)),
                  pl.BlockSpec(memory_space=pl.ANY)],
        out_specs=pl.BlockSpec((tb, SUBLANES, LANES), lambda i: (i, 0, 0)),
        out_shape=jax.ShapeDtypeStruct((n, SUBLANES, LANES), F32),
        scratch_shapes=[pltpu.VMEM((PEER_SLOTS, tb * nk, 2 * SUBLANES, LANES), F32),
                        pltpu.SemaphoreType.DMA((PEER_SLOTS,)),
                        pltpu.VMEM((nk, tb), F32),
                        pltpu.VMEM((tb, nk, LANES), F32)],
        compiler_params=_params("arbitrary"),
        name="peer",
    )(eidx, eidx, eidx, x1s, gate, gffn8, gfin8, table)


def _overlap_t(t):
    nc = (t - CMP_LEN) // CMP_STRIDE + 1
    ns = t // SEL_BLOCK
    ncp = t // CMP_STRIDE
    cstart = np.arange(nc) * CMP_STRIDE
    bstart = np.arange(ns) * SEL_BLOCK
    lo = np.maximum(cstart[:, None], bstart[None, :])
    hi = np.minimum(cstart[:, None] + CMP_LEN, bstart[None, :] + SEL_BLOCK)
    ov = np.clip(hi - lo, 0, None).astype(np.float32) / CMP_LEN
    out = np.zeros((ns, ncp), np.float32)
    out[:, :nc] = ov.T
    return out, nc


def _gate_expand():
    ex = np.zeros((NSA_N_GATES, LANES, NSA_HEADS * HEAD_DIM), np.float32)
    for j in range(NSA_N_GATES):
        for h in range(NSA_HEADS):
            ex[j, h * NSA_N_GATES + j, h * HEAD_DIM:(h + 1) * HEAD_DIM] = 1.0
    return ex


def _layer(x, positions, norm_attn, w_in, attn_sinks, cmp_k_pos, cmp_k_w1, cmp_k_w2, cmp_v_pos, cmp_v_w1,
           cmp_v_w2, w_out, norm_ffn, w_query, sub_keys_1, sub_keys_2, expert_down, expert_up, norm_out,
           tm_in=512, tq_sel=512, tk_sel=512, tq_swa=128, tq_win=256, tm_route=256, tb_peer=16):
    b, t, d = x.shape
    n = b * t
    x2 = x.reshape(n, d)
    half = HEAD_DIM // 2
    inv_freq = ROPE_THETA ** (-jnp.arange(half, dtype=F32) / half)
    invf = jnp.tile(inv_freq, LANES // half).reshape(1, LANES)
    w_pad = jnp.pad(w_in, ((0, 0), (0, IN_PAD - IN_WIDTH))).astype(BF16)
    pr = _inproj(x2, positions.reshape(n, 1), norm_attn.reshape(1, d), w_pad, invf, min(tm_in, n))
    r3 = lambda a: a.reshape(b, t, a.shape[-1])

    o_a = _band(r3(pr["qa"]), r3(pr["ka"]), r3(pr["va"]), attn_sinks.astype(F32), SWA_WINDOW, tq_swa,
                SWA_HEADS, SWA_KV_HEADS)
    kcmp, vcmp = _compress(r3(pr["kc"]), r3(pr["vc"]),
                           cmp_k_pos.reshape(1, -1), cmp_k_w1.astype(BF16), cmp_k_w2.astype(BF16),
                           cmp_v_pos.reshape(1, -1), cmp_v_w1.astype(BF16), cmp_v_w2.astype(BF16))
    ovt, n_cmp = _overlap_t(t)
    qb3 = r3(pr["qb"])
    vst4 = jnp.swapaxes(r3(pr["vs"]).reshape(b, t // tk_sel, tk_sel, LANES), 2, 3)
    o_cmp, o_sel = _nsa(qb3, kcmp, vcmp, jnp.asarray(ovt), r3(pr["ks"]), vst4, tq_sel, tk_sel, n_cmp)
    o_win = _band(qb3, r3(pr["kw"]), r3(pr["vw"]), None, NSA_WINDOW, tq_win, NSA_HEADS, NSA_KV_HEADS)

    x1, eidx, gate = _route(x2, o_a.reshape(n, -1), o_cmp.reshape(n, -1), o_sel.reshape(n, -1),
                            o_win.reshape(n, -1), pr["gates"], jnp.asarray(_gate_expand(), BF16),
                            w_out.astype(BF16), norm_ffn.reshape(1, d), w_query.astype(BF16),
                            sub_keys_1.astype(BF16), sub_keys_2.astype(BF16), min(tm_route, n))
    ne = expert_down.shape[0]
    table = jnp.stack([expert_down.reshape(ne, SUBLANES, LANES), expert_up.reshape(ne, SUBLANES, LANES)],
                      axis=1).reshape(ne, 2 * SUBLANES, LANES)
    y = _peer(eidx, x1, gate, norm_ffn.reshape(SUBLANES, LANES),
              norm_out.reshape(SUBLANES, LANES), table, tb_peer)
    return y.reshape(b, t, d)


def kernel(x, positions, norm_attn, w_in, attn_sinks, cmp_k_pos, cmp_k_w1, cmp_k_w2, cmp_v_pos, cmp_v_w1,
           cmp_v_w2, w_out, norm_ffn, peer_w_query, peer_sub_keys_1, peer_sub_keys_2, peer_expert_down,
           peer_expert_up, norm_f):
    assert norm_attn.shape[0] == 1, "single-layer block"
    return _layer(x, positions, norm_attn[0], w_in[0], attn_sinks[0], cmp_k_pos[0], cmp_k_w1[0], cmp_k_w2[0],
                  cmp_v_pos[0], cmp_v_w1[0], cmp_v_w2[0], w_out[0], norm_ffn[0], peer_w_query[0],
                  peer_sub_keys_1[0], peer_sub_keys_2[0], peer_expert_down[0], peer_expert_up[0], norm_f)
```

```python
import functools

import numpy as np
import jax
import jax.numpy as jnp
from jax import lax
from jax.experimental import pallas as pl
from jax.experimental.pallas import tpu as pltpu

F32 = jnp.float32
BF16 = jnp.bfloat16
I32 = jnp.int32

HEAD_DIM = 64
ROPE_THETA = 10000.0
RMS_EPS = 1e-6
NEG_BIG = -1e30

SWA_HEADS = 8
SWA_KV_HEADS = 2
SWA_WINDOW = 128

NSA_HEADS = 8
NSA_KV_HEADS = 2
CMP_LEN = 32
CMP_STRIDE = 16
CMP_HIDDEN = 256
SEL_BLOCK = 64
SEL_TOPN = 16
NSA_WINDOW = 512
NSA_N_GATES = 3

PEER_HEADS = 8
N_KEYS = 128
PEER_QDIM = 256
PEER_TOPK = 16
PEER_SLOTS = 3

LANES = 128
SUBLANES = 8
VMEM_LIMIT = 56 * 1024 * 1024

_IN_SPLITS = (("qa", 512), ("ka", 128), ("va", 128), ("qb", 512), ("kc", 128), ("vc", 128),
              ("ks", 128), ("vs", 128), ("kw", 128), ("vw", 128), ("gl", 24))
IN_WIDTH = sum(w for _, w in _IN_SPLITS)
IN_PAD = -(-IN_WIDTH // LANES) * LANES
_IN_OFF = {}
_o = 0
for _n, _w in _IN_SPLITS:
    _IN_OFF[_n] = _o
    _o += _w
_ROPED = ("qa", "ka", "qb", "kc", "ks", "kw")


def _gelu(x):
    return 0.5 * x * (1.0 + lax.erf(x * 0.7071067811865476))


def _params(*sem):
    return pltpu.CompilerParams(dimension_semantics=sem, vmem_limit_bytes=VMEM_LIMIT)


def _inproj_kernel(x_ref, pos_ref, g_ref, w_ref, invf_ref,
                   qa_ref, ka_ref, va_ref, qb_ref, kc_ref, vc_ref, ks_ref, vs_ref, kw_ref, vw_ref, gt_ref):
    x = x_ref[...]
    h = x * lax.rsqrt(jnp.mean(x * x, axis=-1, keepdims=True) + RMS_EPS) * g_ref[...]
    proj = jnp.dot(h.astype(BF16), w_ref[...], preferred_element_type=F32)
    ang = pos_ref[...].astype(F32) * invf_ref[...]
    lane = lax.broadcasted_iota(I32, ang.shape, 1)
    lo = (lane & (HEAD_DIM - 1)) < (HEAD_DIM // 2)
    cos = jnp.cos(ang)
    sin = jnp.sin(ang)
    sin_s = jnp.where(lo, -sin, sin)
    outs = {"qa": qa_ref, "ka": ka_ref, "va": va_ref, "qb": qb_ref, "kc": kc_ref, "vc": vc_ref,
            "ks": ks_ref, "vs": vs_ref, "kw": kw_ref, "vw": vw_ref}
    for name, width in _IN_SPLITS[:-1]:
        off = _IN_OFF[name]
        ref = outs[name]
        for i in range(width // LANES):
            v = proj[:, off + LANES * i: off + LANES * (i + 1)]
            if name in _ROPED:
                rot = jnp.where(lo, pltpu.roll(v, LANES - HEAD_DIM // 2, 1), pltpu.roll(v, HEAD_DIM // 2, 1))
                v = v * cos + rot * sin_s
            ref[:, LANES * i: LANES * (i + 1)] = v.astype(ref.dtype)
    gl = proj[:, _IN_OFF["gl"]: _IN_OFF["gl"] + LANES]
    gt_ref[...] = jax.nn.sigmoid(gl)


def _inproj(x2, pos2, g, w_pad, invf, tm):
    n, d = x2.shape
    row = lambda w: pl.BlockSpec((tm, w), lambda i: (i, 0))
    full = lambda a: pl.BlockSpec(a.shape, lambda i: (0,) * a.ndim)
    out_dt = {"qa": BF16, "ka": BF16, "va": BF16, "qb": BF16, "kc": F32, "vc": F32,
              "ks": BF16, "vs": BF16, "kw": BF16, "vw": BF16}
    names = [nm for nm, _ in _IN_SPLITS[:-1]]
    widths = dict(_IN_SPLITS)
    out_shape = [jax.ShapeDtypeStruct((n, widths[nm]), out_dt[nm]) for nm in names]
    out_shape.append(jax.ShapeDtypeStruct((n, LANES), F32))
    out_specs = [row(widths[nm]) for nm in names] + [row(LANES)]
    res = pl.pallas_call(
        _inproj_kernel,
        grid=(n // tm,),
        in_specs=[row(d), row(1), full(g), full(w_pad), full(invf)],
        out_specs=out_specs,
        out_shape=out_shape,
        compiler_params=_params("parallel"),
        name="inproj",
    )(x2, pos2, g, w_pad, invf)
    out = dict(zip(names, res[:-1]))
    out["gates"] = res[-1]
    return out


def _compress_kernel(kc_ref, vc_ref, kpos_ref, kw1_ref, kw2_ref, vpos_ref, vw1_ref, vw2_ref, ko_ref, vo_ref):
    ncp = ko_ref.shape[0]
    for src, pos_ref, w1_ref, w2_ref, out_ref in ((kc_ref, kpos_ref, kw1_ref, kw2_ref, ko_ref),
                                                  (vc_ref, vpos_ref, vw1_ref, vw2_ref, vo_ref)):
        bias = jnp.dot(pos_ref[...].astype(BF16), w1_ref[...], preferred_element_type=F32)
        acc_a = [jnp.zeros((ncp, CMP_HIDDEN), F32) for _ in range(2)]
        acc_b = [jnp.zeros((ncp, CMP_HIDDEN), F32) for _ in range(2)]
        for j in range(CMP_STRIDE):
            pj = src[pl.ds(j, ncp, stride=CMP_STRIDE), :].astype(BF16)
            wa = w1_ref[j * HEAD_DIM:(j + 1) * HEAD_DIM, :]
            wb = w1_ref[(CMP_STRIDE + j) * HEAD_DIM:(CMP_STRIDE + j + 1) * HEAD_DIM, :]
            for hh in range(2):
                ph = pj[:, hh * HEAD_DIM:(hh + 1) * HEAD_DIM]
                acc_a[hh] = acc_a[hh] + jnp.dot(ph, wa, preferred_element_type=F32)
                acc_b[hh] = acc_b[hh] + jnp.dot(ph, wb, preferred_element_type=F32)
        for hh in range(2):
            h1 = acc_a[hh] + pltpu.roll(acc_b[hh], ncp - 1, 0) + bias
            out = jnp.dot(_gelu(h1).astype(BF16), w2_ref[...], preferred_element_type=F32)
            out_ref[:, hh * HEAD_DIM:(hh + 1) * HEAD_DIM] = out


def _compress(kc3, vc3, kpos, kw1, kw2, vpos, vw1, vw2):
    b, t, w = kc3.shape
    ncp = t // CMP_STRIDE
    per_b = pl.BlockSpec((None, t, w), lambda i: (i, 0, 0))
    full = lambda a: pl.BlockSpec(a.shape, lambda i: (0,) * a.ndim)
    out_spec = pl.BlockSpec((None, ncp, w), lambda i: (i, 0, 0))
    return pl.pallas_call(
        _compress_kernel,
        grid=(b,),
        in_specs=[per_b, per_b, full(kpos), full(kw1), full(kw2), full(vpos), full(vw1), full(vw2)],
        out_specs=[out_spec, out_spec],
        out_shape=[jax.ShapeDtypeStruct((b, ncp, w), F32)] * 2,
        compiler_params=_params("parallel"),
        name="compress",
    )(kc3, vc3, kpos, kw1, kw2, vpos, vw1, vw2)


def _cmpsel_kernel(q_ref, kc_ref, vc_ref, ovt_ref, o_ref, sel_ref, *, tq, n_cmp):
    i = pl.program_id(1)
    t0 = i * tq
    g = NSA_HEADS // NSA_KV_HEADS
    ncp = kc_ref.shape[0]
    ns = ovt_ref.shape[0]
    rows = g * tq
    tpos = t0 + (lax.broadcasted_iota(I32, (rows, ncp), 0) & (tq - 1))
    cidx = lax.broadcasted_iota(I32, (rows, ncp), 1)
    vis = (cidx * CMP_STRIDE + (CMP_LEN - 1) <= tpos) & (cidx < n_cmp)
    jidx = lax.broadcasted_iota(I32, (ns, tq), 0)
    tcol = t0 + lax.broadcasted_iota(I32, (ns, tq), 1)
    cur = tcol // SEL_BLOCK
    forced = (jidx == 0) | (jidx == cur) | (jidx == cur - 1)
    valid = jidx * SEL_BLOCK <= tcol
    for hk in range(NSA_KV_HEADS):
        kk = kc_ref[:, hk * HEAD_DIM:(hk + 1) * HEAD_DIM].astype(BF16)
        vv = vc_ref[:, hk * HEAD_DIM:(hk + 1) * HEAD_DIM].astype(BF16)
        q4 = jnp.concatenate(
            [q_ref[:, (hk * g + gi) * HEAD_DIM:(hk * g + gi + 1) * HEAD_DIM] for gi in range(g)], axis=0)
        s = lax.dot_general(q4, kk, (((1,), (1,)), ((), ())), preferred_element_type=F32) * (HEAD_DIM ** -0.5)
        s = jnp.where(vis, s, NEG_BIG)
        m = jnp.max(s, axis=-1, keepdims=True)
        e = jnp.exp(s - m)
        p = jnp.where(vis, e / jnp.sum(e, axis=-1, keepdims=True), 0.0)
        o = jnp.dot(p.astype(BF16), vv, preferred_element_type=F32)
        for gi in range(g):
            h = hk * g + gi
            o_ref[:, h * HEAD_DIM:(h + 1) * HEAD_DIM] = o[gi * tq:(gi + 1) * tq]
        psum = p[0:tq]
        for gi in range(1, g):
            psum = psum + p[gi * tq:(gi + 1) * tq]
        imp_t = lax.dot_general(ovt_ref[...], psum, (((1,), (1,)), ((), ())),
                                precision=lax.Precision.HIGHEST, preferred_element_type=F32)
        score = jnp.where(forced, jnp.inf, jnp.where(valid, imp_t, -jnp.inf))
        cnt = jnp.zeros((ns, tq), F32)
        for r in range(ns):
            row = score[r:r + 1, :]
            beats = (row > score) | ((row >= score) & (jidx > r))
            cnt = cnt + jnp.where(beats, 1.0, 0.0)
        sel_ref[hk] = jnp.where((cnt < SEL_TOPN) & (score > -jnp.inf), 1.0, 0.0)


def _selattn_kernel(q_ref, k_ref, vt_ref, sel_ref, o_ref, m_sc, l_sc, acc_sc, *, tq, tk):
    i = pl.program_id(1)
    t0 = i * tq
    g = NSA_HEADS // NSA_KV_HEADS
    rows = g * tq
    assert tk % tq == 0, "one key tile must cover the whole diagonal of a query tile"
    n_below = (i * tq) // tk
    nb = tk // SEL_BLOCK
    qpos = t0 + (lax.broadcasted_iota(I32, (tk, rows), 1) & (tq - 1))
    krow = lax.broadcasted_iota(I32, (tk, rows), 0)
    for hk in range(NSA_KV_HEADS):
        qt = jnp.concatenate(
            [jnp.transpose(q_ref[:, (hk * g + gi) * HEAD_DIM:(hk * g + gi + 1) * HEAD_DIM].astype(F32))
             for gi in range(g)], axis=1)
        qt = (qt * (HEAD_DIM ** -0.5)).astype(BF16)
        m_sc[...] = jnp.full(m_sc.shape, NEG_BIG, F32)
        l_sc[...] = jnp.zeros(l_sc.shape, F32)
        acc_sc[...] = jnp.zeros(acc_sc.shape, F32)

        def tile(kt, diagonal):
            ks = pl.multiple_of(kt * tk, tk)
            kk = k_ref[pl.ds(ks, tk), hk * HEAD_DIM:(hk + 1) * HEAD_DIM]
            s = jnp.dot(kk, qt, preferred_element_type=F32)
            bias = jnp.concatenate(
                [jnp.broadcast_to(
                    jnp.concatenate([(1.0 - sel_ref[hk, pl.ds(kt * nb + jj, 1), :]) * NEG_BIG] * g, axis=1),
                    (SEL_BLOCK, rows)) for jj in range(nb)], axis=0)
            s = s + bias
            if diagonal:
                s = jnp.where(ks + krow <= qpos, s, NEG_BIG)
            m_old = m_sc[...]
            m_new = jnp.maximum(m_old, jnp.max(s, axis=0, keepdims=True))
            alpha = jnp.exp(m_old - m_new)
            p = jnp.exp(s - m_new)
            l_sc[...] = alpha * l_sc[...] + jnp.sum(p, axis=0, keepdims=True)
            vt = vt_ref[kt, hk * HEAD_DIM:(hk + 1) * HEAD_DIM, :]
            acc_sc[...] = alpha * acc_sc[...] + jnp.dot(vt, p.astype(BF16), preferred_element_type=F32)
            m_sc[...] = m_new

        def below_diagonal(kt, carry):
            tile(kt, False)
            return carry

        lax.fori_loop(0, n_below, below_diagonal, 0)
        tile(n_below, True)
        o_t = acc_sc[...] / l_sc[...]
        for gi in range(g):
            h = hk * g + gi
            o_ref[:, h * HEAD_DIM:(h + 1) * HEAD_DIM] = jnp.transpose(o_t[:, gi * tq:(gi + 1) * tq])


def _nsa_kernel(q_ref, kc_ref, vc_ref, ovt_ref, k_ref, vt_ref, ocmp_ref, osel_ref, sel_sc, m_sc, l_sc, acc_sc,
                *, tq, tk, n_cmp):
    _cmpsel_kernel(q_ref, kc_ref, vc_ref, ovt_ref, ocmp_ref, sel_sc, tq=tq, n_cmp=n_cmp)
    _selattn_kernel(q_ref, k_ref, vt_ref, sel_sc, osel_ref, m_sc, l_sc, acc_sc, tq=tq, tk=tk)


def _nsa(qb3, kcmp, vcmp, ovt, ks3, vst4, tq, tk, n_cmp):
    b, t, w = qb3.shape
    g = NSA_HEADS // NSA_KV_HEADS
    ncp = kcmp.shape[1]
    ns = ovt.shape[0]
    kern = functools.partial(_nsa_kernel, tq=tq, tk=tk, n_cmp=n_cmp)
    per_q = pl.BlockSpec((None, tq, w), lambda bi, i: (bi, i, 0))
    return pl.pallas_call(
        kern,
        grid=(b, t // tq),
        in_specs=[per_q,
                  pl.BlockSpec((None, ncp, LANES), lambda bi, i: (bi, 0, 0)),
                  pl.BlockSpec((None, ncp, LANES), lambda bi, i: (bi, 0, 0)),
                  pl.BlockSpec(ovt.shape, lambda bi, i: (0, 0)),
                  pl.BlockSpec((None, t, LANES), lambda bi, i: (bi, 0, 0)),
                  pl.BlockSpec((None, t // tk, LANES, tk), lambda bi, i: (bi, 0, 0, 0))],
        out_specs=[per_q, per_q],
        out_shape=[jax.ShapeDtypeStruct((b, t, w), F32)] * 2,
        scratch_shapes=[pltpu.VMEM((NSA_KV_HEADS, ns, tq), F32),
                        pltpu.VMEM((1, g * tq), F32), pltpu.VMEM((1, g * tq), F32),
                        pltpu.VMEM((HEAD_DIM, g * tq), F32)],
        compiler_params=_params("parallel", "arbitrary"),
        name="nsa_cmp_sel",
    )(qb3, kcmp, vcmp, ovt, ks3, vst4)


def _band_kernel(*refs, window, tq, span, n_heads, n_kv, has_sink):
    if has_sink:
        sink_ref, q_ref, k_ref, vt_ref, o_ref = refs
    else:
        q_ref, k_ref, vt_ref, o_ref = refs
    i = pl.program_id(1)
    t0 = i * tq
    start = pl.multiple_of(jnp.maximum(t0 + tq - span, 0), tq)
    tile0 = start // LANES
    g = n_heads // n_kv
    rows = g * tq
    qpos = t0 + (lax.broadcasted_iota(I32, (span, rows), 1) & (tq - 1))
    kpos = start + lax.broadcasted_iota(I32, (span, rows), 0)
    diff = qpos - kpos
    mask = (diff >= 0) & (diff < window)
    for hk in range(n_kv):
        kk = k_ref[pl.ds(start, span), hk * HEAD_DIM:(hk + 1) * HEAD_DIM]
        qt = jnp.concatenate(
            [jnp.transpose(q_ref[:, (hk * g + gi) * HEAD_DIM:(hk * g + gi + 1) * HEAD_DIM].astype(F32))
             for gi in range(g)], axis=1)
        qt = (qt * (HEAD_DIM ** -0.5)).astype(BF16)
        s = jnp.where(mask, jnp.dot(kk, qt, preferred_element_type=F32), NEG_BIG)
        m = jnp.max(s, axis=0, keepdims=True)
        if has_sink:
            sink = jnp.concatenate([jnp.full((1, tq), sink_ref[hk * g + gi], F32) for gi in range(g)], axis=1)
            m = jnp.maximum(m, sink)
        p = jnp.exp(s - m)
        l = jnp.sum(p, axis=0, keepdims=True)
        if has_sink:
            l = l + jnp.exp(sink - m)
        vt = jnp.concatenate([vt_ref[tile0 + j, hk * HEAD_DIM:(hk + 1) * HEAD_DIM, :]
                              for j in range(span // LANES)], axis=1)
        o_t = jnp.dot(vt, p.astype(BF16), preferred_element_type=F32) / l
        for gi in range(g):
            h = hk * g + gi
            o_ref[:, h * HEAD_DIM:(h + 1) * HEAD_DIM] = jnp.transpose(o_t[:, gi * tq:(gi + 1) * tq])


def _band_pair_kernel(sink_ref, qa_ref, ka_ref, vat_ref, qb_ref, kw_ref, vwt_ref, oa_ref, ow_ref, *, tq, span_a, span_w):
    _band_kernel(sink_ref, qa_ref, ka_ref, vat_ref, oa_ref, window=SWA_WINDOW, tq=tq, span=span_a,
                 n_heads=SWA_HEADS, n_kv=SWA_KV_HEADS, has_sink=True)
    _band_kernel(qb_ref, kw_ref, vwt_ref, ow_ref, window=NSA_WINDOW, tq=tq, span=span_w,
                 n_heads=NSA_HEADS, n_kv=NSA_KV_HEADS, has_sink=False)


def _band_pair(qa3, ka3, va3, sinks, qb3, kw3, vw3, tq):
    b, t, w = qa3.shape
    span = lambda window: min((-(-(window - 1) // tq) + 1) * tq, t)
    tiles = lambda v3: jnp.swapaxes(v3.reshape(b, t // LANES, LANES, LANES), 2, 3)
    kern = functools.partial(_band_pair_kernel, tq=tq, span_a=span(SWA_WINDOW), span_w=span(NSA_WINDOW))
    per_q = pl.BlockSpec((None, tq, w), lambda bi, i: (bi, i, 0))
    per_b = pl.BlockSpec((None, t, LANES), lambda bi, i: (bi, 0, 0))
    per_bt = pl.BlockSpec((None, t // LANES, LANES, LANES), lambda bi, i: (bi, 0, 0, 0))
    return pl.pallas_call(
        kern,
        grid=(b, t // tq),
        in_specs=[pl.BlockSpec(memory_space=pltpu.SMEM), per_q, per_b, per_bt, per_q, per_b, per_bt],
        out_specs=[per_q, per_q],
        out_shape=[jax.ShapeDtypeStruct((b, t, w), F32)] * 2,
        compiler_params=_params("parallel", "parallel"),
        name="band_pair",
    )(sinks, qa3, ka3, tiles(va3), qb3, kw3, tiles(vw3))


def _first_max_rows(val, payload=None):
    rows, n = val.shape
    sub = lax.broadcasted_iota(I32, (SUBLANES, n), 0).astype(F32)
    parts = [[val[r:r + SUBLANES] for r in range(0, rows, SUBLANES)],
             [sub + float(r) for r in range(0, rows, SUBLANES)]]
    if payload is not None:
        parts.append([payload[r:r + SUBLANES] for r in range(0, rows, SUBLANES)])
    while len(parts[0]) > 1:
        nxt = [[] for _ in parts]
        for j in range(0, len(parts[0]) - 1, 2):
            take_right = parts[0][j + 1] > parts[0][j]
            for dst, src in zip(nxt, parts):
                dst.append(jnp.where(take_right, src[j + 1], src[j]))
        if len(parts[0]) % 2:
            for dst, src in zip(nxt, parts):
                dst.append(src[-1])
        parts = nxt
    v8, i8 = parts[0][0], parts[1][0]
    m = jnp.max(v8, axis=0, keepdims=True)
    first = jnp.min(jnp.where(v8 == m, i8, float(rows)), axis=0, keepdims=True)
    if payload is None:
        return m, first
    return m, first, jnp.max(jnp.where(i8 == first, parts[2][0], -1.0), axis=0, keepdims=True)


def _topk_rows(s, k, ridx):
    vals, idxs = [], []
    for _ in range(k):
        m, first = _first_max_rows(s)
        vals.append(m)
        idxs.append(first)
        s = jnp.where(ridx == first, -jnp.inf, s)
    return jnp.concatenate(vals, axis=0), jnp.concatenate(idxs, axis=0)


def _peer_candidates(t1, i1, t2, i2):
    vals = [t1[0:1] + t2]
    eids = [i1[0:1] * float(N_KEYS) + i2]
    for a in range(1, 8):
        vals.append(t1[a:a + 1] + t2[0:8])
        eids.append(i1[a:a + 1] * float(N_KEYS) + i2[0:8])
    vals.append(t1[8:16] + t2[0:1])
    eids.append(i1[8:16] * float(N_KEYS) + i2[0:1])
    return jnp.concatenate(vals, axis=0), jnp.concatenate(eids, axis=0)


def _route_kernel(x_ref, oa_ref, oc_ref, os_ref, ow_ref, gt_ref, gexp_ref, wo_ref, gf_ref, wq_ref,
                  k1_ref, k2_ref, x1_ref, eidx_ref, gate_ref):
    tm = x_ref.shape[0]
    gt = gt_ref[...]
    g_hi = gt.astype(BF16)
    g_lo = (gt - g_hi.astype(F32)).astype(BF16)
    ob = jnp.zeros(oc_ref.shape, F32)
    for j, br in enumerate((oc_ref, os_ref, ow_ref)):
        ex = gexp_ref[j]
        gj = (jnp.dot(g_hi, ex, preferred_element_type=F32) + jnp.dot(g_lo, ex, preferred_element_type=F32))
        ob = ob + gj * br[...]
    half = oa_ref.shape[1]
    mixed = (jnp.dot(oa_ref[...].astype(BF16), wo_ref[0:half, :], preferred_element_type=F32)
             + jnp.dot(ob.astype(BF16), wo_ref[half:, :], preferred_element_type=F32))
    x1 = x_ref[...] + mixed
    x1_ref[...] = x1.reshape(tm, SUBLANES, LANES)
    h2 = x1 * lax.rsqrt(jnp.mean(x1 * x1, axis=-1, keepdims=True) + RMS_EPS) * gf_ref[...]
    q = jnp.dot(h2.astype(BF16), wq_ref[...], preferred_element_type=F32)
    hq = PEER_QDIM // 2
    ridx = lax.broadcasted_iota(I32, (N_KEYS, tm), 0).astype(F32)
    e_rows, g_rows = [], []
    for h in range(PEER_HEADS):
        q1 = q[:, h * PEER_QDIM: h * PEER_QDIM + hq].astype(BF16)
        q2 = q[:, h * PEER_QDIM + hq:(h + 1) * PEER_QDIM].astype(BF16)
        s1 = lax.dot_general(k1_ref[...], q1, (((1,), (1,)), ((), ())), preferred_element_type=F32)
        s2 = lax.dot_general(k2_ref[...], q2, (((1,), (1,)), ((), ())), preferred_element_type=F32)
        t1, i1 = _topk_rows(s1, PEER_TOPK, ridx)
        t2, i2 = _topk_rows(s2, PEER_TOPK, ridx)
        cand, cand_e = _peer_candidates(t1, i1, t2, i2)
        cidx = lax.broadcasted_iota(I32, cand.shape, 0).astype(F32)
        sc, ee = [], []
        for _ in range(PEER_TOPK):
            m, first, e_first = _first_max_rows(cand, cand_e)
            sc.append(m)
            ee.append(e_first)
            cand = jnp.where(cidx == first, -jnp.inf, cand)
        sc = jnp.concatenate(sc, axis=0)
        ex = jnp.exp(sc - sc[0:1])
        g_rows.append(ex / jnp.sum(ex, axis=0, keepdims=True))
        e_rows.append(jnp.concatenate(ee, axis=0))
    e_t = jnp.concatenate(e_rows, axis=0)
    g_t = jnp.concatenate(g_rows, axis=0)
    eidx_ref[...] = jnp.transpose(e_t).astype(I32)
    gate_ref[...] = jnp.transpose(g_t)


def _route(x2, oa, oc, osel, ow, gates, gexp, wo, gf, wq, k1, k2, tm):
    n, d = x2.shape
    row = lambda w: pl.BlockSpec((tm, w), lambda i: (i, 0))
    full = lambda a: pl.BlockSpec(a.shape, lambda i: (0,) * a.ndim)
    nk = PEER_HEADS * PEER_TOPK
    return pl.pallas_call(
        _route_kernel,
        grid=(n // tm,),
        in_specs=[row(d), row(oa.shape[1]), row(oc.shape[1]), row(osel.shape[1]), row(ow.shape[1]), row(LANES),
                  full(gexp), full(wo), full(gf), full(wq), full(k1), full(k2)],
        out_specs=[pl.BlockSpec((tm, SUBLANES, LANES), lambda i: (i, 0, 0)), row(nk), row(nk)],
        out_shape=[jax.ShapeDtypeStruct((n, SUBLANES, LANES), F32), jax.ShapeDtypeStruct((n, nk), I32),
                   jax.ShapeDtypeStruct((n, nk), F32)],
        compiler_params=_params("parallel"),
        name="route",
    )(x2, oa, oc, osel, ow, gates, gexp, wo, gf, wq, k1, k2)


def _fold8(p):
    sub = lax.broadcasted_iota(I32, (SUBLANES, LANES), 0)
    lo4 = sub < 4
    q = [jnp.where(lo4, p[j], p[j + 4]) + pltpu.roll(jnp.where(lo4, p[j + 4], p[j]), 4, 0) for j in range(4)]
    m2 = (sub & 3) < 2
    r = [jnp.where(m2, q[j] + pltpu.roll(q[j], 6, 0), q[j + 2] + pltpu.roll(q[j + 2], 2, 0)) for j in range(2)]
    m1 = (sub & 1) == 0
    return jnp.where(m1, r[0] + pltpu.roll(r[0], 7, 0), r[1] + pltpu.roll(r[1], 1, 0))


def _peer_kernel(e0_ref, e1_ref, e2_ref, x_ref, gate_ref, gffn_ref, gfin_ref, tab_ref, o_ref,
                 buf, sem, a_sc, cb_sc, *, tb):
    step = pl.program_id(0)
    nsteps = pl.num_programs(0)
    slot = step % PEER_SLOTS
    fill = (step + PEER_SLOTS - 1) % PEER_SLOTS
    nk = PEER_HEADS * PEER_TOPK
    npair = tb * nk

    def issue_rows(eref, dst_slot, t, k0, k1):
        erow = eref.at[t]
        for k in range(k0, k1):
            copy = pltpu.make_async_copy(tab_ref.at[erow[k]], buf.at[dst_slot, t * nk + k], sem.at[dst_slot])
            copy.start(priority=k % 2)

    def wait_slot(s):
        pltpu.make_async_copy(tab_ref.at[pl.ds(0, npair)], buf.at[s], sem.at[s]).wait()

    @pl.when(step == 0)
    def _():
        def first(t, carry):
            issue_rows(e0_ref, 0, t, 0, nk)
            return carry
        lax.fori_loop(0, tb, first, 0)
        def second(t, carry):
            issue_rows(e1_ref, 1, t, 0, nk)
            return carry
        lax.fori_loop(0, tb, second, 0)

    eye = jnp.where(lax.broadcasted_iota(I32, (nk, nk), 0) == lax.broadcasted_iota(I32, (nk, nk), 1),
                    1.0, 0.0).astype(BF16)
    gr = gate_ref[...]
    g0 = gr.astype(BF16)
    r1 = gr - g0.astype(F32)
    g1 = r1.astype(BF16)
    g2 = (r1 - g1.astype(F32)).astype(BF16)
    dn = (((1,), (1,)), ((), ()))
    gate_t = (lax.dot_general(eye, g0, dn, preferred_element_type=F32)
              + lax.dot_general(eye, g1, dn, preferred_element_type=F32)
              + lax.dot_general(eye, g2, dn, preferred_element_type=F32))

    wait_slot(slot)

    lane_tb = lax.broadcasted_iota(I32, (nk, tb), 1)
    inv_d = 1.0 / (SUBLANES * LANES)
    a_sc[...] = jnp.zeros(a_sc.shape, F32)

    ngroup = nk // SUBLANES
    rows_p1 = 6
    rows_p2 = SUBLANES - rows_p1

    def dots(t, carry):
        x8 = x_ref[t]
        ms = jnp.sum(jnp.sum(x8 * x8, axis=1, keepdims=True), axis=0, keepdims=True) * inv_d
        h8 = x8 * lax.rsqrt(ms + RMS_EPS) * gffn_ref[...]
        base = t * nk
        folded = []
        for gi in range(nk // SUBLANES):
            blk = buf[slot, pl.ds(base + gi * SUBLANES, SUBLANES), 0:SUBLANES, :]
            folded.append(_fold8([blk[j] * h8 for j in range(SUBLANES)]))
            issue_rows(e2_ref, fill, t, gi * rows_p1, (gi + 1) * rows_p1)
        a = jnp.sum(jnp.concatenate(folded, axis=0), axis=1, keepdims=True)
        a_sc[...] = jnp.where(lane_tb == t, a, a_sc[...])
        return carry

    lax.fori_loop(0, tb, dots, 0, unroll=2)

    a_sc[...] = gate_t * _gelu(a_sc[...])

    def combine(t, carry):
        base = t * nk
        w_col = jnp.sum(jnp.where(lane_tb == t, a_sc[...], 0.0), axis=1, keepdims=True)
        cb_sc[t] = jnp.broadcast_to(w_col, (nk, LANES))
        accs = [jnp.zeros((SUBLANES, LANES), F32) for _ in range(4)]
        for gi in range(ngroup):
            for k in range(gi * SUBLANES, (gi + 1) * SUBLANES):
                ck = jnp.broadcast_to(cb_sc[t, k:k + 1, :], (SUBLANES, LANES))
                accs[k % 4] = accs[k % 4] + ck * buf[slot, base + k, SUBLANES:2 * SUBLANES, :]
            issue_rows(e2_ref, fill, t, ngroup * rows_p1 + gi * rows_p2, ngroup * rows_p1 + (gi + 1) * rows_p2)
        y8 = x_ref[t] + ((accs[0] + accs[1]) + (accs[2] + accs[3]))
        ms2 = jnp.sum(jnp.sum(y8 * y8, axis=1, keepdims=True), axis=0, keepdims=True) * inv_d
        o_ref[t] = y8 * lax.rsqrt(ms2 + RMS_EPS) * gfin_ref[...]
        return carry

    lax.fori_loop(0, tb, combine, 0, unroll=8)

    @pl.when(step == nsteps - 1)
    def _():
        for ahead in range(1, PEER_SLOTS):
            wait_slot((step + ahead) % PEER_SLOTS)


def _peer(eidx, x1s, gate, gffn8, gfin8, table, tb):
    n = x1s.shape[0]
    nk = PEER_HEADS * PEER_TOPK
    nsteps = n // tb
    kern = functools.partial(_peer_kernel, tb=tb)
    return pl.pallas_call(
        kern,
        grid=(nsteps,),
        in_specs=[pl.BlockSpec((tb, nk), lambda i: (i, 0), memory_space=pltpu.SMEM),
                  pl.BlockSpec((tb, nk), lambda i: ((i + 1) % nsteps, 0), memory_space=pltpu.SMEM),
                  pl.BlockSpec((tb, nk), lambda i: ((i + PEER_SLOTS - 1) % nsteps, 0), memory_space=pltpu.SMEM),
                  pl.BlockSpec((tb, SUBLANES, LANES), lambda i: (i, 0, 0)),
                  pl.BlockSpec((tb, nk), lambda i: (i, 0)),
                  pl.BlockSpec((SUBLANES, LANES), lambda i: (0, 0)),
                  pl.BlockSpec((SUBLANES, LANES), lambda i: (0, 0)),
                  pl.BlockSpec(memory_space=pl.ANY)],
        out_specs=pl.BlockSpec((tb, SUBLANES, LANES), lambda i: (i, 0, 0)),
        out_shape=jax.ShapeDtypeStruct((n, SUBLANES, LANES), F32),
        scratch_shapes=[pltpu.VMEM((PEER_SLOTS, tb * nk, 2 * SUBLANES, LANES), F32),
                        pltpu.SemaphoreType.DMA((PEER_SLOTS,)),
                        pltpu.VMEM((nk, tb), F32),
                        pltpu.VMEM((tb, nk, LANES), F32)],
        compiler_params=_params("arbitrary"),
        name="peer",
    )(eidx, eidx, eidx, x1s, gate, gffn8, gfin8, table)


def _overlap_t(t):
    nc = (t - CMP_LEN) // CMP_STRIDE + 1
    ns = t // SEL_BLOCK
    ncp = t // CMP_STRIDE
    cstart = np.arange(nc) * CMP_STRIDE
    bstart = np.arange(ns) * SEL_BLOCK
    lo = np.maximum(cstart[:, None], bstart[None, :])
    hi = np.minimum(cstart[:, None] + CMP_LEN, bstart[None, :] + SEL_BLOCK)
    ov = np.clip(hi - lo, 0, None).astype(np.float32) / CMP_LEN
    out = np.zeros((ns, ncp), np.float32)
    out[:, :nc] = ov.T
    return out, nc


def _gate_expand():
    ex = np.zeros((NSA_N_GATES, LANES, NSA_HEADS * HEAD_DIM), np.float32)
    for j in range(NSA_N_GATES):
        for h in range(NSA_HEADS):
            ex[j, h * NSA_N_GATES + j, h * HEAD_DIM:(h + 1) * HEAD_DIM] = 1.0
    return ex


def _layer(x, positions, norm_attn, w_in, attn_sinks, cmp_k_pos, cmp_k_w1, cmp_k_w2, cmp_v_pos, cmp_v_w1,
           cmp_v_w2, w_out, norm_ffn, w_query, sub_keys_1, sub_keys_2, expert_down, expert_up, norm_out,
           tm_in=512, tq_sel=512, tk_sel=512, tq_band=256, tm_route=256, tb_peer=16):
    b, t, d = x.shape
    n = b * t
    x2 = x.reshape(n, d)
    half = HEAD_DIM // 2
    inv_freq = ROPE_THETA ** (-jnp.arange(half, dtype=F32) / half)
    invf = jnp.tile(inv_freq, LANES // half).reshape(1, LANES)
    w_pad = jnp.pad(w_in, ((0, 0), (0, IN_PAD - IN_WIDTH))).astype(BF16)
    pr = _inproj(x2, positions.reshape(n, 1), norm_attn.reshape(1, d), w_pad, invf, min(tm_in, n))
    r3 = lambda a: a.reshape(b, t, a.shape[-1])

    kcmp, vcmp = _compress(r3(pr["kc"]), r3(pr["vc"]),
                           cmp_k_pos.reshape(1, -1), cmp_k_w1.astype(BF16), cmp_k_w2.astype(BF16),
                           cmp_v_pos.reshape(1, -1), cmp_v_w1.astype(BF16), cmp_v_w2.astype(BF16))
    ovt, n_cmp = _overlap_t(t)
    qb3 = r3(pr["qb"])
    vst4 = jnp.swapaxes(r3(pr["vs"]).reshape(b, t // tk_sel, tk_sel, LANES), 2, 3)
    o_cmp, o_sel = _nsa(qb3, kcmp, vcmp, jnp.asarray(ovt), r3(pr["ks"]), vst4, tq_sel, tk_sel, n_cmp)
    o_a, o_win = _band_pair(r3(pr["qa"]), r3(pr["ka"]), r3(pr["va"]), attn_sinks.astype(F32),
                            qb3, r3(pr["kw"]), r3(pr["vw"]), tq_band)

    x1, eidx, gate = _route(x2, o_a.reshape(n, -1), o_cmp.reshape(n, -1), o_sel.reshape(n, -1),
                            o_win.reshape(n, -1), pr["gates"], jnp.asarray(_gate_expand(), BF16),
                            w_out.astype(BF16), norm_ffn.reshape(1, d), w_query.astype(BF16),
                            sub_keys_1.astype(BF16), sub_keys_2.astype(BF16), min(tm_route, n))
    ne = expert_down.shape[0]
    table = jnp.stack([expert_down.reshape(ne, SUBLANES, LANES), expert_up.reshape(ne, SUBLANES, LANES)],
                      axis=1).reshape(ne, 2 * SUBLANES, LANES)
    y = _peer(eidx, x1, gate, norm_ffn.reshape(SUBLANES, LANES),
              norm_out.reshape(SUBLANES, LANES), table, tb_peer)
    return y.reshape(b, t, d)


def kernel(x, positions, norm_attn, w_in, attn_sinks, cmp_k_pos, cmp_k_w1, cmp_k_w2, cmp_v_pos, cmp_v_w1,
           cmp_v_w2, w_out, norm_ffn, peer_w_query, peer_sub_keys_1, peer_sub_keys_2, peer_expert_down,
           peer_expert_up, norm_f):
    assert norm_attn.shape[0] == 1, "single-layer block"
    return _layer(x, positions, norm_attn[0], w_in[0], attn_sinks[0], cmp_k_pos[0], cmp_k_w1[0], cmp_k_w2[0],
                  cmp_v_pos[0], cmp_v_w1[0], cmp_v_w2[0], w_out[0], norm_ffn[0], peer_w_query[0],
                  peer_sub_keys_1[0], peer_sub_keys_2[0], peer_expert_down[0], peer_expert_up[0], norm_f)
```
